```python
import math
import jax, jax.numpy as jnp
from jax import lax
import numpy as np

D_MODEL = 2048
BATCH = 4
SEQ = 2048
DEPTH = 1
DEC_BATCH = 32
DEC_SEQ = 1
PAST_LEN = 8192
PAGE_SIZE = 128

N_META = 16
HG_WIDTH = D_MODEL // 2
AT_WIDTH = D_MODEL - HG_WIDTH
HG_DK = 128
HG_DV = 128
HG_HEADS = HG_WIDTH // HG_DV
HG_KW = HG_HEADS * HG_DK
AT_DQK = 64
AT_DV = 2 * AT_DQK
AT_HEADS = AT_WIDTH // AT_DV
AT_QKW = AT_HEADS * 2 * AT_DQK
N_BUCKETS = 32
MAX_DISTANCE = 128
MAX_EXACT = N_BUCKETS // 2
CHUNK = 64
Q_BLOCK = 128
RMS_EPS = 1e-6
NEG_INF = -1e30
PROJ_WIDTHS = [HG_KW, HG_KW, HG_WIDTH, HG_WIDTH, AT_QKW, AT_QKW, AT_WIDTH, AT_WIDTH]
D_IN = sum(PROJ_WIDTHS)
SPLIT_OFFSETS = [int(v) for v in np.cumsum(PROJ_WIDTHS)[:-1]]

kernel_name = "hymba_hgrn2_diffattn_decode_step"


def rmsnorm(x, g):
    xf = x.astype(jnp.float32)
    y = xf * lax.rsqrt(jnp.mean(xf * xf, axis=-1, keepdims=True) + RMS_EPS) * g.astype(jnp.float32)
    return y.astype(x.dtype)


def lambda_init(layer):
    return 0.8 - 0.6 * math.exp(-0.3 * layer)


def project(h, norm_g, w_in, lb):
    B, T = h.shape[:2]
    z = rmsnorm(h, norm_g) @ w_in
    hq, hf, hi, hg, aq, ak, av, ag = jnp.split(z, SPLIT_OFFSETS, axis=-1)
    f = lb + (1.0 - lb) * jax.nn.sigmoid(hf.astype(jnp.float32))
    log_f = jnp.log(f).reshape(B, T, HG_HEADS, HG_DK)
    hk = (1.0 - f).reshape(B, T, HG_HEADS, HG_DK)
    hq = hq.reshape(B, T, HG_HEADS, HG_DK)
    hv = hi.reshape(B, T, HG_HEADS, HG_DV)
    aq = aq.reshape(B, T, AT_HEADS, 2 * AT_DQK)
    ak = ak.reshape(B, T, AT_HEADS, 2 * AT_DQK)
    av = av.reshape(B, T, AT_HEADS, AT_DV)
    return hq, hk, hv, log_f, hg, aq, ak, av, ag


def hgrn2_chunked(q, k, v, log_f, s0, chunk):
    B, L, H, DK = q.shape
    n = L // chunk
    def blocks(a):
        a = a.astype(jnp.float32)
        return jnp.moveaxis(a.reshape(B, n, chunk, H, a.shape[-1]), 1, 0)
    mask = jnp.tril(jnp.ones((chunk, chunk), dtype=bool))
    def step(S, inp):
        qc, kc, vc, gc = inp
        b = jnp.cumsum(gc, axis=1)
        qe = qc * jnp.exp(b)
        ke = kc * jnp.exp(-b)
        A = jnp.einsum('bthk,bshk->bhts', qe, ke)
        A = jnp.where(mask, A, 0.0)
        o = jnp.einsum('bthk,bhkv->bthv', qe, S) + jnp.einsum('bhts,bshv->bthv', A, vc)
        b_last = b[:, -1]
        S_new = jnp.exp(b_last)[..., None] * S + jnp.einsum(
            'bshk,bshv->bhkv', kc * jnp.exp(b_last[:, None] - b), vc)
        return S_new, o
    S, o = lax.scan(step, s0.astype(jnp.float32), (blocks(q), blocks(k), blocks(v), blocks(log_f)))
    o = jnp.moveaxis(o, 0, 1).reshape(B, L, H, v.shape[-1])
    return S, o


def rel_bucket(dist):
    n = jnp.maximum(dist, 0)
    nl = jnp.maximum(n, MAX_EXACT).astype(jnp.float32)
    large = MAX_EXACT + (jnp.log(nl / MAX_EXACT) / math.log(MAX_DISTANCE / MAX_EXACT)
                         * (N_BUCKETS - MAX_EXACT)).astype(jnp.int32)
    large = jnp.minimum(large, N_BUCKETS - 1)
    return jnp.where(n < MAX_EXACT, n, large)


def diff_attend(q, k, v, q_pos, k_pos, rel_bias, lam):
    B, Tq, H, _ = q.shape
    Tk = k.shape[1]
    qf = q.astype(jnp.float32).reshape(B, Tq, H, 2, AT_DQK)
    kf = k.astype(jnp.float32).reshape(B, Tk, H, 2, AT_DQK)
    s = jnp.einsum('bqhcd,bkhcd->bchqk', qf, kf) * (AT_DQK ** -0.5)
    dist = q_pos[:, None] - k_pos[None, :]
    bias = jnp.transpose(rel_bias[rel_bucket(dist)].astype(jnp.float32), (2, 0, 1))
    s = jnp.where(dist >= 0, s + bias, NEG_INF)
    p = jax.nn.softmax(s, axis=-1)
    w = p[:, 0] - lam * p[:, 1]
    return jnp.einsum('bhqk,bkhv->bqhv', w, v.astype(jnp.float32))


def merge(h, o_hg, g_hg, o_at, g_at, hg_norm_g, at_norm_g, lam_init, w_out):
    B, T = h.shape[:2]
    o_hg = rmsnorm(o_hg, hg_norm_g).reshape(B, T, HG_WIDTH) * jax.nn.silu(g_hg.astype(jnp.float32))
    o_at = (rmsnorm(o_at, at_norm_g) * (1.0 - lam_init)).reshape(B, T, AT_WIDTH) * jax.nn.silu(g_at.astype(jnp.float32))
    mixed = jnp.concatenate([o_hg, o_at], axis=-1).astype(h.dtype)
    return h + mixed @ w_out


def setup_inputs(seed: int = 0) -> dict:
    key = jax.random.key(seed)
    ks = jax.random.split(key, 16)
    n_pages = PAST_LEN // PAGE_SIZE
    n_phys = (5 * DEC_BATCH * n_pages + 3) // 4
    perm = jax.random.permutation(ks[5], n_phys)[:DEC_BATCH * n_pages]
    page_table = perm.reshape(DEC_BATCH, n_pages).astype(jnp.int32)
    f32 = jnp.float32
    return {
        "x_prompt": jax.random.normal(ks[0], (BATCH, SEQ, D_MODEL), f32),
        "x_sample": jax.random.normal(ks[1], (DEC_BATCH, DEC_SEQ, D_MODEL), f32),
        "cache_k": jax.random.normal(ks[2], (DEPTH, n_phys, PAGE_SIZE, AT_HEADS, 2 * AT_DQK), f32),
        "cache_v": jax.random.normal(ks[3], (DEPTH, n_phys, PAGE_SIZE, AT_HEADS, AT_DV), f32),
        "state_hgrn": 0.3 * jax.random.normal(ks[4], (DEPTH, DEC_BATCH, HG_HEADS, HG_DK, HG_DV), f32),
        "page_table": page_table,
        "meta_tokens": jax.random.normal(ks[6], (N_META, D_MODEL), f32),
        "rel_bias": 0.1 * jax.random.normal(ks[7], (N_BUCKETS, AT_HEADS), f32),
        "hgrn_lb": 0.1 * jax.random.normal(ks[8], (DEPTH + 1, HG_KW), f32),
        "norm_g": 1.0 + 0.02 * jax.random.normal(ks[9], (DEPTH, D_MODEL), f32),
        "w_in": jax.random.normal(ks[10], (DEPTH, D_MODEL, D_IN), f32) * D_MODEL ** -0.5,
        "hgrn_norm_g": 1.0 + 0.02 * jax.random.normal(ks[11], (DEPTH, HG_DV), f32),
        "diff_norm_g": 1.0 + 0.02 * jax.random.normal(ks[12], (DEPTH, AT_DV), f32),
        "diff_lambda": 0.1 * jax.random.normal(ks[13], (DEPTH, 4, AT_DQK), f32),
        "w_out": jax.random.normal(ks[14], (DEPTH, D_MODEL, D_MODEL), f32) * D_MODEL ** -0.5,
        "final_norm_g": 1.0 + 0.02 * jax.random.normal(ks[15], (D_MODEL,), f32),
    }


def reference(x_prompt, x_sample, cache_k, cache_v, state_hgrn, page_table, meta_tokens, rel_bias,
              hgrn_lb, norm_g, w_in, hgrn_norm_g, diff_norm_g, diff_lambda, w_out, final_norm_g):
    B = x_prompt.shape[0]
    DB = x_sample.shape[0]
    T = N_META + SEQ
    n_blk = SEQ // Q_BLOCK
    lb_table = jnp.cumsum(jax.nn.softmax(hgrn_lb.astype(jnp.float32), axis=0), axis=0)
    pos_p = jnp.arange(T, dtype=jnp.int32)
    pos_q_blocks = pos_p[N_META:].reshape(n_blk, Q_BLOCK)
    pos_s_q = PAST_LEN + jnp.arange(DEC_SEQ, dtype=jnp.int32)
    pos_s_k = jnp.arange(PAST_LEN + DEC_SEQ, dtype=jnp.int32)

    h_p = jnp.concatenate([jnp.broadcast_to(meta_tokens[None].astype(x_prompt.dtype), (B, N_META, D_MODEL)),
                           x_prompt], axis=1)
    h_s = x_sample
    kp_l, vp_l, sp_l, ks_l, vs_l, ss_l = [], [], [], [], [], []
    for l in range(DEPTH):
        lam_i = lambda_init(l)
        lp = diff_lambda[l].astype(jnp.float32)
        lam = jnp.exp(jnp.sum(lp[0] * lp[1])) - jnp.exp(jnp.sum(lp[2] * lp[3])) + lam_i
        lb = lb_table[l]

        hq, hk, hv, lf, hg, aq, ak, av, ag = project(h_p, norm_g[l], w_in[l], lb)
        s0 = jnp.zeros((B, HG_HEADS, HG_DK, HG_DV), jnp.float32)
        s_meta, o_meta = hgrn2_chunked(hq[:, :N_META], hk[:, :N_META], hv[:, :N_META], lf[:, :N_META], s0, N_META)
        s_p, o_real = hgrn2_chunked(hq[:, N_META:], hk[:, N_META:], hv[:, N_META:], lf[:, N_META:], s_meta, CHUNK)
        o_hg_p = jnp.concatenate([o_meta, o_real], axis=1)
        o_at_meta = diff_attend(aq[:, :N_META], ak[:, :N_META], av[:, :N_META],
                                pos_p[:N_META], pos_p[:N_META], rel_bias, lam)
        qb = jnp.moveaxis(aq[:, N_META:].reshape(B, n_blk, Q_BLOCK, AT_HEADS, 2 * AT_DQK), 1, 0)
        ob = lax.map(lambda a: diff_attend(a[0], ak, av, a[1], pos_p, rel_bias, lam), (qb, pos_q_blocks))
        o_at_real = jnp.moveaxis(ob, 0, 1).reshape(B, SEQ, AT_HEADS, AT_DV)
        o_at_p = jnp.concatenate([o_at_meta, o_at_real], axis=1)
        h_p = merge(h_p, o_hg_p, hg, o_at_p, ag, hgrn_norm_g[l], diff_norm_g[l], lam_i, w_out[l])
        kp_l.append(ak)
        vp_l.append(av)
        sp_l.append(s_p)

        sq, sk, sv, slf, sg, bq, bk, bv, bg = project(h_s, norm_g[l], w_in[l], lb)
        s_s, o_hg_s = hgrn2_chunked(sq, sk, sv, slf, state_hgrn[l], DEC_SEQ)
        k_past = cache_k[l][page_table].reshape(DB, PAST_LEN, AT_HEADS, 2 * AT_DQK)
        v_past = cache_v[l][page_table].reshape(DB, PAST_LEN, AT_HEADS, AT_DV)
        k_all = jnp.concatenate([k_past, bk.astype(k_past.dtype)], axis=1)
        v_all = jnp.concatenate([v_past, bv.astype(v_past.dtype)], axis=1)
        o_at_s = diff_attend(bq, k_all, v_all, pos_s_q, pos_s_k, rel_bias, lam)
        h_s = merge(h_s, o_hg_s, sg, o_at_s, bg, hgrn_norm_g[l], diff_norm_g[l], lam_i, w_out[l])
        ks_l.append(bk)
        vs_l.append(bv)
        ss_l.append(s_s)

    y_prompt = rmsnorm(h_p, final_norm_g)[:, N_META:]
    y_sample = rmsnorm(h_s, final_norm_g)
    k_prompt = jnp.stack(kp_l)
    v_prompt = jnp.stack(vp_l)
    s_prompt = jnp.stack(sp_l)
    k_sample = jnp.stack(ks_l)
    v_sample = jnp.stack(vs_l)
    s_sample = jnp.stack(ss_l)
    return (y_prompt, y_sample, k_prompt, v_prompt, s_prompt, k_sample, v_sample, s_sample)
```

```python
import functools
import math

import numpy as np
import jax
import jax.numpy as jnp
from jax import lax
from jax.experimental import pallas as pl
from jax.experimental.pallas import tpu as pltpu

F32 = jnp.float32
BF16 = jnp.bfloat16

N_META = 16
HEAD_W = 128
AT_DQK = 64
N_BUCKETS = 32
MAX_DISTANCE = 128
MAX_EXACT = N_BUCKETS // 2
HGRN_CHUNK = 64
RMS_EPS = 1e-6
NEG_INF = -1e30
LAMBDA_INIT = 0.8 - 0.6 * math.exp(-0.3 * 0)
G_HQ, G_HF, G_HI, G_HG, G_AQ, G_AK, G_AV, G_AG = range(8)

VMEM_LIMIT = 56 * 1024 * 1024


def _bucket_thresholds():
    n = np.arange(0, 2 * MAX_DISTANCE)
    nl = np.maximum(n, MAX_EXACT).astype(np.float32)
    large = MAX_EXACT + (np.log(nl / np.float32(MAX_EXACT)) / np.float32(math.log(MAX_DISTANCE / MAX_EXACT))
                         * np.float32(N_BUCKETS - MAX_EXACT)).astype(np.int32)
    bucket = np.where(n < MAX_EXACT, n, np.minimum(large, N_BUCKETS - 1))
    return [int(np.argmax(bucket >= b)) for b in range(N_BUCKETS)]


BUCKET_THR = _bucket_thresholds()
CONST_BIAS_DIST = BUCKET_THR[N_BUCKETS - 1]


def _sigmoid(x):
    return 1.0 / (1.0 + jnp.exp(-x))


def _silu(x):
    return x * _sigmoid(x)


def _dot_nt(a, b):
    return lax.dot_general(a, b, (((1,), (1,)), ((), ())), preferred_element_type=F32)


def _dot_tn(a, b):
    return lax.dot_general(a, b, (((0,), (0,)), ((), ())), preferred_element_type=F32)


def _proj_kernel(x_ref, g_ref, w_ref, z_ref, xn_ref):
    @pl.when(pl.program_id(1) == 0)
    def _():
        x = x_ref[...]
        ms = jnp.mean(x * x, axis=-1, keepdims=True)
        xn_ref[...] = (x * lax.rsqrt(ms + RMS_EPS) * g_ref[...]).astype(BF16)

    z_ref[...] = jnp.dot(xn_ref[...], w_ref[...], preferred_element_type=F32)


def _proj(x, g, w_bf, tm, tn):
    rows, d = x.shape
    n = w_bf.shape[1]
    return pl.pallas_call(
        _proj_kernel,
        grid=(rows // tm, n // tn),
        in_specs=[pl.BlockSpec((tm, d), lambda i, j: (i, 0)),
                  pl.BlockSpec((1, d), lambda i, j: (0, 0)),
                  pl.BlockSpec((d, tn), lambda i, j: (0, j))],
        out_specs=pl.BlockSpec((tm, tn), lambda i, j: (i, j)),
        out_shape=jax.ShapeDtypeStruct((rows, n), F32),
        scratch_shapes=[pltpu.VMEM((tm, d), BF16)],
        compiler_params=pltpu.CompilerParams(dimension_semantics=("arbitrary", "arbitrary"),
                                             vmem_limit_bytes=VMEM_LIMIT),
        name="proj",
    )(x, g, w_bf)


def _forget_lower_bound(lbp):
    e = jnp.exp(lbp - jnp.max(lbp, axis=0, keepdims=True))
    return e[0:1] / jnp.sum(e, axis=0, keepdims=True)


def _cumsum_rows(g, tri_bf):
    g1 = g.astype(BF16)
    r1 = g - g1.astype(F32)
    g2 = r1.astype(BF16)
    g3 = (r1 - g2.astype(F32)).astype(BF16)
    dot = functools.partial(jnp.dot, preferred_element_type=F32)
    return dot(tri_bf, g1) + dot(tri_bf, g2) + dot(tri_bf, g3)


def _hgrn_chunk(q, hf, v, lb, st):
    c = q.shape[0]
    row = lax.broadcasted_iota(jnp.int32, (c, c), 0)
    col = lax.broadcasted_iota(jnp.int32, (c, c), 1)
    causal = row >= col
    f = lb + (1.0 - lb) * _sigmoid(hf)
    k = 1.0 - f
    b = _cumsum_rows(jnp.log(f), causal.astype(BF16))
    qe = (q * jnp.exp(b)).astype(BF16)
    ke = (k * jnp.exp(-b)).astype(BF16)
    a = jnp.where(causal, _dot_nt(qe, ke), 0.0)
    v_bf = v.astype(BF16)
    o = _dot_nt(qe, st.astype(BF16)) + jnp.dot(a.astype(BF16), v_bf, preferred_element_type=F32)
    b_last = b[c - 1:c]
    kd = (k * jnp.exp(b_last - b)).astype(BF16)
    st_new = st * jnp.exp(b_last) + _dot_tn(v_bf, kd)
    return o, st_new


def _hgrn_prompt_kernel(q_ref, f_ref, v_ref, qm_ref, fm_ref, vm_ref, lbp_ref, o_ref, s_ref, st_ref):
    lb = _forget_lower_bound(lbp_ref[...])
    _, st_meta = _hgrn_chunk(qm_ref[...], fm_ref[...], vm_ref[...], lb, jnp.zeros((HEAD_W, HEAD_W), F32))
    st_ref[...] = st_meta
    n_chunks = q_ref.shape[1] // HGRN_CHUNK

    def body(ci, carry):
        rows = pl.ds(pl.multiple_of(ci * HGRN_CHUNK, HGRN_CHUNK), HGRN_CHUNK)
        o, st_new = _hgrn_chunk(q_ref[0, rows, :], f_ref[0, rows, :], v_ref[0, rows, :], lb, st_ref[...])
        o_ref[0, rows, :] = o
        st_ref[...] = st_new
        return carry

    lax.fori_loop(0, n_chunks, body, 0)
    s_ref[0, 0] = st_ref[...].T


def _hgrn_prompt(z_p, z_small, hgrn_lb, batch, seq, heads):
    zp3 = z_p.reshape(batch, seq, z_p.shape[1])
    big = lambda g: pl.BlockSpec((1, seq, HEAD_W), lambda b, h, g=g: (b, 0, g * heads + h))
    meta = lambda g: pl.BlockSpec((N_META, HEAD_W), lambda b, h, g=g: (0, g * heads + h))
    return pl.pallas_call(
        _hgrn_prompt_kernel,
        grid=(batch, heads),
        in_specs=[big(G_HQ), big(G_HF), big(G_HI), meta(G_HQ), meta(G_HF), meta(G_HI),
                  pl.BlockSpec((hgrn_lb.shape[0], HEAD_W), lambda b, h: (0, h))],
        out_specs=[pl.BlockSpec((1, seq, HEAD_W), lambda b, h: (b, 0, h)),
                   pl.BlockSpec((1, 1, HEAD_W, HEAD_W), lambda b, h: (b, h, 0, 0))],
        out_shape=[jax.ShapeDtypeStruct((batch, seq, heads * HEAD_W), F32),
                   jax.ShapeDtypeStruct((batch, heads, HEAD_W, HEAD_W), F32)],
        scratch_shapes=[pltpu.VMEM((HEAD_W, HEAD_W), F32)],
        compiler_params=pltpu.CompilerParams(dimension_semantics=("arbitrary", "arbitrary"),
                                             vmem_limit_bytes=VMEM_LIMIT),
        name="hgrn_prompt",
    )(zp3, zp3, zp3, z_small, z_small, z_small, hgrn_lb)


def _diff_lambda(lp):
    s01 = jnp.sum(lp[0:1] * lp[1:2], axis=-1, keepdims=True)
    s23 = jnp.sum(lp[2:3] * lp[3:4], axis=-1, keepdims=True)
    return jnp.exp(s01) - jnp.exp(s23) + LAMBDA_INIT


def _bias_by_distance(dist, bias_of_bucket):
    out = jnp.where(dist >= BUCKET_THR[1], bias_of_bucket(1), bias_of_bucket(0))
    for bkt in range(2, N_BUCKETS):
        out = jnp.where(dist >= BUCKET_THR[bkt], bias_of_bucket(bkt), out)
    return out


def _softmax_step(s, v_bf, m_ref, l_ref, acc_ref):
    m_old = m_ref[...]
    m_new = jnp.maximum(m_old, jnp.max(s, axis=-1, keepdims=True))
    alpha = jnp.exp(m_old - m_new)
    p = jnp.exp(s - m_new)
    l_ref[...] = alpha * l_ref[...] + jnp.sum(p, axis=-1, keepdims=True)
    acc_ref[...] = alpha * acc_ref[...] + jnp.dot(p.astype(BF16), v_bf, preferred_element_type=F32)
    m_ref[...] = m_new


ATT_TQ = 256


def _attn_prompt_kernel(rb_ref, lp_ref, q_ref, k_ref, v_ref, km_ref, vm_ref, o_ref,
                        kb_ref, vb_ref, m_ref, l_ref, acc_ref, bdiag_ref, bnear_ref, bmeta_ref):
    h = pl.program_id(0)
    b = pl.program_id(1)
    qi = pl.program_id(2)
    tq = ATT_TQ
    bias_of_bucket = lambda bkt: rb_ref[bkt * rb_ref.shape[0] // N_BUCKETS + h]
    far_bias = bias_of_bucket(N_BUCKETS - 1)

    @pl.when((b == 0) & (qi == 0))
    def _():
        row = lax.broadcasted_iota(jnp.int32, (2 * tq, tq), 0) % tq
        col = lax.broadcasted_iota(jnp.int32, (2 * tq, tq), 1)
        bdiag_ref[...] = _bias_by_distance(row - col, bias_of_bucket)
        bnear_ref[...] = _bias_by_distance(row - col + tq, bias_of_bucket)
        rowm = lax.broadcasted_iota(jnp.int32, (2 * tq, N_META), 0) % tq
        colm = lax.broadcasted_iota(jnp.int32, (2 * tq, N_META), 1)
        bmeta_ref[...] = _bias_by_distance(rowm + N_META - colm, bias_of_bucket)

    @pl.when(qi == 0)
    def _():
        kb_ref[...] = k_ref[0].astype(BF16)
        vb_ref[...] = v_ref[0].astype(BF16)

    q = q_ref[0] * (AT_DQK ** -0.5)
    lane = lax.broadcasted_iota(jnp.int32, q.shape, 1)
    qs = jnp.concatenate([jnp.where(lane < AT_DQK, q, 0.0), jnp.where(lane >= AT_DQK, q, 0.0)],
                         axis=0).astype(BF16)

    m_ref[...] = jnp.full(m_ref.shape, NEG_INF, F32)
    l_ref[...] = jnp.zeros(l_ref.shape, F32)
    acc_ref[...] = jnp.zeros(acc_ref.shape, F32)

    s = _dot_nt(qs, km_ref[...].astype(BF16))
    s = s + jnp.where(qi == 0, bmeta_ref[...], far_bias)
    _softmax_step(s, vm_ref[...].astype(BF16), m_ref, l_ref, acc_ref)

    def far_body(j, carry):
        rows = pl.ds(pl.multiple_of(j * tq, tq), tq)
        s = _dot_nt(qs, kb_ref[rows, :]) + far_bias
        _softmax_step(s, vb_ref[rows, :], m_ref, l_ref, acc_ref)
        return carry

    lax.fori_loop(0, qi - 1, far_body, 0)

    @pl.when(qi >= 1)
    def _():
        rows = pl.ds(pl.multiple_of((qi - 1) * tq, tq), tq)
        s = _dot_nt(qs, kb_ref[rows, :]) + bnear_ref[...]
        _softmax_step(s, vb_ref[rows, :], m_ref, l_ref, acc_ref)

    rows = pl.ds(pl.multiple_of(qi * tq, tq), tq)
    row = lax.broadcasted_iota(jnp.int32, (2 * tq, tq), 0) % tq
    col = lax.broadcasted_iota(jnp.int32, (2 * tq, tq), 1)
    s = jnp.where(row >= col, _dot_nt(qs, kb_ref[rows, :]) + bdiag_ref[...], NEG_INF)
    _softmax_step(s, vb_ref[rows, :], m_ref, l_ref, acc_ref)

    o = acc_ref[...] / l_ref[...]
    o_ref[0] = o[:tq] - _diff_lambda(lp_ref[...]) * o[tq:]


def _attn_prompt(z_p, z_small, rel_bias, diff_lambda, batch, seq, heads):
    assert CONST_BIAS_DIST <= ATT_TQ + 1 and CONST_BIAS_DIST <= N_META + 1 + ATT_TQ - N_META
    zp3 = z_p.reshape(batch, seq, z_p.shape[1])
    tq = ATT_TQ
    full = lambda g: pl.BlockSpec((1, seq, HEAD_W), lambda h, b, i, g=g: (b, 0, g * heads + h))
    meta = lambda g: pl.BlockSpec((N_META, HEAD_W), lambda h, b, i, g=g: (0, g * heads + h))
    return pl.pallas_call(
        _attn_prompt_kernel,
        grid=(heads, batch, seq // tq),
        in_specs=[pl.BlockSpec(memory_space=pltpu.SMEM),
                  pl.BlockSpec(diff_lambda.shape, lambda h, b, i: (0, 0)),
                  pl.BlockSpec((1, tq, HEAD_W), lambda h, b, i: (b, i, G_AQ * heads + h)),
                  full(G_AK), full(G_AV), meta(G_AK), meta(G_AV)],
        out_specs=pl.BlockSpec((1, tq, HEAD_W), lambda h, b, i: (b, i, h)),
        out_shape=jax.ShapeDtypeStruct((batch, seq, heads * HEAD_W), F32),
        scratch_shapes=[pltpu.VMEM((seq, HEAD_W), BF16), pltpu.VMEM((seq, HEAD_W), BF16),
                        pltpu.VMEM((2 * tq, 1), F32), pltpu.VMEM((2 * tq, 1), F32),
                        pltpu.VMEM((2 * tq, HEAD_W), F32),
                        pltpu.VMEM((2 * tq, tq), F32), pltpu.VMEM((2 * tq, tq), F32),
                        pltpu.VMEM((2 * tq, N_META), F32)],
        compiler_params=pltpu.CompilerParams(dimension_semantics=("arbitrary", "arbitrary", "arbitrary"),
                                             vmem_limit_bytes=VMEM_LIMIT),
        name="attn_prompt",
    )(rel_bias.reshape(-1), diff_lambda, zp3, zp3, zp3, z_small, z_small)


def _head_rmsnorm(o, g):
    parts = []
    for hh in range(o.shape[1] // HEAD_W):
        oh = o[:, hh * HEAD_W:(hh + 1) * HEAD_W]
        ms = jnp.mean(oh * oh, axis=-1, keepdims=True)
        parts.append(oh * lax.rsqrt(ms + RMS_EPS) * g)
    return jnp.concatenate(parts, axis=1)


def _merge_kernel(ohg_ref, hg_ref, oat_ref, ag_ref, x_ref, gh_ref, ga_ref, w_ref, gf_ref, y_ref):
    mix_h = _head_rmsnorm(ohg_ref[...], gh_ref[...]) * _silu(hg_ref[...])
    mix_a = _head_rmsnorm(oat_ref[...], ga_ref[...]) * (1.0 - LAMBDA_INIT) * _silu(ag_ref[...])
    mixed = jnp.concatenate([mix_h, mix_a], axis=1).astype(BF16)
    hres = x_ref[...] + jnp.dot(mixed, w_ref[...], preferred_element_type=F32)
    ms = jnp.mean(hres * hres, axis=-1, keepdims=True)
    y_ref[...] = hres * lax.rsqrt(ms + RMS_EPS) * gf_ref[...]


def _merge(o_hg, o_at, z, x, g_hg, g_at, w_out_bf, g_final, tm, heads):
    rows, d = x.shape
    gw = heads * HEAD_W
    row_blk = lambda w, c: pl.BlockSpec((tm, w), lambda i, c=c: (i, c))
    const = lambda shape: pl.BlockSpec(shape, lambda i: (0, 0))
    return pl.pallas_call(
        _merge_kernel,
        grid=(rows // tm,),
        in_specs=[row_blk(gw, 0), row_blk(gw, G_HG), row_blk(gw, 0), row_blk(gw, G_AG), row_blk(d, 0),
                  const((1, HEAD_W)), const((1, HEAD_W)), const(w_out_bf.shape), const((1, d))],
        out_specs=row_blk(d, 0),
        out_shape=jax.ShapeDtypeStruct((rows, d), F32),
        compiler_params=pltpu.CompilerParams(dimension_semantics=("arbitrary",),
                                             vmem_limit_bytes=VMEM_LIMIT),
        name="merge",
    )(o_hg, z, o_at, z, x, g_hg, g_at, w_out_bf, g_final)


def _transpose_rows_to_cols(x):
    pad = jnp.zeros((HEAD_W - x.shape[0], HEAD_W), F32)
    return jnp.concatenate([x, pad], axis=0).T


def _hgrn_step_kernel(q_ref, f_ref, v_ref, lbp_ref, s_ref, o_ref, sn_ref):
    lb = _forget_lower_bound(lbp_ref[...])
    f = lb + (1.0 - lb) * _sigmoid(f_ref[...])
    k = 1.0 - f
    b = jnp.log(f)
    qe = q_ref[...] * jnp.exp(b)
    ke = k * jnp.exp(-b)
    a = jnp.sum(qe * ke, axis=-1, keepdims=True)
    v = v_ref[...]
    decay_t = _transpose_rows_to_cols(jnp.exp(b))
    kd_t = _transpose_rows_to_cols(k * jnp.exp(b - b))
    qe_t = _transpose_rows_to_cols(qe)
    for i in range(q_ref.shape[0]):
        s = s_ref[0, i, 0]
        vi = v[i:i + 1]
        o_ref[i:i + 1, :] = jnp.sum(qe_t[:, i:i + 1] * s, axis=0, keepdims=True) + a[i:i + 1] * vi
        sn_ref[0, i, 0] = decay_t[:, i:i + 1] * s + kd_t[:, i:i + 1] * vi


def _hgrn_step(z_samp, hgrn_lb, state, heads):
    db = z_samp.shape[0]
    col = lambda g: pl.BlockSpec((db, HEAD_W), lambda h, g=g: (0, g * heads + h))
    st_spec = pl.BlockSpec((1, db, 1, HEAD_W, HEAD_W), lambda h: (0, 0, h, 0, 0))
    return pl.pallas_call(
        _hgrn_step_kernel,
        grid=(heads,),
        in_specs=[col(G_HQ), col(G_HF), col(G_HI),
                  pl.BlockSpec((hgrn_lb.shape[0], HEAD_W), lambda h: (0, h)), st_spec],
        out_specs=[pl.BlockSpec((db, HEAD_W), lambda h: (0, h)), st_spec],
        out_shape=[jax.ShapeDtypeStruct((db, heads * HEAD_W), F32),
                   jax.ShapeDtypeStruct(state.shape, F32)],
        compiler_params=pltpu.CompilerParams(dimension_semantics=("arbitrary",),
                                             vmem_limit_bytes=VMEM_LIMIT),
        name="hgrn_step",
    )(z_samp, z_samp, z_samp, hgrn_lb, state)


DEC_PAGES_PER_STEP = 8


def _attn_decode_kernel(pt_ref, rbt_ref, lp_ref, q_ref, kn_ref, vn_ref, *rest, heads, page):
    npp = DEC_PAGES_PER_STEP
    k_refs = rest[:npp]
    v_refs = rest[npp:2 * npp]
    o_ref, qs_ref, m_ref, l_ref, acc_ref, blast_ref = rest[2 * npp:]
    j = pl.program_id(1)
    n_steps = pl.num_programs(1)
    rows = 2 * heads
    lanes = page * heads
    rbt = jnp.concatenate([rbt_ref[...], rbt_ref[...]], axis=0)
    far_bias = rbt[:, N_BUCKETS - 1:N_BUCKETS]
    r_id = lax.broadcasted_iota(jnp.int32, (rows, lanes), 0)
    l_id = lax.broadcasted_iota(jnp.int32, (rows, lanes), 1)
    own_head = l_id % heads == r_id % heads

    def stacked(x):
        lane = lax.broadcasted_iota(jnp.int32, x.shape, 1)
        return jnp.concatenate([jnp.where(lane < AT_DQK, x, 0.0), jnp.where(lane >= AT_DQK, x, 0.0)], axis=0)

    @pl.when(j == 0)
    def _():
        qs_ref[...] = stacked(q_ref[0] * (AT_DQK ** -0.5)).astype(BF16)
        m_ref[...] = jnp.full(m_ref.shape, NEG_INF, F32)
        l_ref[...] = jnp.zeros(l_ref.shape, F32)
        acc_ref[...] = jnp.zeros(acc_ref.shape, F32)
        blast_ref[...] = _bias_by_distance(page - l_id // heads, lambda bkt: rbt[:, bkt:bkt + 1])

    qs = qs_ref[...]
    s_parts = []
    for i in range(npp):
        s_i = _dot_nt(qs, k_refs[i][...].astype(BF16))
        if i == npp - 1:
            s_i = s_i + jnp.where(j == n_steps - 1, blast_ref[...], far_bias)
        else:
            s_i = s_i + far_bias
        s_parts.append(jnp.where(own_head, s_i, NEG_INF))

    m_old = m_ref[...]
    m_blk = s_parts[0]
    for s_i in s_parts[1:]:
        m_blk = jnp.maximum(m_blk, s_i)
    m_new = jnp.maximum(m_old, jnp.max(m_blk, axis=-1, keepdims=True))
    alpha = jnp.exp(m_old - m_new)
    l_blk = jnp.zeros((rows, lanes), F32)
    pv = jnp.zeros(acc_ref.shape, F32)
    for i in range(npp):
        p_i = jnp.exp(s_parts[i] - m_new)
        l_blk = l_blk + p_i
        pv = pv + jnp.dot(p_i.astype(BF16), v_refs[i][...].astype(BF16), preferred_element_type=F32)
    l_ref[...] = alpha * l_ref[...] + jnp.sum(l_blk, axis=-1, keepdims=True)
    acc_ref[...] = alpha * acc_ref[...] + pv
    m_ref[...] = m_new

    @pl.when(j == n_steps - 1)
    def _():
        kn = jnp.concatenate([kn_ref[0], kn_ref[0]], axis=0)
        vn = jnp.concatenate([vn_ref[0], vn_ref[0]], axis=0)
        s_new = jnp.sum(qs.astype(F32) * kn, axis=-1, keepdims=True) + rbt[:, 0:1]
        m_old = m_ref[...]
        m_fin = jnp.maximum(m_old, s_new)
        alpha = jnp.exp(m_old - m_fin)
        p_new = jnp.exp(s_new - m_fin)
        l_fin = alpha * l_ref[...] + p_new
        o = (alpha * acc_ref[...] + p_new * vn) / l_fin
        o_ref[0] = o[:heads] - _diff_lambda(lp_ref[...]) * o[heads:]


def _attn_decode(q_s, k_s, v_s, cache_k, cache_v, page_table, rel_bias, diff_lambda):
    db, n_pages = page_table.shape
    n_phys, page, heads, _ = cache_k.shape
    npp = DEC_PAGES_PER_STEP
    assert CONST_BIAS_DIST <= page + 1 and n_pages % npp == 0
    ck = cache_k.reshape(n_phys, page * heads, HEAD_W)
    cv = cache_v.reshape(n_phys, page * heads, HEAD_W)
    vec = pl.BlockSpec((1, heads, HEAD_W), lambda b, j, pt: (b, 0, 0))
    page_spec = lambda i: pl.BlockSpec((None, page * heads, HEAD_W),
                                       lambda b, j, pt, i=i: (pt[b, j * npp + i], 0, 0))
    rows = 2 * heads
    kernel = functools.partial(_attn_decode_kernel, heads=heads, page=page)
    return pl.pallas_call(
        kernel,
        grid_spec=pltpu.PrefetchScalarGridSpec(
            num_scalar_prefetch=1,
            grid=(db, n_pages // npp),
            in_specs=[pl.BlockSpec((heads, N_BUCKETS), lambda b, j, pt: (0, 0)),
                      pl.BlockSpec(diff_lambda.shape, lambda b, j, pt: (0, 0)),
                      vec, vec, vec]
                     + [page_spec(i) for i in range(npp)] + [page_spec(i) for i in range(npp)],
            out_specs=pl.BlockSpec((1, heads, HEAD_W), lambda b, j, pt: (b, 0, 0)),
            scratch_shapes=[pltpu.VMEM((rows, HEAD_W), BF16), pltpu.VMEM((rows, 1), F32),
                            pltpu.VMEM((rows, 1), F32), pltpu.VMEM((rows, HEAD_W), F32),
                            pltpu.VMEM((rows, page * heads), F32)]),
        out_shape=jax.ShapeDtypeStruct((db, heads, HEAD_W), F32),
        compiler_params=pltpu.CompilerParams(dimension_semantics=("arbitrary", "arbitrary"),
                                             vmem_limit_bytes=VMEM_LIMIT),
        name="attn_decode",
    )(page_table, rel_bias.T, diff_lambda, q_s, k_s, v_s, *([ck] * npp), *([cv] * npp))


def kernel(x_prompt, x_sample, cache_k, cache_v, state_hgrn, page_table, meta_tokens, rel_bias, hgrn_lb,
           norm_g, w_in, hgrn_norm_g, diff_norm_g, diff_lambda, w_out, final_norm_g):
    batch, seq, d = x_prompt.shape
    db = x_sample.shape[0]
    heads = cache_k.shape[3]
    gw = heads * HEAD_W
    assert norm_g.shape[0] == 1 and x_sample.shape[1] == 1, "one layer, one sample token per sequence"

    w_in_bf = w_in[0].astype(BF16)
    w_out_bf = w_out[0].astype(BF16)
    g_in = norm_g[0].reshape(1, d)
    g_final = final_norm_g.reshape(1, d)
    g_hg = hgrn_norm_g[0].reshape(1, HEAD_W)
    g_at = diff_norm_g[0].reshape(1, HEAD_W)
    lam_p = diff_lambda[0]

    xp = x_prompt.reshape(batch * seq, d)
    xs = x_sample.reshape(db, d)
    small = jnp.concatenate([meta_tokens.astype(F32), xs], axis=0)

    z_p = _proj(xp, g_in, w_in_bf, tm=512, tn=1024)
    z_small = _proj(small, g_in, w_in_bf, tm=small.shape[0], tn=1024)
    z_samp = z_small[N_META:]

    o_hg_p, s_p = _hgrn_prompt(z_p, z_small, hgrn_lb, batch, seq, heads)
    o_at_p = _attn_prompt(z_p, z_small, rel_bias, lam_p, batch, seq, heads)
    y_p = _merge(o_hg_p.reshape(batch * seq, gw), o_at_p.reshape(batch * seq, gw), z_p, xp,
                 g_hg, g_at, w_out_bf, g_final, tm=256, heads=heads)

    o_hg_s, s_s = _hgrn_step(z_samp, hgrn_lb, state_hgrn, heads)
    samp_heads = lambda g: z_samp[:, g * gw:(g + 1) * gw].reshape(db, heads, HEAD_W)
    q_s, k_s, v_s = samp_heads(G_AQ), samp_heads(G_AK), samp_heads(G_AV)
    o_at_s = _attn_decode(q_s, k_s, v_s, cache_k[0], cache_v[0], page_table, rel_bias, lam_p)
    y_s = _merge(o_hg_s, o_at_s.reshape(db, gw), z_samp, xs, g_hg, g_at, w_out_bf, g_final, tm=db, heads=heads)

    def with_meta(g):
        meta_rows = jnp.broadcast_to(z_small[None, :N_META, g * gw:(g + 1) * gw], (batch, N_META, gw))
        real_rows = z_p[:, g * gw:(g + 1) * gw].reshape(batch, seq, gw)
        return jnp.concatenate([meta_rows, real_rows], axis=1).reshape(1, batch, N_META + seq, heads, HEAD_W)

    return (y_p.reshape(batch, seq, d), y_s.reshape(db, 1, d), with_meta(G_AK), with_meta(G_AV),
            s_p[None], k_s.reshape(1, db, 1, heads, HEAD_W), v_s.reshape(1, db, 1, heads, HEAD_W), s_s)
```

```python
import functools
import math

import numpy as np
import jax
import jax.numpy as jnp
from jax import lax
from jax.experimental import pallas as pl
from jax.experimental.pallas import tpu as pltpu

F32 = jnp.float32
BF16 = jnp.bfloat16

N_META = 16
HEAD_W = 128
AT_DQK = 64
N_BUCKETS = 32
MAX_DISTANCE = 128
MAX_EXACT = N_BUCKETS // 2
HGRN_CHUNK = 64
RMS_EPS = 1e-6
NEG_INF = -1e30
LAMBDA_INIT = 0.8 - 0.6 * math.exp(-0.3 * 0)
G_HQ, G_HF, G_HI, G_HG, G_AQ, G_AK, G_AV, G_AG = range(8)

VMEM_LIMIT = 56 * 1024 * 1024


def _bucket_thresholds():
    n = np.arange(0, 2 * MAX_DISTANCE)
    nl = np.maximum(n, MAX_EXACT).astype(np.float32)
    large = MAX_EXACT + (np.log(nl / np.float32(MAX_EXACT)) / np.float32(math.log(MAX_DISTANCE / MAX_EXACT))
                         * np.float32(N_BUCKETS - MAX_EXACT)).astype(np.int32)
    bucket = np.where(n < MAX_EXACT, n, np.minimum(large, N_BUCKETS - 1))
    return [int(np.argmax(bucket >= b)) for b in range(N_BUCKETS)]


BUCKET_THR = _bucket_thresholds()
CONST_BIAS_DIST = BUCKET_THR[N_BUCKETS - 1]


def _sigmoid(x):
    return 1.0 / (1.0 + jnp.exp(-x))


def _silu(x):
    return x * _sigmoid(x)


def _dot_nt(a, b):
    return lax.dot_general(a, b, (((1,), (1,)), ((), ())), preferred_element_type=F32)


def _dot_tn(a, b):
    return lax.dot_general(a, b, (((0,), (0,)), ((), ())), preferred_element_type=F32)


def _proj_kernel(x_ref, g_ref, w_ref, z_ref, xn_ref):
    @pl.when(pl.program_id(1) == 0)
    def _():
        x = x_ref[...]
        ms = jnp.mean(x * x, axis=-1, keepdims=True)
        xn_ref[...] = (x * lax.rsqrt(ms + RMS_EPS) * g_ref[...]).astype(BF16)

    z_ref[...] = jnp.dot(xn_ref[...], w_ref[...], preferred_element_type=F32)


def _proj(x, g, w_bf, tm, tn):
    rows, d = x.shape
    n = w_bf.shape[1]
    return pl.pallas_call(
        _proj_kernel,
        grid=(rows // tm, n // tn),
        in_specs=[pl.BlockSpec((tm, d), lambda i, j: (i, 0)),
                  pl.BlockSpec((1, d), lambda i, j: (0, 0)),
                  pl.BlockSpec((d, tn), lambda i, j: (0, j))],
        out_specs=pl.BlockSpec((tm, tn), lambda i, j: (i, j)),
        out_shape=jax.ShapeDtypeStruct((rows, n), F32),
        scratch_shapes=[pltpu.VMEM((tm, d), BF16)],
        compiler_params=pltpu.CompilerParams(dimension_semantics=("arbitrary", "arbitrary"),
                                             vmem_limit_bytes=VMEM_LIMIT),
        name="proj",
    )(x, g, w_bf)


def _forget_lower_bound(lbp):
    e = jnp.exp(lbp - jnp.max(lbp, axis=0, keepdims=True))
    return e[0:1] / jnp.sum(e, axis=0, keepdims=True)


def _cumsum_rows(g, tri_bf):
    g1 = g.astype(BF16)
    r1 = g - g1.astype(F32)
    g2 = r1.astype(BF16)
    g3 = (r1 - g2.astype(F32)).astype(BF16)
    dot = functools.partial(jnp.dot, preferred_element_type=F32)
    return dot(tri_bf, g1) + dot(tri_bf, g2) + dot(tri_bf, g3)


def _hgrn_chunk_local(q, hf, v, lb):
    c = q.shape[0]
    row = lax.broadcasted_iota(jnp.int32, (c, c), 0)
    col = lax.broadcasted_iota(jnp.int32, (c, c), 1)
    causal = row >= col
    f = lb + (1.0 - lb) * _sigmoid(hf)
    k = 1.0 - f
    b = _cumsum_rows(jnp.log(f), causal.astype(BF16))
    qe = (q * jnp.exp(b)).astype(BF16)
    ke = (k * jnp.exp(-b)).astype(BF16)
    a = jnp.where(causal, _dot_nt(qe, ke), 0.0)
    v_bf = v.astype(BF16)
    intra = jnp.dot(a.astype(BF16), v_bf, preferred_element_type=F32)
    b_last = b[c - 1:c]
    kd = (k * jnp.exp(b_last - b)).astype(BF16)
    return qe, intra, _dot_tn(v_bf, kd), jnp.exp(b_last)


HGRN_SLAB = 256
HGRN_PAIR = 128


def _hgrn_prompt_kernel(q_ref, f_ref, v_ref, qm_ref, fm_ref, vm_ref, lbp_ref, o_ref, s_ref,
                        b_ref, qe_ref, ke_ref, kd_ref, vb_ref, vt_ref, a_ref, ut_ref, dec_ref, st_ref):
    ck, slab, pair = HGRN_CHUNK, HGRN_SLAB, HGRN_PAIR
    seq = q_ref.shape[1]
    n_chunks = seq // ck
    dot = functools.partial(jnp.dot, preferred_element_type=F32)
    lb = _forget_lower_bound(lbp_ref[...])

    _, _, st, _ = _hgrn_chunk_local(qm_ref[...], fm_ref[...], vm_ref[...], lb)

    f = lb + (1.0 - lb) * _sigmoid(f_ref[0])
    g = jnp.log(f)
    g1 = g.astype(BF16)
    r1 = g - g1.astype(F32)
    g2 = r1.astype(BF16)
    g3 = (r1 - g2.astype(F32)).astype(BF16)
    gcat = jnp.concatenate([g1, g2, g3], axis=1)
    r = lax.broadcasted_iota(jnp.int32, (slab, slab), 0)
    c = lax.broadcasted_iota(jnp.int32, (slab, slab), 1)
    tri_bd = ((r // ck == c // ck) & (r >= c)).astype(BF16)
    for si in range(seq // slab):
        rows = slice(si * slab, (si + 1) * slab)
        t = dot(tri_bd, gcat[rows])
        b_ref[rows, :] = t[:, :HEAD_W] + t[:, HEAD_W:2 * HEAD_W] + t[:, 2 * HEAD_W:]

    k = 1.0 - f
    b = b_ref[...]
    qe_ref[...] = (q_ref[0] * jnp.exp(b)).astype(BF16)
    ke_ref[...] = (k * jnp.exp(-b)).astype(BF16)
    v = v_ref[0]
    vb_ref[...] = v.astype(BF16)
    vt_ref[...] = v.T.astype(BF16)
    for ci in range(n_chunks):
        rows = slice(ci * ck, (ci + 1) * ck)
        b_last = b_ref[(ci + 1) * ck - 1:(ci + 1) * ck, :]
        dec_ref[ci] = jnp.exp(b_last)
        kd_ref[rows, :] = (k[rows] * jnp.exp(b_last - b_ref[rows, :])).astype(BF16)

    rp = lax.broadcasted_iota(jnp.int32, (pair, pair), 0)
    cp = lax.broadcasted_iota(jnp.int32, (pair, pair), 1)
    keep = (rp // ck == cp // ck) & (rp >= cp)
    for pi in range(seq // pair):
        rows = slice(pi * pair, (pi + 1) * pair)
        a_ref[rows, :] = jnp.where(keep, _dot_nt(qe_ref[rows, :], ke_ref[rows, :]), 0.0).astype(BF16)

    for pi in range(seq // pair):
        rows = slice(pi * pair, (pi + 1) * pair)
        o_ref[0, rows, :] = dot(a_ref[rows, :], vb_ref[rows, :])
    for ci in range(n_chunks):
        rows = slice(ci * ck, (ci + 1) * ck)
        ut_ref[ci] = dot(vt_ref[:, rows], kd_ref[rows, :])

    for ci in range(n_chunks):
        st_ref[ci] = st.astype(BF16)
        st = st * dec_ref[ci] + ut_ref[ci]
    s_ref[0, 0] = st.T

    for ci in range(n_chunks):
        rows = slice(ci * ck, (ci + 1) * ck)
        o_ref[0, rows, :] += _dot_nt(qe_ref[rows, :], st_ref[ci])


def _hgrn_prompt(z_p, z_small, hgrn_lb, batch, seq, heads):
    zp3 = z_p.reshape(batch, seq, z_p.shape[1])
    big = lambda g: pl.BlockSpec((1, seq, HEAD_W), lambda b, h, g=g: (b, 0, g * heads + h))
    meta = lambda g: pl.BlockSpec((N_META, HEAD_W), lambda b, h, g=g: (0, g * heads + h))
    return pl.pallas_call(
        _hgrn_prompt_kernel,
        grid=(batch, heads),
        in_specs=[big(G_HQ), big(G_HF), big(G_HI), meta(G_HQ), meta(G_HF), meta(G_HI),
                  pl.BlockSpec((hgrn_lb.shape[0], HEAD_W), lambda b, h: (0, h))],
        out_specs=[pl.BlockSpec((1, seq, HEAD_W), lambda b, h: (b, 0, h)),
                   pl.BlockSpec((1, 1, HEAD_W, HEAD_W), lambda b, h: (b, h, 0, 0))],
        out_shape=[jax.ShapeDtypeStruct((batch, seq, heads * HEAD_W), F32),
                   jax.ShapeDtypeStruct((batch, heads, HEAD_W, HEAD_W), F32)],
        scratch_shapes=[pltpu.VMEM((seq, HEAD_W), F32),
                        pltpu.VMEM((seq, HEAD_W), BF16), pltpu.VMEM((seq, HEAD_W), BF16),
                        pltpu.VMEM((seq, HEAD_W), BF16), pltpu.VMEM((seq, HEAD_W), BF16),
                        pltpu.VMEM((HEAD_W, seq), BF16), pltpu.VMEM((seq, HGRN_PAIR), BF16),
                        pltpu.VMEM((seq // HGRN_CHUNK, HEAD_W, HEAD_W), F32),
                        pltpu.VMEM((seq // HGRN_CHUNK, 1, HEAD_W), F32),
                        pltpu.VMEM((seq // HGRN_CHUNK, HEAD_W, HEAD_W), BF16)],
        compiler_params=pltpu.CompilerParams(dimension_semantics=("arbitrary", "arbitrary"),
                                             vmem_limit_bytes=VMEM_LIMIT),
        name="hgrn_prompt",
    )(zp3, zp3, zp3, z_small, z_small, z_small, hgrn_lb)


def _diff_lambda(lp):
    s01 = jnp.sum(lp[0:1] * lp[1:2], axis=-1, keepdims=True)
    s23 = jnp.sum(lp[2:3] * lp[3:4], axis=-1, keepdims=True)
    return jnp.exp(s01) - jnp.exp(s23) + LAMBDA_INIT


def _bias_by_distance(dist, bias_of_bucket):
    out = jnp.where(dist >= BUCKET_THR[1], bias_of_bucket(1), bias_of_bucket(0))
    for bkt in range(2, N_BUCKETS):
        out = jnp.where(dist >= BUCKET_THR[bkt], bias_of_bucket(bkt), out)
    return out


ATT_TQ = 256


def _colmax(s):
    return jnp.max(s.reshape(s.shape[0] // 8, 8, s.shape[1]), axis=0)


def _colsum(s):
    return jnp.sum(s.reshape(s.shape[0] // 8, 8, s.shape[1]), axis=0)


def _attn_prompt_kernel(rb_ref, lp_ref, q_ref, k_ref, v_ref, km_ref, vm_ref, o_ref,
                        kb_ref, vt_ref, vmt_ref, s_ref, bdiag_ref, bnear_ref, bmeta_ref):
    h = pl.program_id(0)
    b = pl.program_id(1)
    tq = ATT_TQ
    n_q = q_ref.shape[1] // tq
    log2e = 1.0 / math.log(2.0)
    bias_of_bucket = lambda bkt: rb_ref[bkt * rb_ref.shape[0] // N_BUCKETS + h] * log2e
    far_bias = bias_of_bucket(N_BUCKETS - 1)
    key = lax.broadcasted_iota(jnp.int32, (tq, tq), 0)
    qry = lax.broadcasted_iota(jnp.int32, (tq, tq), 1)

    @pl.when(b == 0)
    def _():
        bdiag_ref[...] = _bias_by_distance(qry - key, bias_of_bucket)
        bnear_ref[...] = _bias_by_distance(qry - key + tq, bias_of_bucket)
        keym = lax.broadcasted_iota(jnp.int32, (N_META, tq), 0)
        qrym = lax.broadcasted_iota(jnp.int32, (N_META, tq), 1)
        bmeta_ref[...] = _bias_by_distance(qrym + N_META - keym, bias_of_bucket)

    kb_ref[...] = k_ref[0].astype(BF16)
    vt_ref[...] = v_ref[0].T.astype(BF16)
    pad = jnp.zeros((HEAD_W - N_META, HEAD_W), F32)
    vmt_ref[...] = jnp.concatenate([vm_ref[...], pad], axis=0).T.astype(BF16)
    km_bf = km_ref[...].astype(BF16)
    lam = _diff_lambda(lp_ref[...])
    causal = qry >= key
    dot = functools.partial(jnp.dot, preferred_element_type=F32)

    for qi in range(n_q):
        par = qi % 2
        q = q_ref[0, qi * tq:(qi + 1) * tq, :] * (AT_DQK ** -0.5 * log2e)
        lane = lax.broadcasted_iota(jnp.int32, q.shape, 1)
        qc = (jnp.where(lane < AT_DQK, q, 0.0).astype(BF16), jnp.where(lane >= AT_DQK, q, 0.0).astype(BF16))
        o_c = []
        for c in range(2):
            sm = _dot_nt(km_bf, qc[c]) + (bmeta_ref[...] if qi == 0 else far_bias)
            m8 = _colmax(sm)
            for j in range(qi + 1):
                s = _dot_nt(kb_ref[j * tq:(j + 1) * tq, :], qc[c])
                if j == qi:
                    s = jnp.where(causal, s + bdiag_ref[...], NEG_INF)
                elif j == qi - 1:
                    s = s + bnear_ref[...]
                s_ref[par, c, j * tq:(j + 1) * tq, :] = s
                m8 = jnp.maximum(m8, _colmax(s) + far_bias if j < qi - 1 else _colmax(s))
            m = jnp.max(m8, axis=0, keepdims=True)
            m_far = m - far_bias
            pm = jnp.exp2(sm - m)
            l8 = _colsum(pm)
            acc = dot(vmt_ref[:, :N_META], pm.astype(BF16))
            for j in range(qi + 1):
                p = jnp.exp2(s_ref[par, c, j * tq:(j + 1) * tq, :] - (m_far if j < qi - 1 else m))
                l8 = l8 + _colsum(p)
                acc = acc + dot(vt_ref[:, j * tq:(j + 1) * tq], p.astype(BF16))
            o_c.append(acc / jnp.sum(l8, axis=0, keepdims=True))
        o_ref[0, qi * tq:(qi + 1) * tq, :] = (o_c[0] - lam * o_c[1]).T


def _attn_prompt(z_p, z_small, rel_bias, diff_lambda, batch, seq, heads):
    assert CONST_BIAS_DIST <= ATT_TQ + 1 and CONST_BIAS_DIST <= N_META + 1 + ATT_TQ - N_META
    zp3 = z_p.reshape(batch, seq, z_p.shape[1])
    tq = ATT_TQ
    full = lambda g: pl.BlockSpec((1, seq, HEAD_W), lambda h, b, g=g: (b, 0, g * heads + h))
    meta = lambda g: pl.BlockSpec((N_META, HEAD_W), lambda h, b, g=g: (0, g * heads + h))
    return pl.pallas_call(
        _attn_prompt_kernel,
        grid=(heads, batch),
        in_specs=[pl.BlockSpec(memory_space=pltpu.SMEM),
                  pl.BlockSpec(diff_lambda.shape, lambda h, b: (0, 0)),
                  full(G_AQ), full(G_AK), full(G_AV), meta(G_AK), meta(G_AV)],
        out_specs=pl.BlockSpec((1, seq, HEAD_W), lambda h, b: (b, 0, h)),
        out_shape=jax.ShapeDtypeStruct((batch, seq, heads * HEAD_W), F32),
        scratch_shapes=[pltpu.VMEM((seq, HEAD_W), BF16), pltpu.VMEM((HEAD_W, seq), BF16),
                        pltpu.VMEM((HEAD_W, HEAD_W), BF16),
                        pltpu.VMEM((2, 2, seq, tq), F32),
                        pltpu.VMEM((tq, tq), F32), pltpu.VMEM((tq, tq), F32),
                        pltpu.VMEM((N_META, tq), F32)],
        compiler_params=pltpu.CompilerParams(dimension_semantics=("arbitrary", "arbitrary"),
                                             vmem_limit_bytes=VMEM_LIMIT),
        name="attn_prompt",
    )(rel_bias.reshape(-1), diff_lambda, zp3, zp3, zp3, z_small, z_small)


def _head_rmsnorm(o, g):
    parts = []
    for hh in range(o.shape[1] // HEAD_W):
        oh = o[:, hh * HEAD_W:(hh + 1) * HEAD_W]
        ms = jnp.mean(oh * oh, axis=-1, keepdims=True)
        parts.append(oh * lax.rsqrt(ms + RMS_EPS) * g)
    return jnp.concatenate(parts, axis=1)


def _merge_kernel(ohg_ref, hg_ref, oat_ref, ag_ref, x_ref, gh_ref, ga_ref, w_ref, gf_ref, y_ref):
    mix_h = _head_rmsnorm(ohg_ref[...], gh_ref[...]) * _silu(hg_ref[...])
    mix_a = _head_rmsnorm(oat_ref[...], ga_ref[...]) * (1.0 - LAMBDA_INIT) * _silu(ag_ref[...])
    mixed = jnp.concatenate([mix_h, mix_a], axis=1).astype(BF16)
    hres = x_ref[...] + jnp.dot(mixed, w_ref[...], preferred_element_type=F32)
    ms = jnp.mean(hres * hres, axis=-1, keepdims=True)
    y_ref[...] = hres * lax.rsqrt(ms + RMS_EPS) * gf_ref[...]


def _merge(o_hg, o_at, z, x, g_hg, g_at, w_out_bf, g_final, tm, heads):
    rows, d = x.shape
    gw = heads * HEAD_W
    row_blk = lambda w, c: pl.BlockSpec((tm, w), lambda i, c=c: (i, c))
    const = lambda shape: pl.BlockSpec(shape, lambda i: (0, 0))
    return pl.pallas_call(
        _merge_kernel,
        grid=(rows // tm,),
        in_specs=[row_blk(gw, 0), row_blk(gw, G_HG), row_blk(gw, 0), row_blk(gw, G_AG), row_blk(d, 0),
                  const((1, HEAD_W)), const((1, HEAD_W)), const(w_out_bf.shape), const((1, d))],
        out_specs=row_blk(d, 0),
        out_shape=jax.ShapeDtypeStruct((rows, d), F32),
        compiler_params=pltpu.CompilerParams(dimension_semantics=("arbitrary",),
                                             vmem_limit_bytes=VMEM_LIMIT),
        name="merge",
    )(o_hg, z, o_at, z, x, g_hg, g_at, w_out_bf, g_final)


def _transpose_rows_to_cols(x):
    pad = jnp.zeros((HEAD_W - x.shape[0], HEAD_W), F32)
    return jnp.concatenate([x, pad], axis=0).T


def _hgrn_step_kernel(q_ref, f_ref, v_ref, lbp_ref, s_ref, o_ref, sn_ref):
    lb = _forget_lower_bound(lbp_ref[...])
    f = lb + (1.0 - lb) * _sigmoid(f_ref[...])
    k = 1.0 - f
    b = jnp.log(f)
    qe = q_ref[...] * jnp.exp(b)
    ke = k * jnp.exp(-b)
    a = jnp.sum(qe * ke, axis=-1, keepdims=True)
    v = v_ref[...]
    decay_t = _transpose_rows_to_cols(jnp.exp(b))
    kd_t = _transpose_rows_to_cols(k * jnp.exp(b - b))
    qe_t = _transpose_rows_to_cols(qe)
    for i in range(q_ref.shape[0]):
        s = s_ref[0, i, 0]
        vi = v[i:i + 1]
        o_ref[i:i + 1, :] = jnp.sum(qe_t[:, i:i + 1] * s, axis=0, keepdims=True) + a[i:i + 1] * vi
        sn_ref[0, i, 0] = decay_t[:, i:i + 1] * s + kd_t[:, i:i + 1] * vi


def _hgrn_step(z_samp, hgrn_lb, state, heads):
    db = z_samp.shape[0]
    col = lambda g: pl.BlockSpec((db, HEAD_W), lambda h, g=g: (0, g * heads + h))
    st_spec = pl.BlockSpec((1, db, 1, HEAD_W, HEAD_W), lambda h: (0, 0, h, 0, 0))
    return pl.pallas_call(
        _hgrn_step_kernel,
        grid=(heads,),
        in_specs=[col(G_HQ), col(G_HF), col(G_HI),
                  pl.BlockSpec((hgrn_lb.shape[0], HEAD_W), lambda h: (0, h)), st_spec],
        out_specs=[pl.BlockSpec((db, HEAD_W), lambda h: (0, h)), st_spec],
        out_shape=[jax.ShapeDtypeStruct((db, heads * HEAD_W), F32),
                   jax.ShapeDtypeStruct(state.shape, F32)],
        compiler_params=pltpu.CompilerParams(dimension_semantics=("arbitrary",),
                                             vmem_limit_bytes=VMEM_LIMIT),
        name="hgrn_step",
    )(z_samp, z_samp, z_samp, hgrn_lb, state)


DEC_PAGES_PER_STEP = 8


def _attn_decode_kernel(pt_ref, rbt_ref, lp_ref, q_ref, kn_ref, vn_ref, *rest, heads, page):
    npp = DEC_PAGES_PER_STEP
    k_refs = rest[:npp]
    v_refs = rest[npp:2 * npp]
    o_ref, qs_ref, m_ref, l_ref, acc_ref, blast_ref = rest[2 * npp:]
    j = pl.program_id(1)
    n_steps = pl.num_programs(1)
    rows = 2 * heads
    lanes = page * heads
    rbt = jnp.concatenate([rbt_ref[...], rbt_ref[...]], axis=0)
    far_bias = rbt[:, N_BUCKETS - 1:N_BUCKETS]
    r_id = lax.broadcasted_iota(jnp.int32, (rows, lanes), 0)
    l_id = lax.broadcasted_iota(jnp.int32, (rows, lanes), 1)
    own_head = l_id % heads == r_id % heads

    def stacked(x):
        lane = lax.broadcasted_iota(jnp.int32, x.shape, 1)
        return jnp.concatenate([jnp.where(lane < AT_DQK, x, 0.0), jnp.where(lane >= AT_DQK, x, 0.0)], axis=0)

    @pl.when(j == 0)
    def _():
        qs_ref[...] = stacked(q_ref[0] * (AT_DQK ** -0.5)).astype(BF16)
        m_ref[...] = jnp.full(m_ref.shape, NEG_INF, F32)
        l_ref[...] = jnp.zeros(l_ref.shape, F32)
        acc_ref[...] = jnp.zeros(acc_ref.shape, F32)
        blast_ref[...] = _bias_by_distance(page - l_id // heads, lambda bkt: rbt[:, bkt:bkt + 1])

    qs = qs_ref[...]
    s_parts = []
    for i in range(npp):
        s_i = _dot_nt(qs, k_refs[i][...].astype(BF16))
        if i == npp - 1:
            s_i = s_i + jnp.where(j == n_steps - 1, blast_ref[...], far_bias)
        else:
            s_i = s_i + far_bias
        s_parts.append(jnp.where(own_head, s_i, NEG_INF))

    m_old = m_ref[...]
    m_blk = s_parts[0]
    for s_i in s_parts[1:]:
        m_blk = jnp.maximum(m_blk, s_i)
    m_new = jnp.maximum(m_old, jnp.max(m_blk, axis=-1, keepdims=True))
    alpha = jnp.exp(m_old - m_new)
    l_blk = jnp.zeros((rows, lanes), F32)
    pv = jnp.zeros(acc_ref.shape, F32)
    for i in range(npp):
        p_i = jnp.exp(s_parts[i] - m_new)
        l_blk = l_blk + p_i
        pv = pv + jnp.dot(p_i.astype(BF16), v_refs[i][...].astype(BF16), preferred_element_type=F32)
    l_ref[...] = alpha * l_ref[...] + jnp.sum(l_blk, axis=-1, keepdims=True)
    acc_ref[...] = alpha * acc_ref[...] + pv
    m_ref[...] = m_new

    @pl.when(j == n_steps - 1)
    def _():
        kn = jnp.concatenate([kn_ref[0], kn_ref[0]], axis=0)
        vn = jnp.concatenate([vn_ref[0], vn_ref[0]], axis=0)
        s_new = jnp.sum(qs.astype(F32) * kn, axis=-1, keepdims=True) + rbt[:, 0:1]
        m_old = m_ref[...]
        m_fin = jnp.maximum(m_old, s_new)
        alpha = jnp.exp(m_old - m_fin)
        p_new = jnp.exp(s_new - m_fin)
        l_fin = alpha * l_ref[...] + p_new
        o = (alpha * acc_ref[...] + p_new * vn) / l_fin
        o_ref[0] = o[:heads] - _diff_lambda(lp_ref[...]) * o[heads:]


def _attn_decode(q_s, k_s, v_s, cache_k, cache_v, page_table, rel_bias, diff_lambda):
    db, n_pages = page_table.shape
    n_phys, page, heads, _ = cache_k.shape
    npp = DEC_PAGES_PER_STEP
    assert CONST_BIAS_DIST <= page + 1 and n_pages % npp == 0
    ck = cache_k.reshape(n_phys, page * heads, HEAD_W)
    cv = cache_v.reshape(n_phys, page * heads, HEAD_W)
    vec = pl.BlockSpec((1, heads, HEAD_W), lambda b, j, pt: (b, 0, 0))
    page_spec = lambda i: pl.BlockSpec((None, page * heads, HEAD_W),
                                       lambda b, j, pt, i=i: (pt[b, j * npp + i], 0, 0))
    rows = 2 * heads
    kernel = functools.partial(_attn_decode_kernel, heads=heads, page=page)
    return pl.pallas_call(
        kernel,
        grid_spec=pltpu.PrefetchScalarGridSpec(
            num_scalar_prefetch=1,
            grid=(db, n_pages // npp),
            in_specs=[pl.BlockSpec((heads, N_BUCKETS), lambda b, j, pt: (0, 0)),
                      pl.BlockSpec(diff_lambda.shape, lambda b, j, pt: (0, 0)),
                      vec, vec, vec]
                     + [page_spec(i) for i in range(npp)] + [page_spec(i) for i in range(npp)],
            out_specs=pl.BlockSpec((1, heads, HEAD_W), lambda b, j, pt: (b, 0, 0)),
            scratch_shapes=[pltpu.VMEM((rows, HEAD_W), BF16), pltpu.VMEM((rows, 1), F32),
                            pltpu.VMEM((rows, 1), F32), pltpu.VMEM((rows, HEAD_W), F32),
                            pltpu.VMEM((rows, page * heads), F32)]),
        out_shape=jax.ShapeDtypeStruct((db, heads, HEAD_W), F32),
        compiler_params=pltpu.CompilerParams(dimension_semantics=("arbitrary", "arbitrary"),
                                             vmem_limit_bytes=VMEM_LIMIT),
        name="attn_decode",
    )(page_table, rel_bias.T, diff_lambda, q_s, k_s, v_s, *([ck] * npp), *([cv] * npp))


def kernel(x_prompt, x_sample, cache_k, cache_v, state_hgrn, page_table, meta_tokens, rel_bias, hgrn_lb,
           norm_g, w_in, hgrn_norm_g, diff_norm_g, diff_lambda, w_out, final_norm_g):
    batch, seq, d = x_prompt.shape
    db = x_sample.shape[0]
    heads = cache_k.shape[3]
    gw = heads * HEAD_W
    assert norm_g.shape[0] == 1 and x_sample.shape[1] == 1, "one layer, one sample token per sequence"

    w_in_bf = w_in[0].astype(BF16)
    w_out_bf = w_out[0].astype(BF16)
    g_in = norm_g[0].reshape(1, d)
    g_final = final_norm_g.reshape(1, d)
    g_hg = hgrn_norm_g[0].reshape(1, HEAD_W)
    g_at = diff_norm_g[0].reshape(1, HEAD_W)
    lam_p = diff_lambda[0]

    xp = x_prompt.reshape(batch * seq, d)
    xs = x_sample.reshape(db, d)
    small = jnp.concatenate([meta_tokens.astype(F32), xs], axis=0)

    z_p = _proj(xp, g_in, w_in_bf, tm=1024, tn=512)
    z_small = _proj(small, g_in, w_in_bf, tm=small.shape[0], tn=1024)
    z_samp = z_small[N_META:]

    o_hg_p, s_p = _hgrn_prompt(z_p, z_small, hgrn_lb, batch, seq, heads)
    o_at_p = _attn_prompt(z_p, z_small, rel_bias, lam_p, batch, seq, heads)
    y_p = _merge(o_hg_p.reshape(batch * seq, gw), o_at_p.reshape(batch * seq, gw), z_p, xp,
                 g_hg, g_at, w_out_bf, g_final, tm=256, heads=heads)

    o_hg_s, s_s = _hgrn_step(z_samp, hgrn_lb, state_hgrn, heads)
    samp_heads = lambda g: z_samp[:, g * gw:(g + 1) * gw].reshape(db, heads, HEAD_W)
    q_s, k_s, v_s = samp_heads(G_AQ), samp_heads(G_AK), samp_heads(G_AV)
    o_at_s = _attn_decode(q_s, k_s, v_s, cache_k[0], cache_v[0], page_table, rel_bias, lam_p)
    y_s = _merge(o_hg_s, o_at_s.reshape(db, gw), z_samp, xs, g_hg, g_at, w_out_bf, g_final, tm=db, heads=heads)

    def with_meta(g):
        meta_rows = jnp.broadcast_to(z_small[None, :N_META, g * gw:(g + 1) * gw], (batch, N_META, gw))
        real_rows = z_p[:, g * gw:(g + 1) * gw].reshape(batch, seq, gw)
        return jnp.concatenate([meta_rows, real_rows], axis=1).reshape(1, batch, N_META + seq, heads, HEAD_W)

    return (y_p.reshape(batch, seq, d), y_s.reshape(db, 1, d), with_meta(G_AK), with_meta(G_AV),
            s_p[None], k_s.reshape(1, db, 1, heads, HEAD_W), v_s.reshape(1, db, 1, heads, HEAD_W), s_s)
```

```python
import functools
import math

import numpy as np
import jax
import jax.numpy as jnp
from jax import lax
from jax.experimental import pallas as pl
from jax.experimental.pallas import tpu as pltpu

F32 = jnp.float32
BF16 = jnp.bfloat16

N_META = 16
HEAD_W = 128
AT_DQK = 64
N_BUCKETS = 32
MAX_DISTANCE = 128
MAX_EXACT = N_BUCKETS // 2
HGRN_CHUNK = 64
RMS_EPS = 1e-6
NEG_INF = -1e30
LAMBDA_INIT = 0.8 - 0.6 * math.exp(-0.3 * 0)
G_HQ, G_HF, G_HI, G_HG, G_AQ, G_AK, G_AV, G_AG = range(8)

VMEM_LIMIT = 56 * 1024 * 1024


def _bucket_thresholds():
    n = np.arange(0, 2 * MAX_DISTANCE)
    nl = np.maximum(n, MAX_EXACT).astype(np.float32)
    large = MAX_EXACT + (np.log(nl / np.float32(MAX_EXACT)) / np.float32(math.log(MAX_DISTANCE / MAX_EXACT))
                         * np.float32(N_BUCKETS - MAX_EXACT)).astype(np.int32)
    bucket = np.where(n < MAX_EXACT, n, np.minimum(large, N_BUCKETS - 1))
    return [int(np.argmax(bucket >= b)) for b in range(N_BUCKETS)]


BUCKET_THR = _bucket_thresholds()
CONST_BIAS_DIST = BUCKET_THR[N_BUCKETS - 1]


def _sigmoid(x):
    return 1.0 / (1.0 + jnp.exp(-x))


def _silu(x):
    return x * _sigmoid(x)


def _dot_nt(a, b):
    return lax.dot_general(a, b, (((1,), (1,)), ((), ())), preferred_element_type=F32)


def _dot_tn(a, b):
    return lax.dot_general(a, b, (((0,), (0,)), ((), ())), preferred_element_type=F32)


def _proj_kernel(x_ref, g_ref, w_ref, z_ref, xn_ref):
    @pl.when(pl.program_id(1) == 0)
    def _():
        x = x_ref[...]
        ms = jnp.mean(x * x, axis=-1, keepdims=True)
        xn_ref[...] = (x * lax.rsqrt(ms + RMS_EPS) * g_ref[...]).astype(BF16)

    z_ref[...] = jnp.dot(xn_ref[...], w_ref[...], preferred_element_type=F32)


def _proj(x, g, w_bf, tm, tn):
    rows, d = x.shape
    n = w_bf.shape[1]
    return pl.pallas_call(
        _proj_kernel,
        grid=(rows // tm, n // tn),
        in_specs=[pl.BlockSpec((tm, d), lambda i, j: (i, 0)),
                  pl.BlockSpec((1, d), lambda i, j: (0, 0)),
                  pl.BlockSpec((d, tn), lambda i, j: (0, j))],
        out_specs=pl.BlockSpec((tm, tn), lambda i, j: (i, j)),
        out_shape=jax.ShapeDtypeStruct((rows, n), F32),
        scratch_shapes=[pltpu.VMEM((tm, d), BF16)],
        compiler_params=pltpu.CompilerParams(dimension_semantics=("arbitrary", "arbitrary"),
                                             vmem_limit_bytes=VMEM_LIMIT),
        name="proj",
    )(x, g, w_bf)


def _forget_lower_bound(lbp):
    e = jnp.exp(lbp - jnp.max(lbp, axis=0, keepdims=True))
    return e[0:1] / jnp.sum(e, axis=0, keepdims=True)


def _cumsum_rows(g, tri_bf):
    g1 = g.astype(BF16)
    r1 = g - g1.astype(F32)
    g2 = r1.astype(BF16)
    g3 = (r1 - g2.astype(F32)).astype(BF16)
    dot = functools.partial(jnp.dot, preferred_element_type=F32)
    return dot(tri_bf, g1) + dot(tri_bf, g2) + dot(tri_bf, g3)


def _hgrn_chunk_local(q, hf, v, lb):
    c = q.shape[0]
    row = lax.broadcasted_iota(jnp.int32, (c, c), 0)
    col = lax.broadcasted_iota(jnp.int32, (c, c), 1)
    causal = row >= col
    f = lb + (1.0 - lb) * _sigmoid(hf)
    k = 1.0 - f
    b = _cumsum_rows(jnp.log(f), causal.astype(BF16))
    qe = (q * jnp.exp(b)).astype(BF16)
    ke = (k * jnp.exp(-b)).astype(BF16)
    a = jnp.where(causal, _dot_nt(qe, ke), 0.0)
    v_bf = v.astype(BF16)
    intra = jnp.dot(a.astype(BF16), v_bf, preferred_element_type=F32)
    b_last = b[c - 1:c]
    kd = (k * jnp.exp(b_last - b)).astype(BF16)
    return qe, intra, _dot_tn(v_bf, kd), jnp.exp(b_last)


HGRN_SLAB = 256
HGRN_PAIR = 128


def _hgrn_prompt_kernel(q_ref, f_ref, v_ref, qm_ref, fm_ref, vm_ref, lbp_ref, o_ref, s_ref,
                        b_ref, qe_ref, ke_ref, kd_ref, vb_ref, vt_ref, a_ref, ut_ref, dec_ref, st_ref):
    ck, slab, pair = HGRN_CHUNK, HGRN_SLAB, HGRN_PAIR
    seq = q_ref.shape[1]
    n_chunks = seq // ck
    dot = functools.partial(jnp.dot, preferred_element_type=F32)
    lb = _forget_lower_bound(lbp_ref[...])

    _, _, st, _ = _hgrn_chunk_local(qm_ref[...], fm_ref[...], vm_ref[...], lb)

    f = lb + (1.0 - lb) * _sigmoid(f_ref[0])
    g = jnp.log(f)
    g1 = g.astype(BF16)
    r1 = g - g1.astype(F32)
    g2 = r1.astype(BF16)
    g3 = (r1 - g2.astype(F32)).astype(BF16)
    gcat = jnp.concatenate([g1, g2, g3], axis=1)
    r = lax.broadcasted_iota(jnp.int32, (slab, slab), 0)
    c = lax.broadcasted_iota(jnp.int32, (slab, slab), 1)
    tri_bd = ((r // ck == c // ck) & (r >= c)).astype(BF16)
    for si in range(seq // slab):
        rows = slice(si * slab, (si + 1) * slab)
        t = dot(tri_bd, gcat[rows])
        b_ref[rows, :] = t[:, :HEAD_W] + t[:, HEAD_W:2 * HEAD_W] + t[:, 2 * HEAD_W:]

    k = 1.0 - f
    b = b_ref[...]
    qe_ref[...] = (q_ref[0] * jnp.exp(b)).astype(BF16)
    ke_ref[...] = (k * jnp.exp(-b)).astype(BF16)
    v = v_ref[0]
    vb_ref[...] = v.astype(BF16)
    vt_ref[...] = v.T.astype(BF16)
    for ci in range(n_chunks):
        rows = slice(ci * ck, (ci + 1) * ck)
        b_last = b_ref[(ci + 1) * ck - 1:(ci + 1) * ck, :]
        dec_ref[ci] = jnp.exp(b_last)
        kd_ref[rows, :] = (k[rows] * jnp.exp(b_last - b_ref[rows, :])).astype(BF16)

    rp = lax.broadcasted_iota(jnp.int32, (pair, pair), 0)
    cp = lax.broadcasted_iota(jnp.int32, (pair, pair), 1)
    keep = (rp // ck == cp // ck) & (rp >= cp)
    for pi in range(seq // pair):
        rows = slice(pi * pair, (pi + 1) * pair)
        a_ref[rows, :] = jnp.where(keep, _dot_nt(qe_ref[rows, :], ke_ref[rows, :]), 0.0).astype(BF16)

    for pi in range(seq // pair):
        rows = slice(pi * pair, (pi + 1) * pair)
        o_ref[0, rows, :] = dot(a_ref[rows, :], vb_ref[rows, :])
    for ci in range(n_chunks):
        rows = slice(ci * ck, (ci + 1) * ck)
        ut_ref[ci] = dot(vt_ref[:, rows], kd_ref[rows, :])

    for ci in range(n_chunks):
        st_ref[ci] = st.astype(BF16)
        st = st * dec_ref[ci] + ut_ref[ci]
    s_ref[0, 0] = st.T

    for ci in range(n_chunks):
        rows = slice(ci * ck, (ci + 1) * ck)
        o_ref[0, rows, :] += _dot_nt(qe_ref[rows, :], st_ref[ci])


def _hgrn_prompt(z_p, z_small, hgrn_lb, batch, seq, heads):
    zp3 = z_p.reshape(batch, seq, z_p.shape[1])
    big = lambda g: pl.BlockSpec((1, seq, HEAD_W), lambda b, h, g=g: (b, 0, g * heads + h))
    meta = lambda g: pl.BlockSpec((N_META, HEAD_W), lambda b, h, g=g: (0, g * heads + h))
    return pl.pallas_call(
        _hgrn_prompt_kernel,
        grid=(batch, heads),
        in_specs=[big(G_HQ), big(G_HF), big(G_HI), meta(G_HQ), meta(G_HF), meta(G_HI),
                  pl.BlockSpec((hgrn_lb.shape[0], HEAD_W), lambda b, h: (0, h))],
        out_specs=[pl.BlockSpec((1, seq, HEAD_W), lambda b, h: (b, 0, h)),
                   pl.BlockSpec((1, 1, HEAD_W, HEAD_W), lambda b, h: (b, h, 0, 0))],
        out_shape=[jax.ShapeDtypeStruct((batch, seq, heads * HEAD_W), F32),
                   jax.ShapeDtypeStruct((batch, heads, HEAD_W, HEAD_W), F32)],
        scratch_shapes=[pltpu.VMEM((seq, HEAD_W), F32),
                        pltpu.VMEM((seq, HEAD_W), BF16), pltpu.VMEM((seq, HEAD_W), BF16),
                        pltpu.VMEM((seq, HEAD_W), BF16), pltpu.VMEM((seq, HEAD_W), BF16),
                        pltpu.VMEM((HEAD_W, seq), BF16), pltpu.VMEM((seq, HGRN_PAIR), BF16),
                        pltpu.VMEM((seq // HGRN_CHUNK, HEAD_W, HEAD_W), F32),
                        pltpu.VMEM((seq // HGRN_CHUNK, 1, HEAD_W), F32),
                        pltpu.VMEM((seq // HGRN_CHUNK, HEAD_W, HEAD_W), BF16)],
        compiler_params=pltpu.CompilerParams(dimension_semantics=("arbitrary", "arbitrary"),
                                             vmem_limit_bytes=VMEM_LIMIT),
        name="hgrn_prompt",
    )(zp3, zp3, zp3, z_small, z_small, z_small, hgrn_lb)


def _diff_lambda(lp):
    s01 = jnp.sum(lp[0:1] * lp[1:2], axis=-1, keepdims=True)
    s23 = jnp.sum(lp[2:3] * lp[3:4], axis=-1, keepdims=True)
    return jnp.exp(s01) - jnp.exp(s23) + LAMBDA_INIT


def _bias_by_distance(dist, bias_of_bucket):
    out = jnp.where(dist >= BUCKET_THR[1], bias_of_bucket(1), bias_of_bucket(0))
    for bkt in range(2, N_BUCKETS):
        out = jnp.where(dist >= BUCKET_THR[bkt], bias_of_bucket(bkt), out)
    return out


ATT_TQ = 256


def _colmax(s):
    return jnp.max(s.reshape(s.shape[0] // 8, 8, s.shape[1]), axis=0)


def _colsum(s):
    return jnp.sum(s.reshape(s.shape[0] // 8, 8, s.shape[1]), axis=0)


def _attn_prompt_kernel(rb_ref, lp_ref, q_ref, k_ref, v_ref, km_ref, vm_ref, o_ref,
                        kb_ref, vt_ref, vmt_ref, s_ref, bdiag_ref, bnear_ref, bmeta_ref):
    h = pl.program_id(0)
    b = pl.program_id(1)
    tq = ATT_TQ
    n_q = q_ref.shape[1] // tq
    log2e = 1.0 / math.log(2.0)
    bias_of_bucket = lambda bkt: rb_ref[bkt * rb_ref.shape[0] // N_BUCKETS + h] * log2e
    far_bias = bias_of_bucket(N_BUCKETS - 1)
    key = lax.broadcasted_iota(jnp.int32, (tq, tq), 0)
    qry = lax.broadcasted_iota(jnp.int32, (tq, tq), 1)

    @pl.when(b == 0)
    def _():
        bdiag_ref[...] = _bias_by_distance(qry - key, bias_of_bucket)
        bnear_ref[...] = _bias_by_distance(qry - key + tq, bias_of_bucket)
        keym = lax.broadcasted_iota(jnp.int32, (N_META, tq), 0)
        qrym = lax.broadcasted_iota(jnp.int32, (N_META, tq), 1)
        bmeta_ref[...] = _bias_by_distance(qrym + N_META - keym, bias_of_bucket)

    kb_ref[...] = k_ref[0].astype(BF16)
    vt_ref[...] = v_ref[0].T.astype(BF16)
    pad = jnp.zeros((HEAD_W - N_META, HEAD_W), F32)
    vmt_ref[...] = jnp.concatenate([vm_ref[...], pad], axis=0).T.astype(BF16)
    km_bf = km_ref[...].astype(BF16)
    lam = _diff_lambda(lp_ref[...])
    causal = qry >= key
    dot = functools.partial(jnp.dot, preferred_element_type=F32)

    for qi in range(n_q):
        par = qi % 2
        q = q_ref[0, qi * tq:(qi + 1) * tq, :] * (AT_DQK ** -0.5 * log2e)
        lane = lax.broadcasted_iota(jnp.int32, q.shape, 1)
        qc = (jnp.where(lane < AT_DQK, q, 0.0).astype(BF16), jnp.where(lane >= AT_DQK, q, 0.0).astype(BF16))
        o_c = []
        for c in range(2):
            sm = _dot_nt(km_bf, qc[c]) + (bmeta_ref[...] if qi == 0 else far_bias)
            m8 = _colmax(sm)
            for j in range(qi + 1):
                s = _dot_nt(kb_ref[j * tq:(j + 1) * tq, :], qc[c])
                if j == qi:
                    s = jnp.where(causal, s + bdiag_ref[...], NEG_INF)
                elif j == qi - 1:
                    s = s + bnear_ref[...]
                s_ref[par, c, j * tq:(j + 1) * tq, :] = s
                m8 = jnp.maximum(m8, _colmax(s) + far_bias if j < qi - 1 else _colmax(s))
            m = jnp.max(m8, axis=0, keepdims=True)
            m_far = m - far_bias
            pm = jnp.exp2(sm - m)
            l8 = _colsum(pm)
            acc = dot(vmt_ref[:, :N_META], pm.astype(BF16))
            for j in range(qi + 1):
                p = jnp.exp2(s_ref[par, c, j * tq:(j + 1) * tq, :] - (m_far if j < qi - 1 else m))
                l8 = l8 + _colsum(p)
                acc = acc + dot(vt_ref[:, j * tq:(j + 1) * tq], p.astype(BF16))
            o_c.append(acc / jnp.sum(l8, axis=0, keepdims=True))
        o_ref[0, qi * tq:(qi + 1) * tq, :] = (o_c[0] - lam * o_c[1]).T


def _attn_prompt(z_p, z_small, rel_bias, diff_lambda, batch, seq, heads):
    assert CONST_BIAS_DIST <= ATT_TQ + 1 and CONST_BIAS_DIST <= N_META + 1 + ATT_TQ - N_META
    zp3 = z_p.reshape(batch, seq, z_p.shape[1])
    tq = ATT_TQ
    full = lambda g: pl.BlockSpec((1, seq, HEAD_W), lambda h, b, g=g: (b, 0, g * heads + h))
    meta = lambda g: pl.BlockSpec((N_META, HEAD_W), lambda h, b, g=g: (0, g * heads + h))
    return pl.pallas_call(
        _attn_prompt_kernel,
        grid=(heads, batch),
        in_specs=[pl.BlockSpec(memory_space=pltpu.SMEM),
                  pl.BlockSpec(diff_lambda.shape, lambda h, b: (0, 0)),
                  full(G_AQ), full(G_AK), full(G_AV), meta(G_AK), meta(G_AV)],
        out_specs=pl.BlockSpec((1, seq, HEAD_W), lambda h, b: (b, 0, h)),
        out_shape=jax.ShapeDtypeStruct((batch, seq, heads * HEAD_W), F32),
        scratch_shapes=[pltpu.VMEM((seq, HEAD_W), BF16), pltpu.VMEM((HEAD_W, seq), BF16),
                        pltpu.VMEM((HEAD_W, HEAD_W), BF16),
                        pltpu.VMEM((2, 2, seq, tq), F32),
                        pltpu.VMEM((tq, tq), F32), pltpu.VMEM((tq, tq), F32),
                        pltpu.VMEM((N_META, tq), F32)],
        compiler_params=pltpu.CompilerParams(dimension_semantics=("arbitrary", "arbitrary"),
                                             vmem_limit_bytes=VMEM_LIMIT),
        name="attn_prompt",
    )(rel_bias.reshape(-1), diff_lambda, zp3, zp3, zp3, z_small, z_small)


def _head_rmsnorm(o, g):
    parts = []
    for hh in range(o.shape[1] // HEAD_W):
        oh = o[:, hh * HEAD_W:(hh + 1) * HEAD_W]
        ms = jnp.mean(oh * oh, axis=-1, keepdims=True)
        parts.append(oh * lax.rsqrt(ms + RMS_EPS) * g)
    return jnp.concatenate(parts, axis=1)


MERGE_SUB = 256


def _merge_kernel(ohg_ref, hg_ref, oat_ref, ag_ref, x_ref, gh_ref, ga_ref, w_ref, gf_ref, y_ref):
    tm = x_ref.shape[0]
    sub = min(MERGE_SUB, tm)
    subs = [slice(s, s + sub) for s in range(0, tm, sub)]
    mixed = []
    for r in subs:
        mix_h = _head_rmsnorm(ohg_ref[r, :], gh_ref[...]) * _silu(hg_ref[r, :])
        mix_a = _head_rmsnorm(oat_ref[r, :], ga_ref[...]) * (1.0 - LAMBDA_INIT) * _silu(ag_ref[r, :])
        mixed.append(jnp.concatenate([mix_h, mix_a], axis=1).astype(BF16))
    proj = [jnp.dot(m, w_ref[...], preferred_element_type=F32) for m in mixed]
    for r, p in zip(subs, proj):
        hres = x_ref[r, :] + p
        ms = jnp.mean(hres * hres, axis=-1, keepdims=True)
        y_ref[r, :] = hres * lax.rsqrt(ms + RMS_EPS) * gf_ref[...]


def _merge(o_hg, o_at, z, x, g_hg, g_at, w_out_bf, g_final, tm, heads):
    rows, d = x.shape
    gw = heads * HEAD_W
    row_blk = lambda w, c: pl.BlockSpec((tm, w), lambda i, c=c: (i, c))
    const = lambda shape: pl.BlockSpec(shape, lambda i: (0, 0))
    return pl.pallas_call(
        _merge_kernel,
        grid=(rows // tm,),
        in_specs=[row_blk(gw, 0), row_blk(gw, G_HG), row_blk(gw, 0), row_blk(gw, G_AG), row_blk(d, 0),
                  const((1, HEAD_W)), const((1, HEAD_W)),
                  pl.BlockSpec(w_out_bf.shape, lambda i: (0, 0), pipeline_mode=pl.Buffered(1)),
                  const((1, d))],
        out_specs=row_blk(d, 0),
        out_shape=jax.ShapeDtypeStruct((rows, d), F32),
        compiler_params=pltpu.CompilerParams(dimension_semantics=("arbitrary",),
                                             vmem_limit_bytes=VMEM_LIMIT),
        name="merge",
    )(o_hg, z, o_at, z, x, g_hg, g_at, w_out_bf, g_final)


def _transpose_rows_to_cols(x):
    pad = jnp.zeros((HEAD_W - x.shape[0], HEAD_W), F32)
    return jnp.concatenate([x, pad], axis=0).T


def _hgrn_step_kernel(q_ref, f_ref, v_ref, lbp_ref, s_ref, o_ref, sn_ref):
    lb = _forget_lower_bound(lbp_ref[...])
    f = lb + (1.0 - lb) * _sigmoid(f_ref[...])
    k = 1.0 - f
    b = jnp.log(f)
    qe = q_ref[...] * jnp.exp(b)
    ke = k * jnp.exp(-b)
    a = jnp.sum(qe * ke, axis=-1, keepdims=True)
    v = v_ref[...]
    decay_t = _transpose_rows_to_cols(jnp.exp(b))
    kd_t = _transpose_rows_to_cols(k * jnp.exp(b - b))
    qe_t = _transpose_rows_to_cols(qe)
    for i in range(q_ref.shape[0]):
        s = s_ref[0, i, 0]
        vi = v[i:i + 1]
        o_ref[i:i + 1, :] = jnp.sum(qe_t[:, i:i + 1] * s, axis=0, keepdims=True) + a[i:i + 1] * vi
        sn_ref[0, i, 0] = decay_t[:, i:i + 1] * s + kd_t[:, i:i + 1] * vi


def _hgrn_step(z_samp, hgrn_lb, state, heads):
    db = z_samp.shape[0]
    col = lambda g: pl.BlockSpec((db, HEAD_W), lambda h, g=g: (0, g * heads + h))
    st_spec = pl.BlockSpec((1, db, 1, HEAD_W, HEAD_W), lambda h: (0, 0, h, 0, 0))
    return pl.pallas_call(
        _hgrn_step_kernel,
        grid=(heads,),
        in_specs=[col(G_HQ), col(G_HF), col(G_HI),
                  pl.BlockSpec((hgrn_lb.shape[0], HEAD_W), lambda h: (0, h)), st_spec],
        out_specs=[pl.BlockSpec((db, HEAD_W), lambda h: (0, h)), st_spec],
        out_shape=[jax.ShapeDtypeStruct((db, heads * HEAD_W), F32),
                   jax.ShapeDtypeStruct(state.shape, F32)],
        compiler_params=pltpu.CompilerParams(dimension_semantics=("arbitrary",),
                                             vmem_limit_bytes=VMEM_LIMIT),
        name="hgrn_step",
    )(z_samp, z_samp, z_samp, hgrn_lb, state)


DEC_PAGES_PER_STEP = 8


def _proj_decode_kernel(pt_ref, x_ref, g_ref, w_ref, rbt_ref, lp_ref, q_ref, kn_ref, vn_ref, *rest,
                        heads, page, n_steps):
    npp = DEC_PAGES_PER_STEP
    k_refs = rest[:npp]
    v_refs = rest[npp:2 * npp]
    z_ref, o_ref, xn_ref, qs_ref, m_ref, l_ref, acc_ref, blast_ref = rest[2 * npp:]

    @pl.when(pl.program_id(1) == 0)
    def _():
        x = x_ref[...]
        ms = jnp.mean(x * x, axis=-1, keepdims=True)
        xn_ref[...] = (x * lax.rsqrt(ms + RMS_EPS) * g_ref[...]).astype(BF16)

    j = pl.program_id(1) % n_steps
    rows = 2 * heads
    lanes = page * heads
    rbt = jnp.concatenate([rbt_ref[...], rbt_ref[...]], axis=0)
    far_bias = rbt[:, N_BUCKETS - 1:N_BUCKETS]
    r_id = lax.broadcasted_iota(jnp.int32, (rows, lanes), 0)
    l_id = lax.broadcasted_iota(jnp.int32, (rows, lanes), 1)
    own_head = l_id % heads == r_id % heads

    def stacked(x):
        lane = lax.broadcasted_iota(jnp.int32, x.shape, 1)
        return jnp.concatenate([jnp.where(lane < AT_DQK, x, 0.0), jnp.where(lane >= AT_DQK, x, 0.0)], axis=0)

    @pl.when(j == 0)
    def _():
        qs_ref[...] = stacked(q_ref[0] * (AT_DQK ** -0.5)).astype(BF16)
        m_ref[...] = jnp.full(m_ref.shape, NEG_INF, F32)
        l_ref[...] = jnp.zeros(l_ref.shape, F32)
        acc_ref[...] = jnp.zeros(acc_ref.shape, F32)
        blast_ref[...] = _bias_by_distance(page - l_id // heads, lambda bkt: rbt[:, bkt:bkt + 1])

    qs = qs_ref[...]
    s_parts = []
    for i in range(npp):
        s_i = _dot_nt(qs, k_refs[i][...].astype(BF16))
        if i == npp - 1:
            s_i = s_i + jnp.where(j == n_steps - 1, blast_ref[...], far_bias)
        else:
            s_i = s_i + far_bias
        s_parts.append(jnp.where(own_head, s_i, NEG_INF))

    m_old = m_ref[...]
    m_blk = s_parts[0]
    for s_i in s_parts[1:]:
        m_blk = jnp.maximum(m_blk, s_i)
    m_new = jnp.maximum(m_old, jnp.max(m_blk, axis=-1, keepdims=True))
    alpha = jnp.exp(m_old - m_new)

    z_ref[...] = jnp.dot(xn_ref[...], w_ref[...], preferred_element_type=F32)

    l_blk = jnp.zeros((rows, lanes), F32)
    pv = jnp.zeros(acc_ref.shape, F32)
    for i in range(npp):
        p_i = jnp.exp(s_parts[i] - m_new)
        l_blk = l_blk + p_i
        pv = pv + jnp.dot(p_i.astype(BF16), v_refs[i][...].astype(BF16), preferred_element_type=F32)
    l_ref[...] = alpha * l_ref[...] + jnp.sum(l_blk, axis=-1, keepdims=True)
    acc_ref[...] = alpha * acc_ref[...] + pv
    m_ref[...] = m_new

    @pl.when(j == n_steps - 1)
    def _():
        kn = jnp.concatenate([kn_ref[0], kn_ref[0]], axis=0)
        vn = jnp.concatenate([vn_ref[0], vn_ref[0]], axis=0)
        s_new = jnp.sum(qs.astype(F32) * kn, axis=-1, keepdims=True) + rbt[:, 0:1]
        m_old = m_ref[...]
        m_fin = jnp.maximum(m_old, s_new)
        alpha = jnp.exp(m_old - m_fin)
        p_new = jnp.exp(s_new - m_fin)
        l_fin = alpha * l_ref[...] + p_new
        o = (alpha * acc_ref[...] + p_new * vn) / l_fin
        o_ref[0] = o[:heads] - _diff_lambda(lp_ref[...]) * o[heads:]


PD_TM = 1024
PD_TN = 256


def _proj_decode(x, g, w_bf, q_s, k_s, v_s, cache_k, cache_v, page_table, rel_bias, diff_lambda):
    rows, d = x.shape
    n = w_bf.shape[1]
    db, n_pages = page_table.shape
    n_phys, page, heads, _ = cache_k.shape
    npp = DEC_PAGES_PER_STEP
    n_steps = n_pages // npp
    grid = (rows // PD_TM, n // PD_TN)
    seq_per_row_tile = grid[1] // n_steps
    assert CONST_BIAS_DIST <= page + 1 and n_pages % npp == 0
    assert rows % PD_TM == 0 and n % PD_TN == 0 and grid[1] % n_steps == 0 and grid[0] * seq_per_row_tile == db
    ck = cache_k.reshape(n_phys, page * heads, HEAD_W)
    cv = cache_v.reshape(n_phys, page * heads, HEAD_W)
    seq_of = lambda i, t: i * seq_per_row_tile + t // n_steps
    vec = pl.BlockSpec((1, heads, HEAD_W), lambda i, t, pt: (seq_of(i, t), 0, 0))
    page_spec = lambda p: pl.BlockSpec(
        (None, page * heads, HEAD_W), lambda i, t, pt, p=p: (pt[seq_of(i, t), (t % n_steps) * npp + p], 0, 0))
    srows = 2 * heads
    kernel = functools.partial(_proj_decode_kernel, heads=heads, page=page, n_steps=n_steps)
    return pl.pallas_call(
        kernel,
        grid_spec=pltpu.PrefetchScalarGridSpec(
            num_scalar_prefetch=1,
            grid=grid,
            in_specs=[pl.BlockSpec((PD_TM, d), lambda i, t, pt: (i, 0)),
                      pl.BlockSpec((1, d), lambda i, t, pt: (0, 0)),
                      pl.BlockSpec((d, PD_TN), lambda i, t, pt: (0, t)),
                      pl.BlockSpec((heads, N_BUCKETS), lambda i, t, pt: (0, 0)),
                      pl.BlockSpec(diff_lambda.shape, lambda i, t, pt: (0, 0)),
                      vec, vec, vec]
                     + [page_spec(p) for p in range(npp)] + [page_spec(p) for p in range(npp)],
            out_specs=[pl.BlockSpec((PD_TM, PD_TN), lambda i, t, pt: (i, t)),
                       pl.BlockSpec((1, heads, HEAD_W), lambda i, t, pt: (seq_of(i, t), 0, 0))],
            scratch_shapes=[pltpu.VMEM((PD_TM, d), BF16),
                            pltpu.VMEM((srows, HEAD_W), BF16), pltpu.VMEM((srows, 1), F32),
                            pltpu.VMEM((srows, 1), F32), pltpu.VMEM((srows, HEAD_W), F32),
                            pltpu.VMEM((srows, page * heads), F32)]),
        out_shape=[jax.ShapeDtypeStruct((rows, n), F32), jax.ShapeDtypeStruct((db, heads, HEAD_W), F32)],
        compiler_params=pltpu.CompilerParams(dimension_semantics=("arbitrary", "arbitrary"),
                                             vmem_limit_bytes=VMEM_LIMIT),
        name="proj_decode",
    )(page_table, x, g, w_bf, rel_bias.T, diff_lambda, q_s, k_s, v_s, *([ck] * npp), *([cv] * npp))


def kernel(x_prompt, x_sample, cache_k, cache_v, state_hgrn, page_table, meta_tokens, rel_bias, hgrn_lb,
           norm_g, w_in, hgrn_norm_g, diff_norm_g, diff_lambda, w_out, final_norm_g):
    batch, seq, d = x_prompt.shape
    db = x_sample.shape[0]
    heads = cache_k.shape[3]
    gw = heads * HEAD_W
    assert norm_g.shape[0] == 1 and x_sample.shape[1] == 1, "one layer, one sample token per sequence"

    w_in_bf = w_in[0].astype(BF16)
    w_out_bf = w_out[0].astype(BF16)
    g_in = norm_g[0].reshape(1, d)
    g_final = final_norm_g.reshape(1, d)
    g_hg = hgrn_norm_g[0].reshape(1, HEAD_W)
    g_at = diff_norm_g[0].reshape(1, HEAD_W)
    lam_p = diff_lambda[0]

    xp = x_prompt.reshape(batch * seq, d)
    xs = x_sample.reshape(db, d)
    small = jnp.concatenate([meta_tokens.astype(F32), xs], axis=0)

    z_small = _proj(small, g_in, w_in_bf, tm=small.shape[0], tn=1024)
    z_samp = z_small[N_META:]
    samp_heads = lambda g: z_samp[:, g * gw:(g + 1) * gw].reshape(db, heads, HEAD_W)
    q_s, k_s, v_s = samp_heads(G_AQ), samp_heads(G_AK), samp_heads(G_AV)

    z_p, o_at_s = _proj_decode(xp, g_in, w_in_bf, q_s, k_s, v_s, cache_k[0], cache_v[0], page_table,
                               rel_bias, lam_p)

    o_hg_p, s_p = _hgrn_prompt(z_p, z_small, hgrn_lb, batch, seq, heads)
    o_at_p = _attn_prompt(z_p, z_small, rel_bias, lam_p, batch, seq, heads)
    y_p = _merge(o_hg_p.reshape(batch * seq, gw), o_at_p.reshape(batch * seq, gw), z_p, xp,
                 g_hg, g_at, w_out_bf, g_final, tm=512, heads=heads)

    o_hg_s, s_s = _hgrn_step(z_samp, hgrn_lb, state_hgrn, heads)
    y_s = _merge(o_hg_s, o_at_s.reshape(db, gw), z_samp, xs, g_hg, g_at, w_out_bf, g_final, tm=db, heads=heads)

    def with_meta(g):
        meta_rows = jnp.broadcast_to(z_small[None, :N_META, g * gw:(g + 1) * gw], (batch, N_META, gw))
        real_rows = z_p[:, g * gw:(g + 1) * gw].reshape(batch, seq, gw)
        return jnp.concatenate([meta_rows, real_rows], axis=1).reshape(1, batch, N_META + seq, heads, HEAD_W)

    return (y_p.reshape(batch, seq, d), y_s.reshape(db, 1, d), with_meta(G_AK), with_meta(G_AV),
            s_p[None], k_s.reshape(1, db, 1, heads, HEAD_W), v_s.reshape(1, db, 1, heads, HEAD_W), s_s)
```

```python
import functools
import math

import numpy as np
import jax
import jax.numpy as jnp
from jax import lax
from jax.experimental import pallas as pl
from jax.experimental.pallas import tpu as pltpu

F32 = jnp.float32
BF16 = jnp.bfloat16

N_META = 16
HEAD_W = 128
AT_DQK = 64
N_BUCKETS = 32
MAX_DISTANCE = 128
MAX_EXACT = N_BUCKETS // 2
HGRN_CHUNK = 64
RMS_EPS = 1e-6
NEG_INF = -1e30
LAMBDA_INIT = 0.8 - 0.6 * math.exp(-0.3 * 0)
G_HQ, G_HF, G_HI, G_HG, G_AQ, G_AK, G_AV, G_AG = range(8)

VMEM_LIMIT = 56 * 1024 * 1024


def _bucket_thresholds():
    n = np.arange(0, 2 * MAX_DISTANCE)
    nl = np.maximum(n, MAX_EXACT).astype(np.float32)
    large = MAX_EXACT + (np.log(nl / np.float32(MAX_EXACT)) / np.float32(math.log(MAX_DISTANCE / MAX_EXACT))
                         * np.float32(N_BUCKETS - MAX_EXACT)).astype(np.int32)
    bucket = np.where(n < MAX_EXACT, n, np.minimum(large, N_BUCKETS - 1))
    return [int(np.argmax(bucket >= b)) for b in range(N_BUCKETS)]


BUCKET_THR = _bucket_thresholds()
CONST_BIAS_DIST = BUCKET_THR[N_BUCKETS - 1]


def _sigmoid(x):
    return 1.0 / (1.0 + jnp.exp(-x))


def _silu(x):
    return x * _sigmoid(x)


def _dot_nt(a, b):
    return lax.dot_general(a, b, (((1,), (1,)), ((), ())), preferred_element_type=F32)


def _dot_tn(a, b):
    return lax.dot_general(a, b, (((0,), (0,)), ((), ())), preferred_element_type=F32)


GROUP_W = 1024


def _proj_small_kernel(x_ref, g_ref, w_ref, z_ref, wb_ref, xn_ref):
    @pl.when(pl.program_id(0) == 0)
    def _():
        x = x_ref[...]
        ms = jnp.mean(x * x, axis=-1, keepdims=True)
        xn_ref[...] = (x * lax.rsqrt(ms + RMS_EPS) * g_ref[...]).astype(BF16)

    w_bf = w_ref[...].astype(BF16)
    wb_ref[...] = w_bf
    z_ref[...] = jnp.dot(xn_ref[...], w_bf, preferred_element_type=F32)


def _proj_small(x, g, w):
    rows, d = x.shape
    n = w.shape[1]
    return pl.pallas_call(
        _proj_small_kernel,
        grid=(n // GROUP_W,),
        in_specs=[pl.BlockSpec((rows, d), lambda j: (0, 0)),
                  pl.BlockSpec((1, d), lambda j: (0, 0)),
                  pl.BlockSpec((d, GROUP_W), lambda j: (0, j))],
        out_specs=[pl.BlockSpec((None, rows, GROUP_W), lambda j: (j, 0, 0)),
                   pl.BlockSpec((d, GROUP_W), lambda j: (0, j))],
        out_shape=[jax.ShapeDtypeStruct((n // GROUP_W, rows, GROUP_W), F32),
                   jax.ShapeDtypeStruct((d, n), BF16)],
        scratch_shapes=[pltpu.VMEM((rows, d), BF16)],
        compiler_params=pltpu.CompilerParams(dimension_semantics=("arbitrary",),
                                             vmem_limit_bytes=VMEM_LIMIT),
        name="proj_small",
    )(x, g, w)


def _forget_lower_bound(lbp):
    e = jnp.exp(lbp - jnp.max(lbp, axis=0, keepdims=True))
    return e[0:1] / jnp.sum(e, axis=0, keepdims=True)


def _cumsum_rows(g, tri_bf):
    g1 = g.astype(BF16)
    r1 = g - g1.astype(F32)
    g2 = r1.astype(BF16)
    g3 = (r1 - g2.astype(F32)).astype(BF16)
    dot = functools.partial(jnp.dot, preferred_element_type=F32)
    return dot(tri_bf, g1) + dot(tri_bf, g2) + dot(tri_bf, g3)


def _hgrn_chunk_local(q, hf, v, lb):
    c = q.shape[0]
    row = lax.broadcasted_iota(jnp.int32, (c, c), 0)
    col = lax.broadcasted_iota(jnp.int32, (c, c), 1)
    causal = row >= col
    f = lb + (1.0 - lb) * _sigmoid(hf)
    k = 1.0 - f
    b = _cumsum_rows(jnp.log(f), causal.astype(BF16))
    qe = (q * jnp.exp(b)).astype(BF16)
    ke = (k * jnp.exp(-b)).astype(BF16)
    a = jnp.where(causal, _dot_nt(qe, ke), 0.0)
    v_bf = v.astype(BF16)
    intra = jnp.dot(a.astype(BF16), v_bf, preferred_element_type=F32)
    b_last = b[c - 1:c]
    kd = (k * jnp.exp(b_last - b)).astype(BF16)
    return qe, intra, _dot_tn(v_bf, kd), jnp.exp(b_last)


HGRN_SLAB = 256
HGRN_PAIR = 128


def _hgrn_prompt_kernel(q_ref, f_ref, v_ref, qm_ref, fm_ref, vm_ref, lbp_ref, o_ref, s_ref,
                        b_ref, qe_ref, ke_ref, kd_ref, vb_ref, vt_ref, a_ref, ut_ref, dec_ref, st_ref):
    ck, slab, pair = HGRN_CHUNK, HGRN_SLAB, HGRN_PAIR
    seq = q_ref.shape[1]
    n_chunks = seq // ck
    dot = functools.partial(jnp.dot, preferred_element_type=F32)
    lb = _forget_lower_bound(lbp_ref[...])

    _, _, st, _ = _hgrn_chunk_local(qm_ref[...], fm_ref[...], vm_ref[...], lb)

    f = lb + (1.0 - lb) * _sigmoid(f_ref[0])
    g = jnp.log(f)
    g1 = g.astype(BF16)
    r1 = g - g1.astype(F32)
    g2 = r1.astype(BF16)
    g3 = (r1 - g2.astype(F32)).astype(BF16)
    gcat = jnp.concatenate([g1, g2, g3], axis=1)
    r = lax.broadcasted_iota(jnp.int32, (slab, slab), 0)
    c = lax.broadcasted_iota(jnp.int32, (slab, slab), 1)
    tri_bd = ((r // ck == c // ck) & (r >= c)).astype(BF16)
    for si in range(seq // slab):
        rows = slice(si * slab, (si + 1) * slab)
        t = dot(tri_bd, gcat[rows])
        b_ref[rows, :] = t[:, :HEAD_W] + t[:, HEAD_W:2 * HEAD_W] + t[:, 2 * HEAD_W:]

    k = 1.0 - f
    b = b_ref[...]
    qe_ref[...] = (q_ref[0] * jnp.exp(b)).astype(BF16)
    ke_ref[...] = (k * jnp.exp(-b)).astype(BF16)
    v = v_ref[0]
    vb_ref[...] = v.astype(BF16)
    vt_ref[...] = v.T.astype(BF16)
    for ci in range(n_chunks):
        rows = slice(ci * ck, (ci + 1) * ck)
        b_last = b_ref[(ci + 1) * ck - 1:(ci + 1) * ck, :]
        dec_ref[ci] = jnp.exp(b_last)
        kd_ref[rows, :] = (k[rows] * jnp.exp(b_last - b_ref[rows, :])).astype(BF16)

    rp = lax.broadcasted_iota(jnp.int32, (pair, pair), 0)
    cp = lax.broadcasted_iota(jnp.int32, (pair, pair), 1)
    keep = (rp // ck == cp // ck) & (rp >= cp)
    for pi in range(seq // pair):
        rows = slice(pi * pair, (pi + 1) * pair)
        a_ref[rows, :] = jnp.where(keep, _dot_nt(qe_ref[rows, :], ke_ref[rows, :]), 0.0).astype(BF16)

    for pi in range(seq // pair):
        rows = slice(pi * pair, (pi + 1) * pair)
        o_ref[0, rows, :] = dot(a_ref[rows, :], vb_ref[rows, :])
    for ci in range(n_chunks):
        rows = slice(ci * ck, (ci + 1) * ck)
        ut_ref[ci] = dot(vt_ref[:, rows], kd_ref[rows, :])

    for ci in range(n_chunks):
        st_ref[ci] = st.astype(BF16)
        st = st * dec_ref[ci] + ut_ref[ci]
    s_ref[0, 0] = st.T

    for ci in range(n_chunks):
        rows = slice(ci * ck, (ci + 1) * ck)
        o_ref[0, rows, :] += _dot_nt(qe_ref[rows, :], st_ref[ci])


def _hgrn_prompt(z_p, z_small, hgrn_lb, batch, seq, heads):
    zp3 = z_p.reshape(z_p.shape[0], batch, seq, GROUP_W)
    big = lambda g: pl.BlockSpec((None, 1, seq, HEAD_W), lambda b, h, g=g: (g, b, 0, h))
    meta = lambda g: pl.BlockSpec((None, N_META, HEAD_W), lambda b, h, g=g: (g, 0, h))
    return pl.pallas_call(
        _hgrn_prompt_kernel,
        grid=(batch, heads),
        in_specs=[big(G_HQ), big(G_HF), big(G_HI), meta(G_HQ), meta(G_HF), meta(G_HI),
                  pl.BlockSpec((hgrn_lb.shape[0], HEAD_W), lambda b, h: (0, h))],
        out_specs=[pl.BlockSpec((1, seq, HEAD_W), lambda b, h: (b, 0, h)),
                   pl.BlockSpec((1, 1, HEAD_W, HEAD_W), lambda b, h: (b, h, 0, 0))],
        out_shape=[jax.ShapeDtypeStruct((batch, seq, heads * HEAD_W), F32),
                   jax.ShapeDtypeStruct((batch, heads, HEAD_W, HEAD_W), F32)],
        scratch_shapes=[pltpu.VMEM((seq, HEAD_W), F32),
                        pltpu.VMEM((seq, HEAD_W), BF16), pltpu.VMEM((seq, HEAD_W), BF16),
                        pltpu.VMEM((seq, HEAD_W), BF16), pltpu.VMEM((seq, HEAD_W), BF16),
                        pltpu.VMEM((HEAD_W, seq), BF16), pltpu.VMEM((seq, HGRN_PAIR), BF16),
                        pltpu.VMEM((seq // HGRN_CHUNK, HEAD_W, HEAD_W), F32),
                        pltpu.VMEM((seq // HGRN_CHUNK, 1, HEAD_W), F32),
                        pltpu.VMEM((seq // HGRN_CHUNK, HEAD_W, HEAD_W), BF16)],
        compiler_params=pltpu.CompilerParams(dimension_semantics=("arbitrary", "arbitrary"),
                                             vmem_limit_bytes=VMEM_LIMIT),
        name="hgrn_prompt",
    )(zp3, zp3, zp3, z_small, z_small, z_small, hgrn_lb)


def _diff_lambda(lp):
    s01 = jnp.sum(lp[0:1] * lp[1:2], axis=-1, keepdims=True)
    s23 = jnp.sum(lp[2:3] * lp[3:4], axis=-1, keepdims=True)
    return jnp.exp(s01) - jnp.exp(s23) + LAMBDA_INIT


def _bias_by_distance(dist, bias_of_bucket):
    out = jnp.where(dist >= BUCKET_THR[1], bias_of_bucket(1), bias_of_bucket(0))
    for bkt in range(2, N_BUCKETS):
        out = jnp.where(dist >= BUCKET_THR[bkt], bias_of_bucket(bkt), out)
    return out


ATT_TQ = 256


def _colmax(s):
    return jnp.max(s.reshape(s.shape[0] // 8, 8, s.shape[1]), axis=0)


def _colsum(s):
    return jnp.sum(s.reshape(s.shape[0] // 8, 8, s.shape[1]), axis=0)


def _attn_prompt_kernel(rb_ref, lp_ref, q_ref, k_ref, v_ref, km_ref, vm_ref, o_ref,
                        kb_ref, vt_ref, vmt_ref, s_ref, bdiag_ref, bnear_ref, bmeta_ref):
    h = pl.program_id(0)
    b = pl.program_id(1)
    tq = ATT_TQ
    n_q = q_ref.shape[1] // tq
    log2e = 1.0 / math.log(2.0)
    bias_of_bucket = lambda bkt: rb_ref[bkt * rb_ref.shape[0] // N_BUCKETS + h] * log2e
    far_bias = bias_of_bucket(N_BUCKETS - 1)
    key = lax.broadcasted_iota(jnp.int32, (tq, tq), 0)
    qry = lax.broadcasted_iota(jnp.int32, (tq, tq), 1)

    @pl.when(b == 0)
    def _():
        bdiag_ref[...] = _bias_by_distance(qry - key, bias_of_bucket)
        bnear_ref[...] = _bias_by_distance(qry - key + tq, bias_of_bucket)
        keym = lax.broadcasted_iota(jnp.int32, (N_META, tq), 0)
        qrym = lax.broadcasted_iota(jnp.int32, (N_META, tq), 1)
        bmeta_ref[...] = _bias_by_distance(qrym + N_META - keym, bias_of_bucket)

    kb_ref[...] = k_ref[0].astype(BF16)
    vt_ref[...] = v_ref[0].T.astype(BF16)
    pad = jnp.zeros((HEAD_W - N_META, HEAD_W), F32)
    vmt_ref[...] = jnp.concatenate([vm_ref[...], pad], axis=0).T.astype(BF16)
    km_bf = km_ref[...].astype(BF16)
    lam = _diff_lambda(lp_ref[...])
    causal = qry >= key
    dot = functools.partial(jnp.dot, preferred_element_type=F32)

    def pass_a(qi):
        par = qi % 2
        q = q_ref[0, qi * tq:(qi + 1) * tq, :] * (AT_DQK ** -0.5 * log2e)
        lane = lax.broadcasted_iota(jnp.int32, q.shape, 1)
        qc = (jnp.where(lane < AT_DQK, q, 0.0).astype(BF16), jnp.where(lane >= AT_DQK, q, 0.0).astype(BF16))
        out = []
        for c in range(2):
            sm = _dot_nt(km_bf, qc[c]) + (bmeta_ref[...] if qi == 0 else far_bias)
            m8 = _colmax(sm)
            for j in range(qi + 1):
                s = _dot_nt(kb_ref[j * tq:(j + 1) * tq, :], qc[c])
                if j == qi:
                    s = jnp.where(causal, s + bdiag_ref[...], NEG_INF)
                elif j == qi - 1:
                    s = s + bnear_ref[...]
                s_ref[par, c, j * tq:(j + 1) * tq, :] = s
                m8 = jnp.maximum(m8, _colmax(s) + far_bias if j < qi - 1 else _colmax(s))
            out.append((sm, jnp.max(m8, axis=0, keepdims=True)))
        return out

    def pass_b(qi, stats):
        par = qi % 2
        o_c = []
        for c in range(2):
            sm, m = stats[c]
            m_far = m - far_bias
            pm = jnp.exp2(sm - m)
            l8 = _colsum(pm)
            acc = dot(vmt_ref[:, :N_META], pm.astype(BF16))
            for j in range(qi + 1):
                p = jnp.exp2(s_ref[par, c, j * tq:(j + 1) * tq, :] - (m_far if j < qi - 1 else m))
                l8 = l8 + _colsum(p)
                acc = acc + dot(vt_ref[:, j * tq:(j + 1) * tq], p.astype(BF16))
            o_c.append(acc / jnp.sum(l8, axis=0, keepdims=True))
        o_ref[0, qi * tq:(qi + 1) * tq, :] = (o_c[0] - lam * o_c[1]).T

    stats = pass_a(0)
    for qi in range(n_q):
        nxt = pass_a(qi + 1) if qi + 1 < n_q else None
        pass_b(qi, stats)
        stats = nxt


def _attn_prompt(z_p, z_small, rel_bias, diff_lambda, batch, seq, heads):
    assert CONST_BIAS_DIST <= ATT_TQ + 1 and CONST_BIAS_DIST <= N_META + 1 + ATT_TQ - N_META
    zp3 = z_p.reshape(z_p.shape[0], batch, seq, GROUP_W)
    tq = ATT_TQ
    full = lambda g: pl.BlockSpec((None, 1, seq, HEAD_W), lambda h, b, g=g: (g, b, 0, h))
    meta = lambda g: pl.BlockSpec((None, N_META, HEAD_W), lambda h, b, g=g: (g, 0, h))
    return pl.pallas_call(
        _attn_prompt_kernel,
        grid=(heads, batch),
        in_specs=[pl.BlockSpec(memory_space=pltpu.SMEM),
                  pl.BlockSpec(diff_lambda.shape, lambda h, b: (0, 0)),
                  full(G_AQ), full(G_AK), full(G_AV), meta(G_AK), meta(G_AV)],
        out_specs=pl.BlockSpec((1, seq, HEAD_W), lambda h, b: (b, 0, h)),
        out_shape=jax.ShapeDtypeStruct((batch, seq, heads * HEAD_W), F32),
        scratch_shapes=[pltpu.VMEM((seq, HEAD_W), BF16), pltpu.VMEM((HEAD_W, seq), BF16),
                        pltpu.VMEM((HEAD_W, HEAD_W), BF16),
                        pltpu.VMEM((2, 2, seq, tq), F32),
                        pltpu.VMEM((tq, tq), F32), pltpu.VMEM((tq, tq), F32),
                        pltpu.VMEM((N_META, tq), F32)],
        compiler_params=pltpu.CompilerParams(dimension_semantics=("arbitrary", "arbitrary"),
                                             vmem_limit_bytes=VMEM_LIMIT),
        name="attn_prompt",
    )(rel_bias.reshape(-1), diff_lambda, zp3, zp3, zp3, z_small, z_small)


def _head_rmsnorm(o, g):
    parts = []
    for hh in range(o.shape[1] // HEAD_W):
        oh = o[:, hh * HEAD_W:(hh + 1) * HEAD_W]
        ms = jnp.mean(oh * oh, axis=-1, keepdims=True)
        parts.append(oh * lax.rsqrt(ms + RMS_EPS) * g)
    return jnp.concatenate(parts, axis=1)


MERGE_SUB = 256


def _merge_kernel(ohg_ref, hg_ref, oat_ref, ag_ref, x_ref, gh_ref, ga_ref, w_ref, gf_ref, y_ref):
    tm = x_ref.shape[0]
    sub = min(MERGE_SUB, tm)
    subs = [slice(s, s + sub) for s in range(0, tm, sub)]
    mixed = []
    for r in subs:
        mix_h = _head_rmsnorm(ohg_ref[r, :], gh_ref[...]) * _silu(hg_ref[r, :])
        mix_a = _head_rmsnorm(oat_ref[r, :], ga_ref[...]) * (1.0 - LAMBDA_INIT) * _silu(ag_ref[r, :])
        mixed.append(jnp.concatenate([mix_h, mix_a], axis=1).astype(BF16))
    proj = [jnp.dot(m, w_ref[...], preferred_element_type=F32) for m in mixed]
    for r, p in zip(subs, proj):
        hres = x_ref[r, :] + p
        ms = jnp.mean(hres * hres, axis=-1, keepdims=True)
        y_ref[r, :] = hres * lax.rsqrt(ms + RMS_EPS) * gf_ref[...]


def _merge(o_hg, o_at, z, x, g_hg, g_at, w_out_bf, g_final, tm, heads):
    rows, d = x.shape
    gw = heads * HEAD_W
    row_blk = lambda w: pl.BlockSpec((tm, w), lambda i: (i, 0))
    gate = lambda g: pl.BlockSpec((None, tm, gw), lambda i, g=g: (g, i, 0))
    const = lambda shape: pl.BlockSpec(shape, lambda i: (0, 0))
    return pl.pallas_call(
        _merge_kernel,
        grid=(rows // tm,),
        in_specs=[row_blk(gw), gate(G_HG), row_blk(gw), gate(G_AG), row_blk(d),
                  const((1, HEAD_W)), const((1, HEAD_W)),
                  pl.BlockSpec(w_out_bf.shape, lambda i: (0, 0), pipeline_mode=pl.Buffered(1)),
                  const((1, d))],
        out_specs=row_blk(d),
        out_shape=jax.ShapeDtypeStruct((rows, d), F32),
        compiler_params=pltpu.CompilerParams(dimension_semantics=("arbitrary",),
                                             vmem_limit_bytes=VMEM_LIMIT),
        name="merge",
    )(o_hg, z, o_at, z, x, g_hg, g_at, w_out_bf, g_final)


def _transpose_rows_to_cols(x):
    pad = jnp.zeros((HEAD_W - x.shape[0], HEAD_W), F32)
    return jnp.concatenate([x, pad], axis=0).T


def _hgrn_step_kernel(q_ref, f_ref, v_ref, lbp_ref, s_ref, o_ref, sn_ref):
    lb = _forget_lower_bound(lbp_ref[...])
    f = lb + (1.0 - lb) * _sigmoid(f_ref[...])
    k = 1.0 - f
    b = jnp.log(f)
    qe = q_ref[...] * jnp.exp(b)
    ke = k * jnp.exp(-b)
    a = jnp.sum(qe * ke, axis=-1, keepdims=True)
    v = v_ref[...]
    decay_t = _transpose_rows_to_cols(jnp.exp(b))
    kd_t = _transpose_rows_to_cols(k * jnp.exp(b - b))
    qe_t = _transpose_rows_to_cols(qe)
    for i in range(q_ref.shape[0]):
        s = s_ref[0, i, 0]
        vi = v[i:i + 1]
        o_ref[i:i + 1, :] = jnp.sum(qe_t[:, i:i + 1] * s, axis=0, keepdims=True) + a[i:i + 1] * vi
        sn_ref[0, i, 0] = decay_t[:, i:i + 1] * s + kd_t[:, i:i + 1] * vi


def _hgrn_step(z_samp, hgrn_lb, state, heads):
    db = z_samp.shape[1]
    col = lambda g: pl.BlockSpec((None, db, HEAD_W), lambda h, g=g: (g, 0, h))
    st_spec = pl.BlockSpec((1, db, 1, HEAD_W, HEAD_W), lambda h: (0, 0, h, 0, 0))
    return pl.pallas_call(
        _hgrn_step_kernel,
        grid=(heads,),
        in_specs=[col(G_HQ), col(G_HF), col(G_HI),
                  pl.BlockSpec((hgrn_lb.shape[0], HEAD_W), lambda h: (0, h)), st_spec],
        out_specs=[pl.BlockSpec((db, HEAD_W), lambda h: (0, h)), st_spec],
        out_shape=[jax.ShapeDtypeStruct((db, heads * HEAD_W), F32),
                   jax.ShapeDtypeStruct(state.shape, F32)],
        compiler_params=pltpu.CompilerParams(dimension_semantics=("arbitrary",),
                                             vmem_limit_bytes=VMEM_LIMIT),
        name="hgrn_step",
    )(z_samp, z_samp, z_samp, hgrn_lb, state)


DEC_PAGES_PER_STEP = 8


def _proj_decode_kernel(pt_ref, x_ref, g_ref, w_ref, rbt_ref, lp_ref, q_ref, kn_ref, vn_ref, *rest,
                        heads, page, n_steps):
    npp = DEC_PAGES_PER_STEP
    k_refs = rest[:npp]
    v_refs = rest[npp:2 * npp]
    z_ref, o_ref, xn_ref, qs_ref, m_ref, l_ref, acc_ref, blast_ref = rest[2 * npp:]

    @pl.when(pl.program_id(1) == 0)
    def _():
        x = x_ref[...]
        ms = jnp.mean(x * x, axis=-1, keepdims=True)
        xn_ref[...] = (x * lax.rsqrt(ms + RMS_EPS) * g_ref[...]).astype(BF16)

    j = pl.program_id(1) % n_steps
    rows = 2 * heads
    lanes = page * heads
    rbt = jnp.concatenate([rbt_ref[...], rbt_ref[...]], axis=0)
    far_bias = rbt[:, N_BUCKETS - 1:N_BUCKETS]
    r_id = lax.broadcasted_iota(jnp.int32, (rows, lanes), 0)
    l_id = lax.broadcasted_iota(jnp.int32, (rows, lanes), 1)
    own_head = l_id % heads == r_id % heads

    def stacked(x):
        lane = lax.broadcasted_iota(jnp.int32, x.shape, 1)
        return jnp.concatenate([jnp.where(lane < AT_DQK, x, 0.0), jnp.where(lane >= AT_DQK, x, 0.0)], axis=0)

    @pl.when(j == 0)
    def _():
        qs_ref[...] = stacked(q_ref[0] * (AT_DQK ** -0.5)).astype(BF16)
        m_ref[...] = jnp.full(m_ref.shape, NEG_INF, F32)
        l_ref[...] = jnp.zeros(l_ref.shape, F32)
        acc_ref[...] = jnp.zeros(acc_ref.shape, F32)
        blast_ref[...] = _bias_by_distance(page - l_id // heads, lambda bkt: rbt[:, bkt:bkt + 1])

    qs = qs_ref[...]
    s_parts = []
    for i in range(npp):
        s_i = _dot_nt(qs, k_refs[i][...].astype(BF16))
        if i == npp - 1:
            s_i = s_i + jnp.where(j == n_steps - 1, blast_ref[...], far_bias)
        else:
            s_i = s_i + far_bias
        s_parts.append(jnp.where(own_head, s_i, NEG_INF))

    m_old = m_ref[...]
    m_blk = s_parts[0]
    for s_i in s_parts[1:]:
        m_blk = jnp.maximum(m_blk, s_i)
    m_new = jnp.maximum(m_old, jnp.max(m_blk, axis=-1, keepdims=True))
    alpha = jnp.exp(m_old - m_new)

    z_ref[...] = jnp.dot(xn_ref[...], w_ref[...], preferred_element_type=F32)

    l_blk = jnp.zeros((rows, lanes), F32)
    pv = jnp.zeros(acc_ref.shape, F32)
    for i in range(npp):
        p_i = jnp.exp(s_parts[i] - m_new)
        l_blk = l_blk + p_i
        pv = pv + jnp.dot(p_i.astype(BF16), v_refs[i][...].astype(BF16), preferred_element_type=F32)
    l_ref[...] = alpha * l_ref[...] + jnp.sum(l_blk, axis=-1, keepdims=True)
    acc_ref[...] = alpha * acc_ref[...] + pv
    m_ref[...] = m_new

    @pl.when(j == n_steps - 1)
    def _():
        kn = jnp.concatenate([kn_ref[0], kn_ref[0]], axis=0)
        vn = jnp.concatenate([vn_ref[0], vn_ref[0]], axis=0)
        s_new = jnp.sum(qs.astype(F32) * kn, axis=-1, keepdims=True) + rbt[:, 0:1]
        m_old = m_ref[...]
        m_fin = jnp.maximum(m_old, s_new)
        alpha = jnp.exp(m_old - m_fin)
        p_new = jnp.exp(s_new - m_fin)
        l_fin = alpha * l_ref[...] + p_new
        o = (alpha * acc_ref[...] + p_new * vn) / l_fin
        o_ref[0] = o[:heads] - _diff_lambda(lp_ref[...]) * o[heads:]


PD_TM = 1024
PD_TN = 256


def _proj_decode(x, g, w_bf, q_s, k_s, v_s, cache_k, cache_v, page_table, rel_bias, diff_lambda):
    rows, d = x.shape
    n = w_bf.shape[1]
    db, n_pages = page_table.shape
    n_phys, page, heads, _ = cache_k.shape
    npp = DEC_PAGES_PER_STEP
    n_steps = n_pages // npp
    grid = (rows // PD_TM, n // PD_TN)
    seq_per_row_tile = grid[1] // n_steps
    tiles_per_group = GROUP_W // PD_TN
    assert CONST_BIAS_DIST <= page + 1 and n_pages % npp == 0 and GROUP_W % PD_TN == 0
    assert rows % PD_TM == 0 and n % PD_TN == 0 and grid[1] % n_steps == 0 and grid[0] * seq_per_row_tile == db
    ck = cache_k.reshape(n_phys, page * heads, HEAD_W)
    cv = cache_v.reshape(n_phys, page * heads, HEAD_W)
    seq_of = lambda i, t: i * seq_per_row_tile + t // n_steps
    vec = pl.BlockSpec((1, heads, HEAD_W), lambda i, t, pt: (seq_of(i, t), 0, 0))
    page_spec = lambda p: pl.BlockSpec(
        (None, page * heads, HEAD_W), lambda i, t, pt, p=p: (pt[seq_of(i, t), (t % n_steps) * npp + p], 0, 0))
    srows = 2 * heads
    kernel = functools.partial(_proj_decode_kernel, heads=heads, page=page, n_steps=n_steps)
    return pl.pallas_call(
        kernel,
        grid_spec=pltpu.PrefetchScalarGridSpec(
            num_scalar_prefetch=1,
            grid=grid,
            in_specs=[pl.BlockSpec((PD_TM, d), lambda i, t, pt: (i, 0)),
                      pl.BlockSpec((1, d), lambda i, t, pt: (0, 0)),
                      pl.BlockSpec((d, PD_TN), lambda i, t, pt: (0, t)),
                      pl.BlockSpec((heads, N_BUCKETS), lambda i, t, pt: (0, 0)),
                      pl.BlockSpec(diff_lambda.shape, lambda i, t, pt: (0, 0)),
                      vec, vec, vec]
                     + [page_spec(p) for p in range(npp)] + [page_spec(p) for p in range(npp)],
            out_specs=[pl.BlockSpec((None, PD_TM, PD_TN),
                                    lambda i, t, pt: (t // tiles_per_group, i, t % tiles_per_group)),
                       pl.BlockSpec((1, heads, HEAD_W), lambda i, t, pt: (seq_of(i, t), 0, 0))],
            scratch_shapes=[pltpu.VMEM((PD_TM, d), BF16),
                            pltpu.VMEM((srows, HEAD_W), BF16), pltpu.VMEM((srows, 1), F32),
                            pltpu.VMEM((srows, 1), F32), pltpu.VMEM((srows, HEAD_W), F32),
                            pltpu.VMEM((srows, page * heads), F32)]),
        out_shape=[jax.ShapeDtypeStruct((n // GROUP_W, rows, GROUP_W), F32),
                   jax.ShapeDtypeStruct((db, heads, HEAD_W), F32)],
        compiler_params=pltpu.CompilerParams(dimension_semantics=("arbitrary", "arbitrary"),
                                             vmem_limit_bytes=VMEM_LIMIT),
        name="proj_decode",
    )(page_table, x, g, w_bf, rel_bias.T, diff_lambda, q_s, k_s, v_s, *([ck] * npp), *([cv] * npp))


KV_ROWS = 256


def _assemble_kv_kernel(k_ref, v_ref, km_ref, vm_ref, ko_ref, vo_ref):
    first = pl.program_id(1) == 0
    for src_ref, meta_ref, out_ref in ((k_ref, km_ref, ko_ref), (v_ref, vm_ref, vo_ref)):
        x = src_ref[0]
        shifted = jnp.concatenate([meta_ref[...], x[:KV_ROWS - N_META]], axis=0)
        out_ref[...] = jnp.where(first, shifted, x).reshape(out_ref.shape)


def _assemble_kv(z_p, z_small, batch, seq, heads):
    assert seq % KV_ROWS == 0 and N_META % 8 == 0
    zp4 = z_p.reshape(z_p.shape[0], batch, seq, GROUP_W)
    src = lambda g: pl.BlockSpec((None, 1, KV_ROWS, GROUP_W),
                                 lambda b, i, g=g: (g, b, jnp.maximum(i - 1, 0), 0))
    meta = lambda g: pl.BlockSpec((None, N_META, GROUP_W), lambda b, i, g=g: (g, 0, 0))
    out = pl.BlockSpec((pl.Element(1), pl.Element(KV_ROWS), pl.Element(heads), pl.Element(HEAD_W)),
                       lambda b, i: (b, jnp.where(i == 0, 0, N_META + (i - 1) * KV_ROWS), 0, 0))
    shape = jax.ShapeDtypeStruct((batch, N_META + seq, heads, HEAD_W), F32)
    return pl.pallas_call(
        _assemble_kv_kernel,
        grid=(batch, seq // KV_ROWS + 1),
        in_specs=[src(G_AK), src(G_AV), meta(G_AK), meta(G_AV)],
        out_specs=[out, out],
        out_shape=[shape, shape],
        compiler_params=pltpu.CompilerParams(dimension_semantics=("arbitrary", "arbitrary"),
                                             vmem_limit_bytes=VMEM_LIMIT),
        name="assemble_kv",
    )(zp4, zp4, z_small, z_small)


def kernel(x_prompt, x_sample, cache_k, cache_v, state_hgrn, page_table, meta_tokens, rel_bias, hgrn_lb,
           norm_g, w_in, hgrn_norm_g, diff_norm_g, diff_lambda, w_out, final_norm_g):
    batch, seq, d = x_prompt.shape
    db = x_sample.shape[0]
    heads = cache_k.shape[3]
    gw = heads * HEAD_W
    assert norm_g.shape[0] == 1 and x_sample.shape[1] == 1, "one layer, one sample token per sequence"
    assert gw == GROUP_W

    w_out_bf = w_out[0].astype(BF16)
    g_in = norm_g[0].reshape(1, d)
    g_final = final_norm_g.reshape(1, d)
    g_hg = hgrn_norm_g[0].reshape(1, HEAD_W)
    g_at = diff_norm_g[0].reshape(1, HEAD_W)
    lam_p = diff_lambda[0]

    xp = x_prompt.reshape(batch * seq, d)
    xs = x_sample.reshape(db, d)
    small = jnp.concatenate([meta_tokens.astype(F32), xs], axis=0)

    z_small, w_in_bf = _proj_small(small, g_in, w_in[0])
    z_samp = z_small[:, N_META:]
    samp_heads = lambda g: z_samp[g].reshape(db, heads, HEAD_W)
    q_s, k_s, v_s = samp_heads(G_AQ), samp_heads(G_AK), samp_heads(G_AV)

    z_p, o_at_s = _proj_decode(xp, g_in, w_in_bf, q_s, k_s, v_s, cache_k[0], cache_v[0], page_table,
                               rel_bias, lam_p)

    o_hg_p, s_p = _hgrn_prompt(z_p, z_small, hgrn_lb, batch, seq, heads)
    o_at_p = _attn_prompt(z_p, z_small, rel_bias, lam_p, batch, seq, heads)
    y_p = _merge(o_hg_p.reshape(batch * seq, gw), o_at_p.reshape(batch * seq, gw), z_p, xp,
                 g_hg, g_at, w_out_bf, g_final, tm=512, heads=heads)

    o_hg_s, s_s = _hgrn_step(z_samp, hgrn_lb, state_hgrn, heads)
    y_s = _merge(o_hg_s, o_at_s.reshape(db, gw), z_samp, xs, g_hg, g_at, w_out_bf, g_final, tm=db, heads=heads)

    k_p, v_p = _assemble_kv(z_p, z_small, batch, seq, heads)

    return (y_p.reshape(batch, seq, d), y_s.reshape(db, 1, d), k_p[None], v_p[None],
            s_p[None], k_s.reshape(1, db, 1, heads, HEAD_W), v_s.reshape(1, db, 1, heads, HEAD_W), s_s)
```

```python
import functools
import math

import numpy as np
import jax
import jax.numpy as jnp
from jax import lax
from jax.experimental import pallas as pl
from jax.experimental.pallas import tpu as pltpu

F32 = jnp.float32
BF16 = jnp.bfloat16

N_META = 16
HEAD_W = 128
AT_DQK = 64
N_BUCKETS = 32
MAX_DISTANCE = 128
MAX_EXACT = N_BUCKETS // 2
HGRN_CHUNK = 64
RMS_EPS = 1e-6
NEG_INF = -1e30
LAMBDA_INIT = 0.8 - 0.6 * math.exp(-0.3 * 0)
G_HQ, G_HF, G_HI, G_HG, G_AQ, G_AK, G_AV, G_AG = range(8)

VMEM_LIMIT = 56 * 1024 * 1024


def _bucket_thresholds():
    n = np.arange(0, 2 * MAX_DISTANCE)
    nl = np.maximum(n, MAX_EXACT).astype(np.float32)
    large = MAX_EXACT + (np.log(nl / np.float32(MAX_EXACT)) / np.float32(math.log(MAX_DISTANCE / MAX_EXACT))
                         * np.float32(N_BUCKETS - MAX_EXACT)).astype(np.int32)
    bucket = np.where(n < MAX_EXACT, n, np.minimum(large, N_BUCKETS - 1))
    return [int(np.argmax(bucket >= b)) for b in range(N_BUCKETS)]


BUCKET_THR = _bucket_thresholds()
CONST_BIAS_DIST = BUCKET_THR[N_BUCKETS - 1]


def _sigmoid(x):
    return 1.0 / (1.0 + jnp.exp(-x))


def _silu(x):
    return x * _sigmoid(x)


def _dot_nt(a, b):
    return lax.dot_general(a, b, (((1,), (1,)), ((), ())), preferred_element_type=F32)


def _dot_tn(a, b):
    return lax.dot_general(a, b, (((0,), (0,)), ((), ())), preferred_element_type=F32)


GROUP_W = 1024


def _proj_small_kernel(x_ref, g_ref, w_ref, z_ref, wb_ref, xn_ref):
    @pl.when(pl.program_id(0) == 0)
    def _():
        x = x_ref[...]
        ms = jnp.mean(x * x, axis=-1, keepdims=True)
        xn_ref[...] = (x * lax.rsqrt(ms + RMS_EPS) * g_ref[...]).astype(BF16)

    w_bf = w_ref[...].astype(BF16)
    wb_ref[...] = w_bf
    z_ref[...] = jnp.dot(xn_ref[...], w_bf, preferred_element_type=F32)


def _proj_small(x, g, w):
    rows, d = x.shape
    n = w.shape[1]
    return pl.pallas_call(
        _proj_small_kernel,
        grid=(n // GROUP_W,),
        in_specs=[pl.BlockSpec((rows, d), lambda j: (0, 0)),
                  pl.BlockSpec((1, d), lambda j: (0, 0)),
                  pl.BlockSpec((d, GROUP_W), lambda j: (0, j))],
        out_specs=[pl.BlockSpec((None, rows, GROUP_W), lambda j: (j, 0, 0)),
                   pl.BlockSpec((d, GROUP_W), lambda j: (0, j))],
        out_shape=[jax.ShapeDtypeStruct((n // GROUP_W, rows, GROUP_W), F32),
                   jax.ShapeDtypeStruct((d, n), BF16)],
        scratch_shapes=[pltpu.VMEM((rows, d), BF16)],
        compiler_params=pltpu.CompilerParams(dimension_semantics=("arbitrary",),
                                             vmem_limit_bytes=VMEM_LIMIT),
        name="proj_small",
    )(x, g, w)


def _forget_lower_bound(lbp):
    e = jnp.exp(lbp - jnp.max(lbp, axis=0, keepdims=True))
    return e[0:1] / jnp.sum(e, axis=0, keepdims=True)


def _cumsum_rows(g, tri_bf):
    g1 = g.astype(BF16)
    r1 = g - g1.astype(F32)
    g2 = r1.astype(BF16)
    g3 = (r1 - g2.astype(F32)).astype(BF16)
    dot = functools.partial(jnp.dot, preferred_element_type=F32)
    return dot(tri_bf, g1) + dot(tri_bf, g2) + dot(tri_bf, g3)


def _hgrn_chunk_local(q, hf, v, lb):
    c = q.shape[0]
    row = lax.broadcasted_iota(jnp.int32, (c, c), 0)
    col = lax.broadcasted_iota(jnp.int32, (c, c), 1)
    causal = row >= col
    f = lb + (1.0 - lb) * _sigmoid(hf)
    k = 1.0 - f
    b = _cumsum_rows(jnp.log(f), causal.astype(BF16))
    qe = (q * jnp.exp(b)).astype(BF16)
    ke = (k * jnp.exp(-b)).astype(BF16)
    a = jnp.where(causal, _dot_nt(qe, ke), 0.0)
    v_bf = v.astype(BF16)
    intra = jnp.dot(a.astype(BF16), v_bf, preferred_element_type=F32)
    b_last = b[c - 1:c]
    kd = (k * jnp.exp(b_last - b)).astype(BF16)
    return qe, intra, _dot_tn(v_bf, kd), jnp.exp(b_last)


HGRN_SLAB = 256
HGRN_PAIR = 128


def _hgrn_prompt_kernel(q_ref, f_ref, v_ref, qm_ref, fm_ref, vm_ref, lbp_ref, o_ref, s_ref,
                        b_ref, qe_ref, ke_ref, kd_ref, vb_ref, vt_ref, a_ref, ut_ref, dec_ref, st_ref):
    ck, slab, pair = HGRN_CHUNK, HGRN_SLAB, HGRN_PAIR
    seq = q_ref.shape[1]
    n_chunks = seq // ck
    dot = functools.partial(jnp.dot, preferred_element_type=F32)
    lb = _forget_lower_bound(lbp_ref[...])

    _, _, st, _ = _hgrn_chunk_local(qm_ref[...], fm_ref[...], vm_ref[...], lb)

    f = lb + (1.0 - lb) * _sigmoid(f_ref[0])
    g = jnp.log(f)
    g1 = g.astype(BF16)
    r1 = g - g1.astype(F32)
    g2 = r1.astype(BF16)
    g3 = (r1 - g2.astype(F32)).astype(BF16)
    gcat = jnp.concatenate([g1, g2, g3], axis=1)
    r = lax.broadcasted_iota(jnp.int32, (slab, slab), 0)
    c = lax.broadcasted_iota(jnp.int32, (slab, slab), 1)
    tri_bd = ((r // ck == c // ck) & (r >= c)).astype(BF16)
    for si in range(seq // slab):
        rows = slice(si * slab, (si + 1) * slab)
        t = dot(tri_bd, gcat[rows])
        b_ref[rows, :] = t[:, :HEAD_W] + t[:, HEAD_W:2 * HEAD_W] + t[:, 2 * HEAD_W:]

    k = 1.0 - f
    b = b_ref[...]
    qe_ref[...] = (q_ref[0] * jnp.exp(b)).astype(BF16)
    ke_ref[...] = (k * jnp.exp(-b)).astype(BF16)
    v = v_ref[0]
    vb_ref[...] = v.astype(BF16)
    vt_ref[...] = v.T.astype(BF16)
    for ci in range(n_chunks):
        rows = slice(ci * ck, (ci + 1) * ck)
        b_last = b_ref[(ci + 1) * ck - 1:(ci + 1) * ck, :]
        dec_ref[ci] = jnp.exp(b_last)
        kd_ref[rows, :] = (k[rows] * jnp.exp(b_last - b_ref[rows, :])).astype(BF16)

    rp = lax.broadcasted_iota(jnp.int32, (pair, pair), 0)
    cp = lax.broadcasted_iota(jnp.int32, (pair, pair), 1)
    keep = (rp // ck == cp // ck) & (rp >= cp)
    for pi in range(seq // pair):
        rows = slice(pi * pair, (pi + 1) * pair)
        a_ref[rows, :] = jnp.where(keep, _dot_nt(qe_ref[rows, :], ke_ref[rows, :]), 0.0).astype(BF16)

    for pi in range(seq // pair):
        rows = slice(pi * pair, (pi + 1) * pair)
        o_ref[0, rows, :] = dot(a_ref[rows, :], vb_ref[rows, :])
    for ci in range(n_chunks):
        rows = slice(ci * ck, (ci + 1) * ck)
        ut_ref[ci] = dot(vt_ref[:, rows], kd_ref[rows, :])

    for ci in range(n_chunks):
        st_ref[ci] = st.astype(BF16)
        st = st * dec_ref[ci] + ut_ref[ci]
    s_ref[0, 0] = st.T

    for ci in range(n_chunks):
        rows = slice(ci * ck, (ci + 1) * ck)
        o_ref[0, rows, :] += _dot_nt(qe_ref[rows, :], st_ref[ci])


def _hgrn_prompt(z_p, z_small, hgrn_lb, batch, seq, heads):
    zp3 = z_p.reshape(z_p.shape[0], batch, seq, GROUP_W)
    big = lambda g: pl.BlockSpec((None, 1, seq, HEAD_W), lambda b, h, g=g: (g, b, 0, h))
    meta = lambda g: pl.BlockSpec((None, N_META, HEAD_W), lambda b, h, g=g: (g, 0, h))
    return pl.pallas_call(
        _hgrn_prompt_kernel,
        grid=(batch, heads),
        in_specs=[big(G_HQ), big(G_HF), big(G_HI), meta(G_HQ), meta(G_HF), meta(G_HI),
                  pl.BlockSpec((hgrn_lb.shape[0], HEAD_W), lambda b, h: (0, h))],
        out_specs=[pl.BlockSpec((1, seq, HEAD_W), lambda b, h: (b, 0, h)),
                   pl.BlockSpec((1, 1, HEAD_W, HEAD_W), lambda b, h: (b, h, 0, 0))],
        out_shape=[jax.ShapeDtypeStruct((batch, seq, heads * HEAD_W), F32),
                   jax.ShapeDtypeStruct((batch, heads, HEAD_W, HEAD_W), F32)],
        scratch_shapes=[pltpu.VMEM((seq, HEAD_W), F32),
                        pltpu.VMEM((seq, HEAD_W), BF16), pltpu.VMEM((seq, HEAD_W), BF16),
                        pltpu.VMEM((seq, HEAD_W), BF16), pltpu.VMEM((seq, HEAD_W), BF16),
                        pltpu.VMEM((HEAD_W, seq), BF16), pltpu.VMEM((seq, HGRN_PAIR), BF16),
                        pltpu.VMEM((seq // HGRN_CHUNK, HEAD_W, HEAD_W), F32),
                        pltpu.VMEM((seq // HGRN_CHUNK, 1, HEAD_W), F32),
                        pltpu.VMEM((seq // HGRN_CHUNK, HEAD_W, HEAD_W), BF16)],
        compiler_params=pltpu.CompilerParams(dimension_semantics=("arbitrary", "arbitrary"),
                                             vmem_limit_bytes=VMEM_LIMIT),
        name="hgrn_prompt",
    )(zp3, zp3, zp3, z_small, z_small, z_small, hgrn_lb)


def _diff_lambda(lp):
    s01 = jnp.sum(lp[0:1] * lp[1:2], axis=-1, keepdims=True)
    s23 = jnp.sum(lp[2:3] * lp[3:4], axis=-1, keepdims=True)
    return jnp.exp(s01) - jnp.exp(s23) + LAMBDA_INIT


def _bias_by_distance(dist, bias_of_bucket):
    out = jnp.where(dist >= BUCKET_THR[1], bias_of_bucket(1), bias_of_bucket(0))
    for bkt in range(2, N_BUCKETS):
        out = jnp.where(dist >= BUCKET_THR[bkt], bias_of_bucket(bkt), out)
    return out


ATT_TQ = 256


def _colmax(s):
    return jnp.max(s.reshape(s.shape[0] // 8, 8, s.shape[1]), axis=0)


def _colsum(s):
    return jnp.sum(s.reshape(s.shape[0] // 8, 8, s.shape[1]), axis=0)


def _attn_prompt_kernel(rb_ref, lp_ref, q_ref, k_ref, v_ref, km_ref, vm_ref, o_ref,
                        kb_ref, vt_ref, vmt_ref, s_ref, bdiag_ref, bnear_ref, bmeta_ref):
    h = pl.program_id(0)
    b = pl.program_id(1)
    tq = ATT_TQ
    n_q = q_ref.shape[1] // tq
    log2e = 1.0 / math.log(2.0)
    bias_of_bucket = lambda bkt: rb_ref[bkt * rb_ref.shape[0] // N_BUCKETS + h] * log2e
    far_bias = bias_of_bucket(N_BUCKETS - 1)
    key = lax.broadcasted_iota(jnp.int32, (tq, tq), 0)
    qry = lax.broadcasted_iota(jnp.int32, (tq, tq), 1)

    @pl.when(b == 0)
    def _():
        bdiag_ref[...] = _bias_by_distance(qry - key, bias_of_bucket)
        bnear_ref[...] = _bias_by_distance(qry - key + tq, bias_of_bucket)
        keym = lax.broadcasted_iota(jnp.int32, (N_META, tq), 0)
        qrym = lax.broadcasted_iota(jnp.int32, (N_META, tq), 1)
        bmeta_ref[...] = _bias_by_distance(qrym + N_META - keym, bias_of_bucket)

    kb_ref[...] = k_ref[0].astype(BF16)
    vt_ref[...] = v_ref[0].T.astype(BF16)
    pad = jnp.zeros((HEAD_W - N_META, HEAD_W), F32)
    vmt_ref[...] = jnp.concatenate([vm_ref[...], pad], axis=0).T.astype(BF16)
    km_bf = km_ref[...].astype(BF16)
    lam = _diff_lambda(lp_ref[...])
    causal = qry >= key
    dot = functools.partial(jnp.dot, preferred_element_type=F32)

    def pass_a(qi):
        par = qi % 2
        q = q_ref[0, qi * tq:(qi + 1) * tq, :] * (AT_DQK ** -0.5 * log2e)
        lane = lax.broadcasted_iota(jnp.int32, q.shape, 1)
        qc = (jnp.where(lane < AT_DQK, q, 0.0).astype(BF16), jnp.where(lane >= AT_DQK, q, 0.0).astype(BF16))
        out = []
        for c in range(2):
            sm = _dot_nt(km_bf, qc[c]) + (bmeta_ref[...] if qi == 0 else far_bias)
            m8 = _colmax(sm)
            for j in range(qi + 1):
                s = _dot_nt(kb_ref[j * tq:(j + 1) * tq, :], qc[c])
                if j == qi:
                    s = jnp.where(causal, s + bdiag_ref[...], NEG_INF)
                elif j == qi - 1:
                    s = s + bnear_ref[...]
                s_ref[par, c, j * tq:(j + 1) * tq, :] = s
                m8 = jnp.maximum(m8, _colmax(s) + far_bias if j < qi - 1 else _colmax(s))
            out.append((sm, jnp.max(m8, axis=0, keepdims=True)))
        return out

    def pass_b(qi, stats):
        par = qi % 2
        o_c = []
        for c in range(2):
            sm, m = stats[c]
            m_far = m - far_bias
            pm = jnp.exp2(sm - m)
            l8 = _colsum(pm)
            acc = dot(vmt_ref[:, :N_META], pm.astype(BF16))
            for j in range(qi + 1):
                p = jnp.exp2(s_ref[par, c, j * tq:(j + 1) * tq, :] - (m_far if j < qi - 1 else m))
                l8 = l8 + _colsum(p)
                acc = acc + dot(vt_ref[:, j * tq:(j + 1) * tq], p.astype(BF16))
            o_c.append(acc / jnp.sum(l8, axis=0, keepdims=True))
        o_ref[0, qi * tq:(qi + 1) * tq, :] = (o_c[0] - lam * o_c[1]).T

    stats = pass_a(0)
    for qi in range(n_q):
        nxt = pass_a(qi + 1) if qi + 1 < n_q else None
        pass_b(qi, stats)
        stats = nxt


def _attn_prompt(z_p, z_small, rel_bias, diff_lambda, batch, seq, heads):
    assert CONST_BIAS_DIST <= ATT_TQ + 1 and CONST_BIAS_DIST <= N_META + 1 + ATT_TQ - N_META
    zp3 = z_p.reshape(z_p.shape[0], batch, seq, GROUP_W)
    tq = ATT_TQ
    full = lambda g: pl.BlockSpec((None, 1, seq, HEAD_W), lambda h, b, g=g: (g, b, 0, h))
    meta = lambda g: pl.BlockSpec((None, N_META, HEAD_W), lambda h, b, g=g: (g, 0, h))
    return pl.pallas_call(
        _attn_prompt_kernel,
        grid=(heads, batch),
        in_specs=[pl.BlockSpec(memory_space=pltpu.SMEM),
                  pl.BlockSpec(diff_lambda.shape, lambda h, b: (0, 0)),
                  full(G_AQ), full(G_AK), full(G_AV), meta(G_AK), meta(G_AV)],
        out_specs=pl.BlockSpec((1, seq, HEAD_W), lambda h, b: (b, 0, h)),
        out_shape=jax.ShapeDtypeStruct((batch, seq, heads * HEAD_W), F32),
        scratch_shapes=[pltpu.VMEM((seq, HEAD_W), BF16), pltpu.VMEM((HEAD_W, seq), BF16),
                        pltpu.VMEM((HEAD_W, HEAD_W), BF16),
                        pltpu.VMEM((2, 2, seq, tq), F32),
                        pltpu.VMEM((tq, tq), F32), pltpu.VMEM((tq, tq), F32),
                        pltpu.VMEM((N_META, tq), F32)],
        compiler_params=pltpu.CompilerParams(dimension_semantics=("arbitrary", "arbitrary"),
                                             vmem_limit_bytes=VMEM_LIMIT),
        name="attn_prompt",
    )(rel_bias.reshape(-1), diff_lambda, zp3, zp3, zp3, z_small, z_small)


def _head_rmsnorm(o, g):
    parts = []
    for hh in range(o.shape[1] // HEAD_W):
        oh = o[:, hh * HEAD_W:(hh + 1) * HEAD_W]
        ms = jnp.mean(oh * oh, axis=-1, keepdims=True)
        parts.append(oh * lax.rsqrt(ms + RMS_EPS) * g)
    return jnp.concatenate(parts, axis=1)


MERGE_SUB = 256


def _merge_kernel(ohg_ref, hg_ref, oat_ref, ag_ref, x_ref, gh_ref, ga_ref, w_ref, gf_ref, y_ref):
    tm = x_ref.shape[0]
    sub = min(MERGE_SUB, tm)
    subs = [slice(s, s + sub) for s in range(0, tm, sub)]
    mixed = []
    for r in subs:
        mix_h = _head_rmsnorm(ohg_ref[r, :], gh_ref[...]) * _silu(hg_ref[r, :])
        mix_a = _head_rmsnorm(oat_ref[r, :], ga_ref[...]) * (1.0 - LAMBDA_INIT) * _silu(ag_ref[r, :])
        mixed.append(jnp.concatenate([mix_h, mix_a], axis=1).astype(BF16))
    proj = [jnp.dot(m, w_ref[...], preferred_element_type=F32) for m in mixed]
    for r, p in zip(subs, proj):
        hres = x_ref[r, :] + p
        ms = jnp.mean(hres * hres, axis=-1, keepdims=True)
        y_ref[r, :] = hres * lax.rsqrt(ms + RMS_EPS) * gf_ref[...]


def _merge(o_hg, o_at, z, x, g_hg, g_at, w_out_bf, g_final, tm, heads):
    rows, d = x.shape
    gw = heads * HEAD_W
    row_blk = lambda w: pl.BlockSpec((tm, w), lambda i: (i, 0))
    gate = lambda g: pl.BlockSpec((None, tm, gw), lambda i, g=g: (g, i, 0))
    const = lambda shape: pl.BlockSpec(shape, lambda i: (0, 0))
    return pl.pallas_call(
        _merge_kernel,
        grid=(rows // tm,),
        in_specs=[row_blk(gw), gate(G_HG), row_blk(gw), gate(G_AG), row_blk(d),
                  const((1, HEAD_W)), const((1, HEAD_W)),
                  pl.BlockSpec(w_out_bf.shape, lambda i: (0, 0), pipeline_mode=pl.Buffered(1)),
                  const((1, d))],
        out_specs=row_blk(d),
        out_shape=jax.ShapeDtypeStruct((rows, d), F32),
        compiler_params=pltpu.CompilerParams(dimension_semantics=("arbitrary",),
                                             vmem_limit_bytes=VMEM_LIMIT),
        name="merge",
    )(o_hg, z, o_at, z, x, g_hg, g_at, w_out_bf, g_final)


def _transpose_rows_to_cols(x):
    pad = jnp.zeros((HEAD_W - x.shape[0], HEAD_W), F32)
    return jnp.concatenate([x, pad], axis=0).T


def _hgrn_step_kernel(q_ref, f_ref, v_ref, lbp_ref, s_ref, o_ref, sn_ref):
    lb = _forget_lower_bound(lbp_ref[...])
    f = lb + (1.0 - lb) * _sigmoid(f_ref[...])
    k = 1.0 - f
    b = jnp.log(f)
    qe = q_ref[...] * jnp.exp(b)
    ke = k * jnp.exp(-b)
    a = jnp.sum(qe * ke, axis=-1, keepdims=True)
    v = v_ref[...]
    decay_t = _transpose_rows_to_cols(jnp.exp(b))
    kd_t = _transpose_rows_to_cols(k * jnp.exp(b - b))
    qe_t = _transpose_rows_to_cols(qe)
    for i in range(q_ref.shape[0]):
        s = s_ref[0, i, 0]
        vi = v[i:i + 1]
        o_ref[i:i + 1, :] = jnp.sum(qe_t[:, i:i + 1] * s, axis=0, keepdims=True) + a[i:i + 1] * vi
        sn_ref[0, i, 0] = decay_t[:, i:i + 1] * s + kd_t[:, i:i + 1] * vi


def _hgrn_step(z_samp, hgrn_lb, state, heads):
    db = z_samp.shape[1]
    col = lambda g: pl.BlockSpec((None, db, HEAD_W), lambda h, g=g: (g, 0, h))
    st_spec = pl.BlockSpec((1, db, 1, HEAD_W, HEAD_W), lambda h: (0, 0, h, 0, 0))
    return pl.pallas_call(
        _hgrn_step_kernel,
        grid=(heads,),
        in_specs=[col(G_HQ), col(G_HF), col(G_HI),
                  pl.BlockSpec((hgrn_lb.shape[0], HEAD_W), lambda h: (0, h)), st_spec],
        out_specs=[pl.BlockSpec((db, HEAD_W), lambda h: (0, h)), st_spec],
        out_shape=[jax.ShapeDtypeStruct((db, heads * HEAD_W), F32),
                   jax.ShapeDtypeStruct(state.shape, F32)],
        compiler_params=pltpu.CompilerParams(dimension_semantics=("arbitrary",),
                                             vmem_limit_bytes=VMEM_LIMIT),
        name="hgrn_step",
    )(z_samp, z_samp, z_samp, hgrn_lb, state)


DEC_PAGES_PER_STEP = 8
DEC_RING = 3


def _page_copies(pt_ref, ck_ref, cv_ref, kbuf_ref, vbuf_ref, sem_ref, step, n_steps):
    npp = DEC_PAGES_PER_STEP
    slot = step % DEC_RING
    seq = step // n_steps
    first_page = (step % n_steps) * npp
    copies = []
    for p in range(npp):
        phys = pt_ref[seq, first_page + p]
        copies.append(pltpu.make_async_copy(ck_ref.at[phys], kbuf_ref.at[slot, p], sem_ref.at[slot]))
        copies.append(pltpu.make_async_copy(cv_ref.at[phys], vbuf_ref.at[slot, p], sem_ref.at[slot]))
    return copies


def _proj_decode_kernel(pt_ref, x_ref, g_ref, w_ref, rbt_ref, lp_ref, q_ref, kn_ref, vn_ref, ck_ref, cv_ref,
                        z_ref, o_ref, xn_ref, qs_ref, m_ref, l_ref, acc_ref, blast_ref, kbuf_ref, vbuf_ref, sem_ref,
                        *, heads, page, n_steps):
    npp = DEC_PAGES_PER_STEP
    step = pl.program_id(0) * pl.num_programs(1) + pl.program_id(1)
    total_steps = pl.num_programs(0) * pl.num_programs(1)
    copies_for = lambda s: _page_copies(pt_ref, ck_ref, cv_ref, kbuf_ref, vbuf_ref, sem_ref, s, n_steps)

    @pl.when(step == 0)
    def _():
        for s in range(DEC_RING - 1):
            for cp in copies_for(s):
                cp.start()

    @pl.when(step + (DEC_RING - 1) < total_steps)
    def _():
        for cp in copies_for(step + (DEC_RING - 1)):
            cp.start()

    @pl.when(pl.program_id(1) == 0)
    def _():
        x = x_ref[...]
        ms = jnp.mean(x * x, axis=-1, keepdims=True)
        xn_ref[...] = (x * lax.rsqrt(ms + RMS_EPS) * g_ref[...]).astype(BF16)

    for cp in copies_for(step):
        cp.wait()
    slot = step % DEC_RING
    k_refs = [kbuf_ref.at[slot, p] for p in range(npp)]
    v_refs = [vbuf_ref.at[slot, p] for p in range(npp)]

    j = pl.program_id(1) % n_steps
    rows = 2 * heads
    lanes = page * heads
    rbt = jnp.concatenate([rbt_ref[...], rbt_ref[...]], axis=0)
    far_bias = rbt[:, N_BUCKETS - 1:N_BUCKETS]
    r_id = lax.broadcasted_iota(jnp.int32, (rows, lanes), 0)
    l_id = lax.broadcasted_iota(jnp.int32, (rows, lanes), 1)
    own_head = l_id % heads == r_id % heads

    def stacked(x):
        lane = lax.broadcasted_iota(jnp.int32, x.shape, 1)
        return jnp.concatenate([jnp.where(lane < AT_DQK, x, 0.0), jnp.where(lane >= AT_DQK, x, 0.0)], axis=0)

    @pl.when(j == 0)
    def _():
        qs_ref[...] = stacked(q_ref[0] * (AT_DQK ** -0.5)).astype(BF16)
        m_ref[...] = jnp.full(m_ref.shape, NEG_INF, F32)
        l_ref[...] = jnp.zeros(l_ref.shape, F32)
        acc_ref[...] = jnp.zeros(acc_ref.shape, F32)
        blast_ref[...] = _bias_by_distance(page - l_id // heads, lambda bkt: rbt[:, bkt:bkt + 1])

    qs = qs_ref[...]
    s_parts = []
    for i in range(npp):
        s_i = _dot_nt(qs, k_refs[i][...].astype(BF16))
        if i == npp - 1:
            s_i = s_i + jnp.where(j == n_steps - 1, blast_ref[...], far_bias)
        else:
            s_i = s_i + far_bias
        s_parts.append(jnp.where(own_head, s_i, NEG_INF))

    m_old = m_ref[...]
    m_blk = s_parts[0]
    for s_i in s_parts[1:]:
        m_blk = jnp.maximum(m_blk, s_i)
    m_new = jnp.maximum(m_old, jnp.max(m_blk, axis=-1, keepdims=True))
    alpha = jnp.exp(m_old - m_new)

    z_ref[...] = jnp.dot(xn_ref[...], w_ref[...], preferred_element_type=F32)

    l_blk = jnp.zeros((rows, lanes), F32)
    pv = jnp.zeros(acc_ref.shape, F32)
    for i in range(npp):
        p_i = jnp.exp(s_parts[i] - m_new)
        l_blk = l_blk + p_i
        pv = pv + jnp.dot(p_i.astype(BF16), v_refs[i][...].astype(BF16), preferred_element_type=F32)
    l_ref[...] = alpha * l_ref[...] + jnp.sum(l_blk, axis=-1, keepdims=True)
    acc_ref[...] = alpha * acc_ref[...] + pv
    m_ref[...] = m_new

    @pl.when(j == n_steps - 1)
    def _():
        kn = jnp.concatenate([kn_ref[0], kn_ref[0]], axis=0)
        vn = jnp.concatenate([vn_ref[0], vn_ref[0]], axis=0)
        s_new = jnp.sum(qs.astype(F32) * kn, axis=-1, keepdims=True) + rbt[:, 0:1]
        m_old = m_ref[...]
        m_fin = jnp.maximum(m_old, s_new)
        alpha = jnp.exp(m_old - m_fin)
        p_new = jnp.exp(s_new - m_fin)
        l_fin = alpha * l_ref[...] + p_new
        o = (alpha * acc_ref[...] + p_new * vn) / l_fin
        o_ref[0] = o[:heads] - _diff_lambda(lp_ref[...]) * o[heads:]


PD_TM = 1024
PD_TN = 256


def _proj_decode(x, g, w_bf, q_s, k_s, v_s, cache_k, cache_v, page_table, rel_bias, diff_lambda):
    rows, d = x.shape
    n = w_bf.shape[1]
    db, n_pages = page_table.shape
    n_phys, page, heads, _ = cache_k.shape
    npp = DEC_PAGES_PER_STEP
    n_steps = n_pages // npp
    grid = (rows // PD_TM, n // PD_TN)
    seq_per_row_tile = grid[1] // n_steps
    tiles_per_group = GROUP_W // PD_TN
    assert CONST_BIAS_DIST <= page + 1 and n_pages % npp == 0 and GROUP_W % PD_TN == 0
    assert rows % PD_TM == 0 and n % PD_TN == 0 and grid[1] % n_steps == 0 and grid[0] * seq_per_row_tile == db
    ck = cache_k.reshape(n_phys, page * heads, HEAD_W)
    cv = cache_v.reshape(n_phys, page * heads, HEAD_W)
    seq_of = lambda i, t: i * seq_per_row_tile + t // n_steps
    vec = pl.BlockSpec((1, heads, HEAD_W), lambda i, t, pt: (seq_of(i, t), 0, 0))
    hbm = pl.BlockSpec(memory_space=pl.ANY)
    srows = 2 * heads
    kernel = functools.partial(_proj_decode_kernel, heads=heads, page=page, n_steps=n_steps)
    return pl.pallas_call(
        kernel,
        grid_spec=pltpu.PrefetchScalarGridSpec(
            num_scalar_prefetch=1,
            grid=grid,
            in_specs=[pl.BlockSpec((PD_TM, d), lambda i, t, pt: (i, 0)),
                      pl.BlockSpec((1, d), lambda i, t, pt: (0, 0)),
                      pl.BlockSpec((d, PD_TN), lambda i, t, pt: (0, t)),
                      pl.BlockSpec((heads, N_BUCKETS), lambda i, t, pt: (0, 0)),
                      pl.BlockSpec(diff_lambda.shape, lambda i, t, pt: (0, 0)),
                      vec, vec, vec, hbm, hbm],
            out_specs=[pl.BlockSpec((None, PD_TM, PD_TN),
                                    lambda i, t, pt: (t // tiles_per_group, i, t % tiles_per_group)),
                       pl.BlockSpec((1, heads, HEAD_W), lambda i, t, pt: (seq_of(i, t), 0, 0))],
            scratch_shapes=[pltpu.VMEM((PD_TM, d), BF16),
                            pltpu.VMEM((srows, HEAD_W), BF16), pltpu.VMEM((srows, 1), F32),
                            pltpu.VMEM((srows, 1), F32), pltpu.VMEM((srows, HEAD_W), F32),
                            pltpu.VMEM((srows, page * heads), F32),
                            pltpu.VMEM((DEC_RING, npp, page * heads, HEAD_W), F32),
                            pltpu.VMEM((DEC_RING, npp, page * heads, HEAD_W), F32),
                            pltpu.SemaphoreType.DMA((DEC_RING,))]),
        out_shape=[jax.ShapeDtypeStruct((n // GROUP_W, rows, GROUP_W), F32),
                   jax.ShapeDtypeStruct((db, heads, HEAD_W), F32)],
        compiler_params=pltpu.CompilerParams(dimension_semantics=("arbitrary", "arbitrary"),
                                             vmem_limit_bytes=VMEM_LIMIT),
        name="proj_decode",
    )(page_table, x, g, w_bf, rel_bias.T, diff_lambda, q_s, k_s, v_s, ck, cv)


KV_ROWS = 256


def _assemble_kv_kernel(k_ref, v_ref, kp_ref, vp_ref, km_ref, vm_ref, ko_ref, vo_ref):
    first = pl.program_id(1) == 0
    for src_ref, prev_ref, meta_ref, out_ref in ((k_ref, kp_ref, km_ref, ko_ref), (v_ref, vp_ref, vm_ref, vo_ref)):
        lead = jnp.where(first, meta_ref[...], prev_ref[0])
        rows = jnp.concatenate([lead, src_ref[0, :KV_ROWS - N_META, :]], axis=0)
        out_ref[...] = rows.reshape(out_ref.shape)


def _assemble_kv(z_p, z_small, batch, seq, heads):
    assert seq % KV_ROWS == 0 and KV_ROWS % N_META == 0 and N_META % 8 == 0
    n_blk = seq // KV_ROWS
    per = KV_ROWS // N_META
    zp4 = z_p.reshape(z_p.shape[0], batch, seq, GROUP_W)
    cur = lambda g: pl.BlockSpec((None, 1, KV_ROWS, GROUP_W),
                                 lambda b, i, g=g: (g, b, jnp.minimum(i, n_blk - 1), 0))
    prev = lambda g: pl.BlockSpec((None, 1, N_META, GROUP_W),
                                  lambda b, i, g=g: (g, b, jnp.maximum(i * per - 1, 0), 0))
    meta = lambda g: pl.BlockSpec((None, N_META, GROUP_W), lambda b, i, g=g: (g, 0, 0))
    out = pl.BlockSpec((1, KV_ROWS, heads, HEAD_W), lambda b, i: (b, i, 0, 0))
    shape = jax.ShapeDtypeStruct((batch, N_META + seq, heads, HEAD_W), F32)
    return pl.pallas_call(
        _assemble_kv_kernel,
        grid=(batch, n_blk + 1),
        in_specs=[cur(G_AK), cur(G_AV), prev(G_AK), prev(G_AV), meta(G_AK), meta(G_AV)],
        out_specs=[out, out],
        out_shape=[shape, shape],
        compiler_params=pltpu.CompilerParams(dimension_semantics=("arbitrary", "arbitrary"),
                                             vmem_limit_bytes=VMEM_LIMIT),
        name="assemble_kv",
    )(zp4, zp4, zp4, zp4, z_small, z_small)


def kernel(x_prompt, x_sample, cache_k, cache_v, state_hgrn, page_table, meta_tokens, rel_bias, hgrn_lb,
           norm_g, w_in, hgrn_norm_g, diff_norm_g, diff_lambda, w_out, final_norm_g):
    batch, seq, d = x_prompt.shape
    db = x_sample.shape[0]
    heads = cache_k.shape[3]
    gw = heads * HEAD_W
    assert norm_g.shape[0] == 1 and x_sample.shape[1] == 1, "one layer, one sample token per sequence"
    assert gw == GROUP_W

    w_out_bf = w_out[0].astype(BF16)
    g_in = norm_g[0].reshape(1, d)
    g_final = final_norm_g.reshape(1, d)
    g_hg = hgrn_norm_g[0].reshape(1, HEAD_W)
    g_at = diff_norm_g[0].reshape(1, HEAD_W)
    lam_p = diff_lambda[0]

    xp = x_prompt.reshape(batch * seq, d)
    xs = x_sample.reshape(db, d)
    small = jnp.concatenate([meta_tokens.astype(F32), xs], axis=0)

    z_small, w_in_bf = _proj_small(small, g_in, w_in[0])
    z_samp = z_small[:, N_META:]
    samp_heads = lambda g: z_samp[g].reshape(db, heads, HEAD_W)
    q_s, k_s, v_s = samp_heads(G_AQ), samp_heads(G_AK), samp_heads(G_AV)

    z_p, o_at_s = _proj_decode(xp, g_in, w_in_bf, q_s, k_s, v_s, cache_k[0], cache_v[0], page_table,
                               rel_bias, lam_p)

    o_hg_p, s_p = _hgrn_prompt(z_p, z_small, hgrn_lb, batch, seq, heads)
    o_at_p = _attn_prompt(z_p, z_small, rel_bias, lam_p, batch, seq, heads)
    y_p = _merge(o_hg_p.reshape(batch * seq, gw), o_at_p.reshape(batch * seq, gw), z_p, xp,
                 g_hg, g_at, w_out_bf, g_final, tm=512, heads=heads)

    o_hg_s, s_s = _hgrn_step(z_samp, hgrn_lb, state_hgrn, heads)
    y_s = _merge(o_hg_s, o_at_s.reshape(db, gw), z_samp, xs, g_hg, g_at, w_out_bf, g_final, tm=db, heads=heads)

    k_p, v_p = _assemble_kv(z_p, z_small, batch, seq, heads)

    return (y_p.reshape(batch, seq, d), y_s.reshape(db, 1, d), k_p[None], v_p[None],
            s_p[None], k_s.reshape(1, db, 1, heads, HEAD_W), v_s.reshape(1, db, 1, heads, HEAD_W), s_s)
```

```python
import functools
import math

import numpy as np
import jax
import jax.numpy as jnp
from jax import lax
from jax.experimental import pallas as pl
from jax.experimental.pallas import tpu as pltpu

F32 = jnp.float32
BF16 = jnp.bfloat16

N_META = 16
HEAD_W = 128
AT_DQK = 64
N_BUCKETS = 32
MAX_DISTANCE = 128
MAX_EXACT = N_BUCKETS // 2
HGRN_CHUNK = 64
RMS_EPS = 1e-6
NEG_INF = -1e30
LAMBDA_INIT = 0.8 - 0.6 * math.exp(-0.3 * 0)
G_HQ, G_HF, G_HI, G_HG, G_AQ, G_AK, G_AV, G_AG = range(8)

VMEM_LIMIT = 56 * 1024 * 1024


def _bucket_thresholds():
    n = np.arange(0, 2 * MAX_DISTANCE)
    nl = np.maximum(n, MAX_EXACT).astype(np.float32)
    large = MAX_EXACT + (np.log(nl / np.float32(MAX_EXACT)) / np.float32(math.log(MAX_DISTANCE / MAX_EXACT))
                         * np.float32(N_BUCKETS - MAX_EXACT)).astype(np.int32)
    bucket = np.where(n < MAX_EXACT, n, np.minimum(large, N_BUCKETS - 1))
    return [int(np.argmax(bucket >= b)) for b in range(N_BUCKETS)]


BUCKET_THR = _bucket_thresholds()
CONST_BIAS_DIST = BUCKET_THR[N_BUCKETS - 1]


def _sigmoid(x):
    return 1.0 / (1.0 + jnp.exp(-x))


def _silu(x):
    return x * _sigmoid(x)


def _dot_nt(a, b):
    return lax.dot_general(a, b, (((1,), (1,)), ((), ())), preferred_element_type=F32)


def _dot_tn(a, b):
    return lax.dot_general(a, b, (((0,), (0,)), ((), ())), preferred_element_type=F32)


GROUP_W = 1024


def _proj_small_kernel(x_ref, g_ref, w_ref, z_ref, wb_ref, xn_ref):
    @pl.when(pl.program_id(0) == 0)
    def _():
        x = x_ref[...]
        ms = jnp.mean(x * x, axis=-1, keepdims=True)
        xn_ref[...] = (x * lax.rsqrt(ms + RMS_EPS) * g_ref[...]).astype(BF16)

    w_bf = w_ref[...].astype(BF16)
    wb_ref[...] = w_bf
    z_ref[...] = jnp.dot(xn_ref[...], w_bf, preferred_element_type=F32)


def _proj_small(x, g, w):
    rows, d = x.shape
    n = w.shape[1]
    return pl.pallas_call(
        _proj_small_kernel,
        grid=(n // GROUP_W,),
        in_specs=[pl.BlockSpec((rows, d), lambda j: (0, 0)),
                  pl.BlockSpec((1, d), lambda j: (0, 0)),
                  pl.BlockSpec((d, GROUP_W), lambda j: (0, j))],
        out_specs=[pl.BlockSpec((None, rows, GROUP_W), lambda j: (j, 0, 0)),
                   pl.BlockSpec((d, GROUP_W), lambda j: (0, j))],
        out_shape=[jax.ShapeDtypeStruct((n // GROUP_W, rows, GROUP_W), F32),
                   jax.ShapeDtypeStruct((d, n), BF16)],
        scratch_shapes=[pltpu.VMEM((rows, d), BF16)],
        compiler_params=pltpu.CompilerParams(dimension_semantics=("arbitrary",),
                                             vmem_limit_bytes=VMEM_LIMIT),
        name="proj_small",
    )(x, g, w)


def _forget_lower_bound(lbp):
    e = jnp.exp(lbp - jnp.max(lbp, axis=0, keepdims=True))
    return e[0:1] / jnp.sum(e, axis=0, keepdims=True)


def _cumsum_rows(g, tri_bf):
    g1 = g.astype(BF16)
    r1 = g - g1.astype(F32)
    g2 = r1.astype(BF16)
    g3 = (r1 - g2.astype(F32)).astype(BF16)
    dot = functools.partial(jnp.dot, preferred_element_type=F32)
    return dot(tri_bf, g1) + dot(tri_bf, g2) + dot(tri_bf, g3)


def _hgrn_chunk_local(q, hf, v, lb):
    c = q.shape[0]
    row = lax.broadcasted_iota(jnp.int32, (c, c), 0)
    col = lax.broadcasted_iota(jnp.int32, (c, c), 1)
    causal = row >= col
    f = lb + (1.0 - lb) * _sigmoid(hf)
    k = 1.0 - f
    b = _cumsum_rows(jnp.log(f), causal.astype(BF16))
    qe = (q * jnp.exp(b)).astype(BF16)
    ke = (k * jnp.exp(-b)).astype(BF16)
    a = jnp.where(causal, _dot_nt(qe, ke), 0.0)
    v_bf = v.astype(BF16)
    intra = jnp.dot(a.astype(BF16), v_bf, preferred_element_type=F32)
    b_last = b[c - 1:c]
    kd = (k * jnp.exp(b_last - b)).astype(BF16)
    return qe, intra, _dot_tn(v_bf, kd), jnp.exp(b_last)


HGRN_SLAB = 256
HGRN_PAIR = 128


def _hgrn_prompt_kernel(q_ref, f_ref, v_ref, qm_ref, fm_ref, vm_ref, lbp_ref, o_ref, s_ref,
                        b_ref, qe_ref, ke_ref, kd_ref, vb_ref, vt_ref, a_ref, ut_ref, dec_ref, st_ref):
    ck, slab, pair = HGRN_CHUNK, HGRN_SLAB, HGRN_PAIR
    seq = q_ref.shape[1]
    n_chunks = seq // ck
    dot = functools.partial(jnp.dot, preferred_element_type=F32)
    lb = _forget_lower_bound(lbp_ref[...])

    _, _, st, _ = _hgrn_chunk_local(qm_ref[...], fm_ref[...], vm_ref[...], lb)

    f = lb + (1.0 - lb) * _sigmoid(f_ref[0])
    g = jnp.log(f)
    g1 = g.astype(BF16)
    r1 = g - g1.astype(F32)
    g2 = r1.astype(BF16)
    g3 = (r1 - g2.astype(F32)).astype(BF16)
    gcat = jnp.concatenate([g1, g2, g3], axis=1)
    r = lax.broadcasted_iota(jnp.int32, (slab, slab), 0)
    c = lax.broadcasted_iota(jnp.int32, (slab, slab), 1)
    tri_bd = ((r // ck == c // ck) & (r >= c)).astype(BF16)
    for si in range(seq // slab):
        rows = slice(si * slab, (si + 1) * slab)
        t = dot(tri_bd, gcat[rows])
        b_ref[rows, :] = t[:, :HEAD_W] + t[:, HEAD_W:2 * HEAD_W] + t[:, 2 * HEAD_W:]

    k = 1.0 - f
    b = b_ref[...]
    qe_ref[...] = (q_ref[0] * jnp.exp(b)).astype(BF16)
    ke_ref[...] = (k * jnp.exp(-b)).astype(BF16)
    v = v_ref[0]
    vb_ref[...] = v.astype(BF16)
    vt_ref[...] = v.T.astype(BF16)
    for ci in range(n_chunks):
        rows = slice(ci * ck, (ci + 1) * ck)
        b_last = b_ref[(ci + 1) * ck - 1:(ci + 1) * ck, :]
        dec_ref[ci] = jnp.exp(b_last)
        kd_ref[rows, :] = (k[rows] * jnp.exp(b_last - b_ref[rows, :])).astype(BF16)

    rp = lax.broadcasted_iota(jnp.int32, (pair, pair), 0)
    cp = lax.broadcasted_iota(jnp.int32, (pair, pair), 1)
    keep = (rp // ck == cp // ck) & (rp >= cp)
    for pi in range(seq // pair):
        rows = slice(pi * pair, (pi + 1) * pair)
        a_ref[rows, :] = jnp.where(keep, _dot_nt(qe_ref[rows, :], ke_ref[rows, :]), 0.0).astype(BF16)

    for pi in range(seq // pair):
        rows = slice(pi * pair, (pi + 1) * pair)
        o_ref[0, rows, :] = dot(a_ref[rows, :], vb_ref[rows, :])
    for ci in range(n_chunks):
        rows = slice(ci * ck, (ci + 1) * ck)
        ut_ref[ci] = dot(vt_ref[:, rows], kd_ref[rows, :])

    for ci in range(n_chunks):
        st_ref[ci] = st.astype(BF16)
        st = st * dec_ref[ci] + ut_ref[ci]
    s_ref[0, 0] = st.T

    for ci in range(n_chunks):
        rows = slice(ci * ck, (ci + 1) * ck)
        o_ref[0, rows, :] += _dot_nt(qe_ref[rows, :], st_ref[ci])


def _hgrn_prompt(z_p, z_small, hgrn_lb, batch, seq, heads):
    zp3 = z_p.reshape(z_p.shape[0], batch, seq, GROUP_W)
    big = lambda g: pl.BlockSpec((None, 1, seq, HEAD_W), lambda b, h, g=g: (g, b, 0, h))
    meta = lambda g: pl.BlockSpec((None, N_META, HEAD_W), lambda b, h, g=g: (g, 0, h))
    return pl.pallas_call(
        _hgrn_prompt_kernel,
        grid=(batch, heads),
        in_specs=[big(G_HQ), big(G_HF), big(G_HI), meta(G_HQ), meta(G_HF), meta(G_HI),
                  pl.BlockSpec((hgrn_lb.shape[0], HEAD_W), lambda b, h: (0, h))],
        out_specs=[pl.BlockSpec((1, seq, HEAD_W), lambda b, h: (b, 0, h)),
                   pl.BlockSpec((1, 1, HEAD_W, HEAD_W), lambda b, h: (b, h, 0, 0))],
        out_shape=[jax.ShapeDtypeStruct((batch, seq, heads * HEAD_W), F32),
                   jax.ShapeDtypeStruct((batch, heads, HEAD_W, HEAD_W), F32)],
        scratch_shapes=[pltpu.VMEM((seq, HEAD_W), F32),
                        pltpu.VMEM((seq, HEAD_W), BF16), pltpu.VMEM((seq, HEAD_W), BF16),
                        pltpu.VMEM((seq, HEAD_W), BF16), pltpu.VMEM((seq, HEAD_W), BF16),
                        pltpu.VMEM((HEAD_W, seq), BF16), pltpu.VMEM((seq, HGRN_PAIR), BF16),
                        pltpu.VMEM((seq // HGRN_CHUNK, HEAD_W, HEAD_W), F32),
                        pltpu.VMEM((seq // HGRN_CHUNK, 1, HEAD_W), F32),
                        pltpu.VMEM((seq // HGRN_CHUNK, HEAD_W, HEAD_W), BF16)],
        compiler_params=pltpu.CompilerParams(dimension_semantics=("arbitrary", "arbitrary"),
                                             vmem_limit_bytes=VMEM_LIMIT),
        name="hgrn_prompt",
    )(zp3, zp3, zp3, z_small, z_small, z_small, hgrn_lb)


def _diff_lambda(lp):
    s01 = jnp.sum(lp[0:1] * lp[1:2], axis=-1, keepdims=True)
    s23 = jnp.sum(lp[2:3] * lp[3:4], axis=-1, keepdims=True)
    return jnp.exp(s01) - jnp.exp(s23) + LAMBDA_INIT


def _bias_by_distance(dist, bias_of_bucket):
    out = jnp.where(dist >= BUCKET_THR[1], bias_of_bucket(1), bias_of_bucket(0))
    for bkt in range(2, N_BUCKETS):
        out = jnp.where(dist >= BUCKET_THR[bkt], bias_of_bucket(bkt), out)
    return out


ATT_TQ = 256
ATT_AHEAD = 2


def _colmax(s):
    return jnp.max(s.reshape(s.shape[0] // 8, 8, s.shape[1]), axis=0)


def _colsum(s):
    return jnp.sum(s.reshape(s.shape[0] // 8, 8, s.shape[1]), axis=0)


def _assemble_kv_block(main_ref, lead_refs, meta_ref, out_ref, first):
    lead = jnp.where(first, meta_ref[...], jnp.concatenate([r[0] for r in lead_refs], axis=0))
    rows = jnp.concatenate([lead, main_ref[0, :out_ref.shape[1] - N_META, :]], axis=0)
    out_ref[...] = rows.reshape(out_ref.shape)


def _attn_prompt_kernel(rb_ref, lp_ref, q_ref, k_ref, v_ref, km_ref, vm_ref,
                        ka_ref, kl0_ref, kl1_ref, kma_ref, va_ref, vl0_ref, vl1_ref, vma_ref,
                        o_ref, ko_ref, vo_ref,
                        kb_ref, vt_ref, vmt_ref, s_ref, p_ref, bdiag_ref, bnear_ref, bmeta_ref):
    h = pl.program_id(0)
    b = pl.program_id(1)
    _assemble_kv_block(ka_ref, (kl0_ref, kl1_ref), kma_ref, ko_ref, h == 0)
    _assemble_kv_block(va_ref, (vl0_ref, vl1_ref), vma_ref, vo_ref, h == 0)
    tq = ATT_TQ
    n_q = q_ref.shape[1] // tq
    log2e = 1.0 / math.log(2.0)
    bias_of_bucket = lambda bkt: rb_ref[bkt * rb_ref.shape[0] // N_BUCKETS + h] * log2e
    far_bias = bias_of_bucket(N_BUCKETS - 1)
    key = lax.broadcasted_iota(jnp.int32, (tq, tq), 0)
    qry = lax.broadcasted_iota(jnp.int32, (tq, tq), 1)

    @pl.when(b == 0)
    def _():
        bdiag_ref[...] = _bias_by_distance(qry - key, bias_of_bucket)
        bnear_ref[...] = _bias_by_distance(qry - key + tq, bias_of_bucket)
        keym = lax.broadcasted_iota(jnp.int32, (N_META, tq), 0)
        qrym = lax.broadcasted_iota(jnp.int32, (N_META, tq), 1)
        bmeta_ref[...] = _bias_by_distance(qrym + N_META - keym, bias_of_bucket)

    kb_ref[...] = k_ref[0].astype(BF16)
    vt_ref[...] = v_ref[0].T.astype(BF16)
    pad = jnp.zeros((HEAD_W - N_META, HEAD_W), F32)
    vmt_ref[...] = jnp.concatenate([vm_ref[...], pad], axis=0).T.astype(BF16)
    km_bf = km_ref[...].astype(BF16)
    lam = _diff_lambda(lp_ref[...])
    causal = qry >= key
    dot = functools.partial(jnp.dot, preferred_element_type=F32)

    def pass_a(qi):
        par = qi % (ATT_AHEAD + 1)
        q = q_ref[0, qi * tq:(qi + 1) * tq, :] * (AT_DQK ** -0.5 * log2e)
        lane = lax.broadcasted_iota(jnp.int32, q.shape, 1)
        qc = (jnp.where(lane < AT_DQK, q, 0.0).astype(BF16), jnp.where(lane >= AT_DQK, q, 0.0).astype(BF16))
        n_keys = (qi + 1) * tq
        out = []
        for c in range(2):
            s_ref[par, c, :n_keys, :] = _dot_nt(kb_ref[:n_keys, :], qc[c])
            sm = _dot_nt(km_bf, qc[c]) + (bmeta_ref[...] if qi == 0 else far_bias)
            m8 = _colmax(sm)
            for j in range(qi + 1):
                m8 = jnp.maximum(m8, _colmax(biased(s_ref[par, c, j * tq:(j + 1) * tq, :], j, qi))
                                 + (far_bias if j < qi - 1 else 0.0))
            out.append((sm, jnp.max(m8, axis=0, keepdims=True)))
        return out

    def biased(s, j, qi):
        if j == qi:
            return jnp.where(causal, s + bdiag_ref[...], NEG_INF)
        if j == qi - 1:
            return s + bnear_ref[...]
        return s

    def pass_b(qi, stats):
        par = qi % (ATT_AHEAD + 1)
        n_keys = (qi + 1) * tq
        o_c = []
        for c in range(2):
            sm, m = stats[c]
            m_far = m - far_bias
            pm = jnp.exp2(sm - m)
            l8 = _colsum(pm)
            for j in range(qi + 1):
                rows = slice(j * tq, (j + 1) * tq)
                p = jnp.exp2(biased(s_ref[par, c, rows, :], j, qi) - (m_far if j < qi - 1 else m))
                l8 = l8 + _colsum(p)
                p_ref[qi % 2, c, rows, :] = p.astype(BF16)
            acc = dot(vt_ref[:, :n_keys], p_ref[qi % 2, c, :n_keys, :]) + dot(vmt_ref[:, :N_META], pm.astype(BF16))
            o_c.append(acc / jnp.sum(l8, axis=0, keepdims=True))
        o_ref[0, qi * tq:(qi + 1) * tq, :] = (o_c[0] - lam * o_c[1]).T

    stats = [pass_a(qi) for qi in range(min(ATT_AHEAD, n_q))]
    for qi in range(n_q):
        if qi + ATT_AHEAD < n_q:
            stats.append(pass_a(qi + ATT_AHEAD))
        pass_b(qi, stats.pop(0))


def _attn_prompt(z_p, z_small, rel_bias, diff_lambda, batch, seq, heads):
    assert CONST_BIAS_DIST <= ATT_TQ + 1 and CONST_BIAS_DIST <= N_META + 1 + ATT_TQ - N_META
    zp3 = z_p.reshape(z_p.shape[0], batch, seq, GROUP_W)
    tq = ATT_TQ
    full = lambda g: pl.BlockSpec((None, 1, seq, HEAD_W), lambda h, b, g=g: (g, b, 0, h))
    meta = lambda g: pl.BlockSpec((None, N_META, HEAD_W), lambda h, b, g=g: (g, 0, h))

    tokens = N_META + seq
    blk = -(-tokens // (8 * heads)) * 8
    assert N_META == 16 and (heads - 1) * blk < seq
    asm_main = lambda g: pl.BlockSpec((None, 1, blk, GROUP_W), lambda h, b, g=g: (g, b, h, 0))
    asm_lead = lambda g, piece: pl.BlockSpec(
        (None, 1, 8, GROUP_W),
        lambda h, b, g=g, piece=piece: (g, b, jnp.maximum(h * (blk // 8) - 2 + piece, 0), 0))
    asm_meta = lambda g: pl.BlockSpec((None, N_META, GROUP_W), lambda h, b, g=g: (g, 0, 0))
    asm_in = lambda g: [asm_main(g), asm_lead(g, 0), asm_lead(g, 1), asm_meta(g)]
    asm_out = pl.BlockSpec((1, blk, heads, HEAD_W), lambda h, b: (b, h, 0, 0))
    kv_shape = jax.ShapeDtypeStruct((batch, tokens, heads, HEAD_W), F32)

    return pl.pallas_call(
        _attn_prompt_kernel,
        grid=(heads, batch),
        in_specs=[pl.BlockSpec(memory_space=pltpu.SMEM),
                  pl.BlockSpec(diff_lambda.shape, lambda h, b: (0, 0)),
                  full(G_AQ), full(G_AK), full(G_AV), meta(G_AK), meta(G_AV)] + asm_in(G_AK) + asm_in(G_AV),
        out_specs=[pl.BlockSpec((1, seq, HEAD_W), lambda h, b: (b, 0, h)), asm_out, asm_out],
        out_shape=[jax.ShapeDtypeStruct((batch, seq, heads * HEAD_W), F32), kv_shape, kv_shape],
        scratch_shapes=[pltpu.VMEM((seq, HEAD_W), BF16), pltpu.VMEM((HEAD_W, seq), BF16),
                        pltpu.VMEM((HEAD_W, HEAD_W), BF16),
                        pltpu.VMEM((ATT_AHEAD + 1, 2, seq, tq), F32), pltpu.VMEM((2, 2, seq, tq), BF16),
                        pltpu.VMEM((tq, tq), F32), pltpu.VMEM((tq, tq), F32),
                        pltpu.VMEM((N_META, tq), F32)],
        compiler_params=pltpu.CompilerParams(dimension_semantics=("arbitrary", "arbitrary"),
                                             vmem_limit_bytes=VMEM_LIMIT),
        name="attn_prompt",
    )(rel_bias.reshape(-1), diff_lambda, zp3, zp3, zp3, z_small, z_small,
      zp3, zp3, zp3, z_small, zp3, zp3, zp3, z_small)


def _head_rmsnorm(o, g):
    parts = []
    for hh in range(o.shape[1] // HEAD_W):
        oh = o[:, hh * HEAD_W:(hh + 1) * HEAD_W]
        ms = jnp.mean(oh * oh, axis=-1, keepdims=True)
        parts.append(oh * lax.rsqrt(ms + RMS_EPS) * g)
    return jnp.concatenate(parts, axis=1)


MERGE_SUB = 256


def _merge_kernel(ohg_ref, hg_ref, oat_ref, ag_ref, x_ref, gh_ref, ga_ref, w_ref, gf_ref, y_ref):
    tm = x_ref.shape[0]
    sub = min(MERGE_SUB, tm)
    subs = [slice(s, s + sub) for s in range(0, tm, sub)]
    mixed = []
    for r in subs:
        mix_h = _head_rmsnorm(ohg_ref[r, :], gh_ref[...]) * _silu(hg_ref[r, :])
        mix_a = _head_rmsnorm(oat_ref[r, :], ga_ref[...]) * (1.0 - LAMBDA_INIT) * _silu(ag_ref[r, :])
        mixed.append(jnp.concatenate([mix_h, mix_a], axis=1).astype(BF16))
    proj = [jnp.dot(m, w_ref[...], preferred_element_type=F32) for m in mixed]
    for r, p in zip(subs, proj):
        hres = x_ref[r, :] + p
        ms = jnp.mean(hres * hres, axis=-1, keepdims=True)
        y_ref[r, :] = hres * lax.rsqrt(ms + RMS_EPS) * gf_ref[...]


def _merge(o_hg, o_at, z, x, g_hg, g_at, w_out_bf, g_final, tm, heads):
    rows, d = x.shape
    gw = heads * HEAD_W
    row_blk = lambda w: pl.BlockSpec((tm, w), lambda i: (i, 0))
    gate = lambda g: pl.BlockSpec((None, tm, gw), lambda i, g=g: (g, i, 0))
    const = lambda shape: pl.BlockSpec(shape, lambda i: (0, 0))
    return pl.pallas_call(
        _merge_kernel,
        grid=(rows // tm,),
        in_specs=[row_blk(gw), gate(G_HG), row_blk(gw), gate(G_AG), row_blk(d),
                  const((1, HEAD_W)), const((1, HEAD_W)),
                  pl.BlockSpec(w_out_bf.shape, lambda i: (0, 0), pipeline_mode=pl.Buffered(1)),
                  const((1, d))],
        out_specs=row_blk(d),
        out_shape=jax.ShapeDtypeStruct((rows, d), F32),
        compiler_params=pltpu.CompilerParams(dimension_semantics=("arbitrary",),
                                             vmem_limit_bytes=VMEM_LIMIT),
        name="merge",
    )(o_hg, z, o_at, z, x, g_hg, g_at, w_out_bf, g_final)


def _transpose_rows_to_cols(x):
    pad = jnp.zeros((HEAD_W - x.shape[0], HEAD_W), F32)
    return jnp.concatenate([x, pad], axis=0).T


def _hgrn_step_kernel(q_ref, f_ref, v_ref, lbp_ref, s_ref, o_ref, sn_ref):
    lb = _forget_lower_bound(lbp_ref[...])
    f = lb + (1.0 - lb) * _sigmoid(f_ref[...])
    k = 1.0 - f
    b = jnp.log(f)
    qe = q_ref[...] * jnp.exp(b)
    ke = k * jnp.exp(-b)
    a = jnp.sum(qe * ke, axis=-1, keepdims=True)
    v = v_ref[...]
    decay_t = _transpose_rows_to_cols(jnp.exp(b))
    kd_t = _transpose_rows_to_cols(k * jnp.exp(b - b))
    qe_t = _transpose_rows_to_cols(qe)
    for i in range(q_ref.shape[0]):
        s = s_ref[0, i, 0]
        vi = v[i:i + 1]
        o_ref[i:i + 1, :] = jnp.sum(qe_t[:, i:i + 1] * s, axis=0, keepdims=True) + a[i:i + 1] * vi
        sn_ref[0, i, 0] = decay_t[:, i:i + 1] * s + kd_t[:, i:i + 1] * vi


def _hgrn_step(z_samp, hgrn_lb, state, heads):
    db = z_samp.shape[1]
    col = lambda g: pl.BlockSpec((None, db, HEAD_W), lambda h, g=g: (g, 0, h))
    st_spec = pl.BlockSpec((1, db, 1, HEAD_W, HEAD_W), lambda h: (0, 0, h, 0, 0))
    return pl.pallas_call(
        _hgrn_step_kernel,
        grid=(heads,),
        in_specs=[col(G_HQ), col(G_HF), col(G_HI),
                  pl.BlockSpec((hgrn_lb.shape[0], HEAD_W), lambda h: (0, h)), st_spec],
        out_specs=[pl.BlockSpec((db, HEAD_W), lambda h: (0, h)), st_spec],
        out_shape=[jax.ShapeDtypeStruct((db, heads * HEAD_W), F32),
                   jax.ShapeDtypeStruct(state.shape, F32)],
        compiler_params=pltpu.CompilerParams(dimension_semantics=("arbitrary",),
                                             vmem_limit_bytes=VMEM_LIMIT),
        name="hgrn_step",
    )(z_samp, z_samp, z_samp, hgrn_lb, state)


DEC_PAGES_PER_STEP = 8
DEC_RING = 3


def _page_copies(pt_ref, ck_ref, cv_ref, kbuf_ref, vbuf_ref, sem_ref, step, n_steps):
    npp = DEC_PAGES_PER_STEP
    slot = step % DEC_RING
    seq = step // n_steps
    first_page = (step % n_steps) * npp
    copies = []
    for p in range(npp):
        phys = pt_ref[seq, first_page + p]
        copies.append(pltpu.make_async_copy(ck_ref.at[phys], kbuf_ref.at[slot, p], sem_ref.at[slot]))
        copies.append(pltpu.make_async_copy(cv_ref.at[phys], vbuf_ref.at[slot, p], sem_ref.at[slot]))
    return copies


def _proj_decode_kernel(pt_ref, x_ref, g_ref, w_ref, rbt_ref, lp_ref, q_ref, kn_ref, vn_ref, ck_ref, cv_ref,
                        z_ref, o_ref, xn_ref, qs_ref, m_ref, l_ref, acc_ref, blast_ref, kbuf_ref, vbuf_ref, sem_ref,
                        *, heads, page, n_steps):
    npp = DEC_PAGES_PER_STEP
    step = pl.program_id(0) * pl.num_programs(1) + pl.program_id(1)
    total_steps = pl.num_programs(0) * pl.num_programs(1)
    copies_for = lambda s: _page_copies(pt_ref, ck_ref, cv_ref, kbuf_ref, vbuf_ref, sem_ref, s, n_steps)

    @pl.when(step == 0)
    def _():
        for s in range(DEC_RING - 1):
            for cp in copies_for(s):
                cp.start()

    @pl.when(step + (DEC_RING - 1) < total_steps)
    def _():
        for cp in copies_for(step + (DEC_RING - 1)):
            cp.start()

    @pl.when(pl.program_id(1) == 0)
    def _():
        x = x_ref[...]
        ms = jnp.mean(x * x, axis=-1, keepdims=True)
        xn_ref[...] = (x * lax.rsqrt(ms + RMS_EPS) * g_ref[...]).astype(BF16)

    for cp in copies_for(step):
        cp.wait()
    slot = step % DEC_RING
    k_refs = [kbuf_ref.at[slot, p] for p in range(npp)]
    v_refs = [vbuf_ref.at[slot, p] for p in range(npp)]

    j = pl.program_id(1) % n_steps
    rows = 2 * heads
    lanes = page * heads
    rbt = jnp.concatenate([rbt_ref[...], rbt_ref[...]], axis=0)
    far_bias = rbt[:, N_BUCKETS - 1:N_BUCKETS]
    r_id = lax.broadcasted_iota(jnp.int32, (rows, lanes), 0)
    l_id = lax.broadcasted_iota(jnp.int32, (rows, lanes), 1)
    own_head = l_id % heads == r_id % heads

    def stacked(x):
        lane = lax.broadcasted_iota(jnp.int32, x.shape, 1)
        return jnp.concatenate([jnp.where(lane < AT_DQK, x, 0.0), jnp.where(lane >= AT_DQK, x, 0.0)], axis=0)

    @pl.when(j == 0)
    def _():
        qs_ref[...] = stacked(q_ref[0] * (AT_DQK ** -0.5)).astype(BF16)
        m_ref[...] = jnp.full(m_ref.shape, NEG_INF, F32)
        l_ref[...] = jnp.zeros(l_ref.shape, F32)
        acc_ref[...] = jnp.zeros(acc_ref.shape, F32)
        blast_ref[...] = _bias_by_distance(page - l_id // heads, lambda bkt: rbt[:, bkt:bkt + 1])

    qs = qs_ref[...]
    s_parts = []
    for i in range(npp):
        s_i = _dot_nt(qs, k_refs[i][...].astype(BF16))
        if i == npp - 1:
            s_i = s_i + jnp.where(j == n_steps - 1, blast_ref[...], far_bias)
        else:
            s_i = s_i + far_bias
        s_parts.append(jnp.where(own_head, s_i, NEG_INF))

    m_old = m_ref[...]
    m_blk = s_parts[0]
    for s_i in s_parts[1:]:
        m_blk = jnp.maximum(m_blk, s_i)
    m_new = jnp.maximum(m_old, jnp.max(m_blk, axis=-1, keepdims=True))
    alpha = jnp.exp(m_old - m_new)

    z_ref[...] = jnp.dot(xn_ref[...], w_ref[...], preferred_element_type=F32)

    l_blk = jnp.zeros((rows, lanes), F32)
    pv = jnp.zeros(acc_ref.shape, F32)
    for i in range(npp):
        p_i = jnp.exp(s_parts[i] - m_new)
        l_blk = l_blk + p_i
        pv = pv + jnp.dot(p_i.astype(BF16), v_refs[i][...].astype(BF16), preferred_element_type=F32)
    l_ref[...] = alpha * l_ref[...] + jnp.sum(l_blk, axis=-1, keepdims=True)
    acc_ref[...] = alpha * acc_ref[...] + pv
    m_ref[...] = m_new

    @pl.when(j == n_steps - 1)
    def _():
        kn = jnp.concatenate([kn_ref[0], kn_ref[0]], axis=0)
        vn = jnp.concatenate([vn_ref[0], vn_ref[0]], axis=0)
        s_new = jnp.sum(qs.astype(F32) * kn, axis=-1, keepdims=True) + rbt[:, 0:1]
        m_old = m_ref[...]
        m_fin = jnp.maximum(m_old, s_new)
        alpha = jnp.exp(m_old - m_fin)
        p_new = jnp.exp(s_new - m_fin)
        l_fin = alpha * l_ref[...] + p_new
        o = (alpha * acc_ref[...] + p_new * vn) / l_fin
        o_ref[0] = o[:heads] - _diff_lambda(lp_ref[...]) * o[heads:]


PD_TM = 1024
PD_TN = 256


def _proj_decode(x, g, w_bf, q_s, k_s, v_s, cache_k, cache_v, page_table, rel_bias, diff_lambda):
    rows, d = x.shape
    n = w_bf.shape[1]
    db, n_pages = page_table.shape
    n_phys, page, heads, _ = cache_k.shape
    npp = DEC_PAGES_PER_STEP
    n_steps = n_pages // npp
    grid = (rows // PD_TM, n // PD_TN)
    seq_per_row_tile = grid[1] // n_steps
    tiles_per_group = GROUP_W // PD_TN
    assert CONST_BIAS_DIST <= page + 1 and n_pages % npp == 0 and GROUP_W % PD_TN == 0
    assert rows % PD_TM == 0 and n % PD_TN == 0 and grid[1] % n_steps == 0 and grid[0] * seq_per_row_tile == db
    ck = cache_k.reshape(n_phys, page * heads, HEAD_W)
    cv = cache_v.reshape(n_phys, page * heads, HEAD_W)
    seq_of = lambda i, t: i * seq_per_row_tile + t // n_steps
    vec = pl.BlockSpec((1, heads, HEAD_W), lambda i, t, pt: (seq_of(i, t), 0, 0))
    hbm = pl.BlockSpec(memory_space=pl.ANY)
    srows = 2 * heads
    kernel = functools.partial(_proj_decode_kernel, heads=heads, page=page, n_steps=n_steps)
    return pl.pallas_call(
        kernel,
        grid_spec=pltpu.PrefetchScalarGridSpec(
            num_scalar_prefetch=1,
            grid=grid,
            in_specs=[pl.BlockSpec((PD_TM, d), lambda i, t, pt: (i, 0)),
                      pl.BlockSpec((1, d), lambda i, t, pt: (0, 0)),
                      pl.BlockSpec((d, PD_TN), lambda i, t, pt: (0, t)),
                      pl.BlockSpec((heads, N_BUCKETS), lambda i, t, pt: (0, 0)),
                      pl.BlockSpec(diff_lambda.shape, lambda i, t, pt: (0, 0)),
                      vec, vec, vec, hbm, hbm],
            out_specs=[pl.BlockSpec((None, PD_TM, PD_TN),
                                    lambda i, t, pt: (t // tiles_per_group, i, t % tiles_per_group)),
                       pl.BlockSpec((1, heads, HEAD_W), lambda i, t, pt: (seq_of(i, t), 0, 0))],
            scratch_shapes=[pltpu.VMEM((PD_TM, d), BF16),
                            pltpu.VMEM((srows, HEAD_W), BF16), pltpu.VMEM((srows, 1), F32),
                            pltpu.VMEM((srows, 1), F32), pltpu.VMEM((srows, HEAD_W), F32),
                            pltpu.VMEM((srows, page * heads), F32),
                            pltpu.VMEM((DEC_RING, npp, page * heads, HEAD_W), F32),
                            pltpu.VMEM((DEC_RING, npp, page * heads, HEAD_W), F32),
                            pltpu.SemaphoreType.DMA((DEC_RING,))]),
        out_shape=[jax.ShapeDtypeStruct((n // GROUP_W, rows, GROUP_W), F32),
                   jax.ShapeDtypeStruct((db, heads, HEAD_W), F32)],
        compiler_params=pltpu.CompilerParams(dimension_semantics=("arbitrary", "arbitrary"),
                                             vmem_limit_bytes=VMEM_LIMIT),
        name="proj_decode",
    )(page_table, x, g, w_bf, rel_bias.T, diff_lambda, q_s, k_s, v_s, ck, cv)


def kernel(x_prompt, x_sample, cache_k, cache_v, state_hgrn, page_table, meta_tokens, rel_bias, hgrn_lb,
           norm_g, w_in, hgrn_norm_g, diff_norm_g, diff_lambda, w_out, final_norm_g):
    batch, seq, d = x_prompt.shape
    db = x_sample.shape[0]
    heads = cache_k.shape[3]
    gw = heads * HEAD_W
    assert norm_g.shape[0] == 1 and x_sample.shape[1] == 1, "one layer, one sample token per sequence"
    assert gw == GROUP_W

    w_out_bf = w_out[0].astype(BF16)
    g_in = norm_g[0].reshape(1, d)
    g_final = final_norm_g.reshape(1, d)
    g_hg = hgrn_norm_g[0].reshape(1, HEAD_W)
    g_at = diff_norm_g[0].reshape(1, HEAD_W)
    lam_p = diff_lambda[0]

    xp = x_prompt.reshape(batch * seq, d)
    xs = x_sample.reshape(db, d)
    small = jnp.concatenate([meta_tokens.astype(F32), xs], axis=0)

    z_small, w_in_bf = _proj_small(small, g_in, w_in[0])
    z_samp = z_small[:, N_META:]
    samp_heads = lambda g: z_samp[g].reshape(db, heads, HEAD_W)
    q_s, k_s, v_s = samp_heads(G_AQ), samp_heads(G_AK), samp_heads(G_AV)

    z_p, o_at_s = _proj_decode(xp, g_in, w_in_bf, q_s, k_s, v_s, cache_k[0], cache_v[0], page_table,
                               rel_bias, lam_p)

    o_hg_p, s_p = _hgrn_prompt(z_p, z_small, hgrn_lb, batch, seq, heads)
    o_at_p, k_p, v_p = _attn_prompt(z_p, z_small, rel_bias, lam_p, batch, seq, heads)
    y_p = _merge(o_hg_p.reshape(batch * seq, gw), o_at_p.reshape(batch * seq, gw), z_p, xp,
                 g_hg, g_at, w_out_bf, g_final, tm=512, heads=heads)

    o_hg_s, s_s = _hgrn_step(z_samp, hgrn_lb, state_hgrn, heads)
    y_s = _merge(o_hg_s, o_at_s.reshape(db, gw), z_samp, xs, g_hg, g_at, w_out_bf, g_final, tm=db, heads=heads)

    return (y_p.reshape(batch, seq, d), y_s.reshape(db, 1, d), k_p[None], v_p[None],
            s_p[None], k_s.reshape(1, db, 1, heads, HEAD_W), v_s.reshape(1, db, 1, heads, HEAD_W), s_s)
```

```python
import functools
import math

import numpy as np
import jax
import jax.numpy as jnp
from jax import lax
from jax.experimental import pallas as pl
from jax.experimental.pallas import tpu as pltpu

F32 = jnp.float32
BF16 = jnp.bfloat16

N_META = 16
HEAD_W = 128
AT_DQK = 64
N_BUCKETS = 32
MAX_DISTANCE = 128
MAX_EXACT = N_BUCKETS // 2
HGRN_CHUNK = 64
RMS_EPS = 1e-6
NEG_INF = -1e30
LAMBDA_INIT = 0.8 - 0.6 * math.exp(-0.3 * 0)
G_HQ, G_HF, G_HI, G_HG, G_AQ, G_AK, G_AV, G_AG = range(8)
KV_SLOT = {G_AK: 0, G_AV: 1}
LO_SLOT = {g: (g if g < G_AK else g - 2) for g in range(8) if g not in KV_SLOT}

VMEM_LIMIT = 56 * 1024 * 1024


def _bucket_thresholds():
    n = np.arange(0, 2 * MAX_DISTANCE)
    nl = np.maximum(n, MAX_EXACT).astype(np.float32)
    large = MAX_EXACT + (np.log(nl / np.float32(MAX_EXACT)) / np.float32(math.log(MAX_DISTANCE / MAX_EXACT))
                         * np.float32(N_BUCKETS - MAX_EXACT)).astype(np.int32)
    bucket = np.where(n < MAX_EXACT, n, np.minimum(large, N_BUCKETS - 1))
    return [int(np.argmax(bucket >= b)) for b in range(N_BUCKETS)]


BUCKET_THR = _bucket_thresholds()
CONST_BIAS_DIST = BUCKET_THR[N_BUCKETS - 1]


def _sigmoid(x):
    return 1.0 / (1.0 + jnp.exp(-x))


def _silu(x):
    return x * _sigmoid(x)


def _dot_nt(a, b):
    return lax.dot_general(a, b, (((1,), (1,)), ((), ())), preferred_element_type=F32)


def _dot_tn(a, b):
    return lax.dot_general(a, b, (((0,), (0,)), ((), ())), preferred_element_type=F32)


GROUP_W = 1024


def _proj_small_kernel(x_ref, g_ref, w_ref, z_ref, wb_ref, xn_ref):
    @pl.when(pl.program_id(0) == 0)
    def _():
        x = x_ref[...]
        ms = jnp.mean(x * x, axis=-1, keepdims=True)
        xn_ref[...] = (x * lax.rsqrt(ms + RMS_EPS) * g_ref[...]).astype(BF16)

    w_bf = w_ref[...].astype(BF16)
    wb_ref[...] = w_bf
    z_ref[...] = jnp.dot(xn_ref[...], w_bf, preferred_element_type=F32)


def _proj_small(x, g, w):
    rows, d = x.shape
    n = w.shape[1]
    return pl.pallas_call(
        _proj_small_kernel,
        grid=(n // GROUP_W,),
        in_specs=[pl.BlockSpec((rows, d), lambda j: (0, 0)),
                  pl.BlockSpec((1, d), lambda j: (0, 0)),
                  pl.BlockSpec((d, GROUP_W), lambda j: (0, j))],
        out_specs=[pl.BlockSpec((None, rows, GROUP_W), lambda j: (j, 0, 0)),
                   pl.BlockSpec((d, GROUP_W), lambda j: (0, j))],
        out_shape=[jax.ShapeDtypeStruct((n // GROUP_W, rows, GROUP_W), F32),
                   jax.ShapeDtypeStruct((d, n), BF16)],
        scratch_shapes=[pltpu.VMEM((rows, d), BF16)],
        compiler_params=pltpu.CompilerParams(dimension_semantics=("arbitrary",),
                                             vmem_limit_bytes=VMEM_LIMIT),
        name="proj_small",
    )(x, g, w)


def _forget_lower_bound(lbp):
    e = jnp.exp(lbp - jnp.max(lbp, axis=0, keepdims=True))
    return e[0:1] / jnp.sum(e, axis=0, keepdims=True)


def _cumsum_rows(g, tri_bf):
    g1 = g.astype(BF16)
    r1 = g - g1.astype(F32)
    g2 = r1.astype(BF16)
    g3 = (r1 - g2.astype(F32)).astype(BF16)
    dot = functools.partial(jnp.dot, preferred_element_type=F32)
    return dot(tri_bf, g1) + dot(tri_bf, g2) + dot(tri_bf, g3)


def _hgrn_chunk_local(q, hf, v, lb):
    c = q.shape[0]
    row = lax.broadcasted_iota(jnp.int32, (c, c), 0)
    col = lax.broadcasted_iota(jnp.int32, (c, c), 1)
    causal = row >= col
    f = lb + (1.0 - lb) * _sigmoid(hf)
    k = 1.0 - f
    b = _cumsum_rows(jnp.log(f), causal.astype(BF16))
    qe = (q * jnp.exp(b)).astype(BF16)
    ke = (k * jnp.exp(-b)).astype(BF16)
    a = jnp.where(causal, _dot_nt(qe, ke), 0.0)
    v_bf = v.astype(BF16)
    intra = jnp.dot(a.astype(BF16), v_bf, preferred_element_type=F32)
    b_last = b[c - 1:c]
    kd = (k * jnp.exp(b_last - b)).astype(BF16)
    return qe, intra, _dot_tn(v_bf, kd), jnp.exp(b_last)


HGRN_SLAB = 256
HGRN_PAIR = 128


def _hgrn_prompt_kernel(q_ref, f_ref, v_ref, qm_ref, fm_ref, vm_ref, lbp_ref, o_ref, s_ref,
                        b_ref, qe_ref, ke_ref, kd_ref, vb_ref, vt_ref, a_ref, ut_ref, dec_ref, st_ref):
    ck, slab, pair = HGRN_CHUNK, HGRN_SLAB, HGRN_PAIR
    seq = q_ref.shape[1]
    n_chunks = seq // ck
    dot = functools.partial(jnp.dot, preferred_element_type=F32)
    lb = _forget_lower_bound(lbp_ref[...])

    _, _, st, _ = _hgrn_chunk_local(qm_ref[...], fm_ref[...], vm_ref[...], lb)

    f = lb + (1.0 - lb) * _sigmoid(f_ref[0].astype(F32))
    g = jnp.log(f)
    g1 = g.astype(BF16)
    r1 = g - g1.astype(F32)
    g2 = r1.astype(BF16)
    g3 = (r1 - g2.astype(F32)).astype(BF16)
    gcat = jnp.concatenate([g1, g2, g3], axis=1)
    r = lax.broadcasted_iota(jnp.int32, (slab, slab), 0)
    c = lax.broadcasted_iota(jnp.int32, (slab, slab), 1)
    tri_bd = ((r // ck == c // ck) & (r >= c)).astype(BF16)
    for si in range(seq // slab):
        rows = slice(si * slab, (si + 1) * slab)
        t = dot(tri_bd, gcat[rows])
        b_ref[rows, :] = t[:, :HEAD_W] + t[:, HEAD_W:2 * HEAD_W] + t[:, 2 * HEAD_W:]

    k = 1.0 - f
    b = b_ref[...]
    qe_ref[...] = (q_ref[0].astype(F32) * jnp.exp(b)).astype(BF16)
    ke_ref[...] = (k * jnp.exp(-b)).astype(BF16)
    v = v_ref[0].astype(F32)
    vb_ref[...] = v.astype(BF16)
    vt_ref[...] = v.T.astype(BF16)
    for ci in range(n_chunks):
        rows = slice(ci * ck, (ci + 1) * ck)
        b_last = b_ref[(ci + 1) * ck - 1:(ci + 1) * ck, :]
        dec_ref[ci] = jnp.exp(b_last)
        kd_ref[rows, :] = (k[rows] * jnp.exp(b_last - b_ref[rows, :])).astype(BF16)

    rp = lax.broadcasted_iota(jnp.int32, (pair, pair), 0)
    cp = lax.broadcasted_iota(jnp.int32, (pair, pair), 1)
    keep = (rp // ck == cp // ck) & (rp >= cp)
    for pi in range(seq // pair):
        rows = slice(pi * pair, (pi + 1) * pair)
        a_ref[rows, :] = jnp.where(keep, _dot_nt(qe_ref[rows, :], ke_ref[rows, :]), 0.0).astype(BF16)

    for pi in range(seq // pair):
        rows = slice(pi * pair, (pi + 1) * pair)
        o_ref[0, rows, :] = dot(a_ref[rows, :], vb_ref[rows, :])
    for ci in range(n_chunks):
        rows = slice(ci * ck, (ci + 1) * ck)
        ut_ref[ci] = dot(vt_ref[:, rows], kd_ref[rows, :])

    for ci in range(n_chunks):
        st_ref[ci] = st.astype(BF16)
        st = st * dec_ref[ci] + ut_ref[ci]
    s_ref[0, 0] = st.T

    for ci in range(n_chunks):
        rows = slice(ci * ck, (ci + 1) * ck)
        o_ref[0, rows, :] += _dot_nt(qe_ref[rows, :], st_ref[ci])


def _hgrn_prompt(z_lo, z_small, hgrn_lb, batch, seq, heads):
    zp3 = z_lo.reshape(z_lo.shape[0], batch, seq, GROUP_W)
    big = lambda g: pl.BlockSpec((None, 1, seq, HEAD_W), lambda b, h, g=g: (LO_SLOT[g], b, 0, h))
    meta = lambda g: pl.BlockSpec((None, N_META, HEAD_W), lambda b, h, g=g: (g, 0, h))
    return pl.pallas_call(
        _hgrn_prompt_kernel,
        grid=(batch, heads),
        in_specs=[big(G_HQ), big(G_HF), big(G_HI), meta(G_HQ), meta(G_HF), meta(G_HI),
                  pl.BlockSpec((hgrn_lb.shape[0], HEAD_W), lambda b, h: (0, h))],
        out_specs=[pl.BlockSpec((1, seq, HEAD_W), lambda b, h: (b, 0, h)),
                   pl.BlockSpec((1, 1, HEAD_W, HEAD_W), lambda b, h: (b, h, 0, 0))],
        out_shape=[jax.ShapeDtypeStruct((batch, seq, heads * HEAD_W), F32),
                   jax.ShapeDtypeStruct((batch, heads, HEAD_W, HEAD_W), F32)],
        scratch_shapes=[pltpu.VMEM((seq, HEAD_W), F32),
                        pltpu.VMEM((seq, HEAD_W), BF16), pltpu.VMEM((seq, HEAD_W), BF16),
                        pltpu.VMEM((seq, HEAD_W), BF16), pltpu.VMEM((seq, HEAD_W), BF16),
                        pltpu.VMEM((HEAD_W, seq), BF16), pltpu.VMEM((seq, HGRN_PAIR), BF16),
                        pltpu.VMEM((seq // HGRN_CHUNK, HEAD_W, HEAD_W), F32),
                        pltpu.VMEM((seq // HGRN_CHUNK, 1, HEAD_W), F32),
                        pltpu.VMEM((seq // HGRN_CHUNK, HEAD_W, HEAD_W), BF16)],
        compiler_params=pltpu.CompilerParams(dimension_semantics=("arbitrary", "arbitrary"),
                                             vmem_limit_bytes=VMEM_LIMIT),
        name="hgrn_prompt",
    )(zp3, zp3, zp3, z_small, z_small, z_small, hgrn_lb)


def _diff_lambda(lp):
    s01 = jnp.sum(lp[0:1] * lp[1:2], axis=-1, keepdims=True)
    s23 = jnp.sum(lp[2:3] * lp[3:4], axis=-1, keepdims=True)
    return jnp.exp(s01) - jnp.exp(s23) + LAMBDA_INIT


def _bias_by_distance(dist, bias_of_bucket):
    out = jnp.where(dist >= BUCKET_THR[1], bias_of_bucket(1), bias_of_bucket(0))
    for bkt in range(2, N_BUCKETS):
        out = jnp.where(dist >= BUCKET_THR[bkt], bias_of_bucket(bkt), out)
    return out


ATT_TQ = 256
ATT_AHEAD = 2


def _colmax(s):
    return jnp.max(s.reshape(s.shape[0] // 8, 8, s.shape[1]), axis=0)


def _colsum(s):
    return jnp.sum(s.reshape(s.shape[0] // 8, 8, s.shape[1]), axis=0)


def _assemble_kv_block(main_ref, lead_refs, meta_ref, out_ref, first):
    lead = jnp.where(first, meta_ref[...], jnp.concatenate([r[0] for r in lead_refs], axis=0))
    rows = jnp.concatenate([lead, main_ref[0, :out_ref.shape[1] - N_META, :]], axis=0)
    out_ref[...] = rows.reshape(out_ref.shape)


def _attn_prompt_kernel(rb_ref, lp_ref, q_ref, k_ref, v_ref, km_ref, vm_ref,
                        ka_ref, kl0_ref, kl1_ref, kma_ref, va_ref, vl0_ref, vl1_ref, vma_ref,
                        o_ref, ko_ref, vo_ref,
                        kb_ref, vt_ref, vmt_ref, s_ref, p_ref, bdiag_ref, bnear_ref, bmeta_ref):
    h = pl.program_id(0)
    b = pl.program_id(1)
    _assemble_kv_block(ka_ref, (kl0_ref, kl1_ref), kma_ref, ko_ref, h == 0)
    _assemble_kv_block(va_ref, (vl0_ref, vl1_ref), vma_ref, vo_ref, h == 0)
    tq = ATT_TQ
    n_q = q_ref.shape[1] // tq
    log2e = 1.0 / math.log(2.0)
    bias_of_bucket = lambda bkt: rb_ref[bkt * rb_ref.shape[0] // N_BUCKETS + h] * log2e
    far_bias = bias_of_bucket(N_BUCKETS - 1)
    key = lax.broadcasted_iota(jnp.int32, (tq, tq), 0)
    qry = lax.broadcasted_iota(jnp.int32, (tq, tq), 1)

    @pl.when(b == 0)
    def _():
        bdiag_ref[...] = _bias_by_distance(qry - key, bias_of_bucket)
        bnear_ref[...] = _bias_by_distance(qry - key + tq, bias_of_bucket)
        keym = lax.broadcasted_iota(jnp.int32, (N_META, tq), 0)
        qrym = lax.broadcasted_iota(jnp.int32, (N_META, tq), 1)
        bmeta_ref[...] = _bias_by_distance(qrym + N_META - keym, bias_of_bucket)

    kb_ref[...] = k_ref[0].astype(BF16)
    vt_ref[...] = v_ref[0].T.astype(BF16)
    pad = jnp.zeros((HEAD_W - N_META, HEAD_W), F32)
    vmt_ref[...] = jnp.concatenate([vm_ref[...], pad], axis=0).T.astype(BF16)
    km_bf = km_ref[...].astype(BF16)
    lam = _diff_lambda(lp_ref[...])
    causal = qry >= key
    dot = functools.partial(jnp.dot, preferred_element_type=F32)

    def pass_a(qi):
        par = qi % (ATT_AHEAD + 1)
        q = q_ref[0, qi * tq:(qi + 1) * tq, :].astype(F32) * (AT_DQK ** -0.5 * log2e)
        lane = lax.broadcasted_iota(jnp.int32, q.shape, 1)
        qc = (jnp.where(lane < AT_DQK, q, 0.0).astype(BF16), jnp.where(lane >= AT_DQK, q, 0.0).astype(BF16))
        n_keys = (qi + 1) * tq
        out = []
        for c in range(2):
            s_ref[par, c, :n_keys, :] = _dot_nt(kb_ref[:n_keys, :], qc[c])
            sm = _dot_nt(km_bf, qc[c]) + (bmeta_ref[...] if qi == 0 else far_bias)
            m8 = _colmax(sm)
            for j in range(qi + 1):
                m8 = jnp.maximum(m8, _colmax(biased(s_ref[par, c, j * tq:(j + 1) * tq, :], j, qi))
                                 + (far_bias if j < qi - 1 else 0.0))
            out.append((sm, jnp.max(m8, axis=0, keepdims=True)))
        return out

    def biased(s, j, qi):
        if j == qi:
            return jnp.where(causal, s + bdiag_ref[...], NEG_INF)
        if j == qi - 1:
            return s + bnear_ref[...]
        return s

    def pass_b(qi, stats):
        par = qi % (ATT_AHEAD + 1)
        n_keys = (qi + 1) * tq
        o_c = []
        for c in range(2):
            sm, m = stats[c]
            m_far = m - far_bias
            pm = jnp.exp2(sm - m)
            l8 = _colsum(pm)
            for j in range(qi + 1):
                rows = slice(j * tq, (j + 1) * tq)
                p = jnp.exp2(biased(s_ref[par, c, rows, :], j, qi) - (m_far if j < qi - 1 else m))
                l8 = l8 + _colsum(p)
                p_ref[qi % 2, c, rows, :] = p.astype(BF16)
            acc = dot(vt_ref[:, :n_keys], p_ref[qi % 2, c, :n_keys, :]) + dot(vmt_ref[:, :N_META], pm.astype(BF16))
            o_c.append(acc / jnp.sum(l8, axis=0, keepdims=True))
        o_ref[0, qi * tq:(qi + 1) * tq, :] = (o_c[0] - lam * o_c[1]).T

    stats = [pass_a(qi) for qi in range(min(ATT_AHEAD, n_q))]
    for qi in range(n_q):
        if qi + ATT_AHEAD < n_q:
            stats.append(pass_a(qi + ATT_AHEAD))
        pass_b(qi, stats.pop(0))


def _attn_prompt(z_lo, z_kv, z_small, rel_bias, diff_lambda, batch, seq, heads):
    assert CONST_BIAS_DIST <= ATT_TQ + 1 and CONST_BIAS_DIST <= N_META + 1 + ATT_TQ - N_META
    zlo = z_lo.reshape(z_lo.shape[0], batch, seq, GROUP_W)
    zkv = z_kv.reshape(z_kv.shape[0], batch, seq, GROUP_W)
    tq = ATT_TQ
    full = lambda slot: pl.BlockSpec((None, 1, seq, HEAD_W), lambda h, b, slot=slot: (slot, b, 0, h))
    meta = lambda g: pl.BlockSpec((None, N_META, HEAD_W), lambda h, b, g=g: (g, 0, h))

    tokens = N_META + seq
    blk = -(-tokens // (8 * heads)) * 8
    assert N_META == 16 and (heads - 1) * blk < seq
    asm_main = lambda g: pl.BlockSpec((None, 1, blk, GROUP_W), lambda h, b, g=g: (KV_SLOT[g], b, h, 0))
    asm_lead = lambda g, piece: pl.BlockSpec(
        (None, 1, 8, GROUP_W),
        lambda h, b, g=g, piece=piece: (KV_SLOT[g], b, jnp.maximum(h * (blk // 8) - 2 + piece, 0), 0))
    asm_meta = lambda g: pl.BlockSpec((None, N_META, GROUP_W), lambda h, b, g=g: (g, 0, 0))
    asm_in = lambda g: [asm_main(g), asm_lead(g, 0), asm_lead(g, 1), asm_meta(g)]
    asm_out = pl.BlockSpec((1, blk, heads, HEAD_W), lambda h, b: (b, h, 0, 0))
    kv_shape = jax.ShapeDtypeStruct((batch, tokens, heads, HEAD_W), F32)

    return pl.pallas_call(
        _attn_prompt_kernel,
        grid=(heads, batch),
        in_specs=[pl.BlockSpec(memory_space=pltpu.SMEM),
                  pl.BlockSpec(diff_lambda.shape, lambda h, b: (0, 0)),
                  full(LO_SLOT[G_AQ]), full(KV_SLOT[G_AK]), full(KV_SLOT[G_AV]), meta(G_AK), meta(G_AV)]
                 + asm_in(G_AK) + asm_in(G_AV),
        out_specs=[pl.BlockSpec((1, seq, HEAD_W), lambda h, b: (b, 0, h)), asm_out, asm_out],
        out_shape=[jax.ShapeDtypeStruct((batch, seq, heads * HEAD_W), F32), kv_shape, kv_shape],
        scratch_shapes=[pltpu.VMEM((seq, HEAD_W), BF16), pltpu.VMEM((HEAD_W, seq), BF16),
                        pltpu.VMEM((HEAD_W, HEAD_W), BF16),
                        pltpu.VMEM((ATT_AHEAD + 1, 2, seq, tq), F32), pltpu.VMEM((2, 2, seq, tq), BF16),
                        pltpu.VMEM((tq, tq), F32), pltpu.VMEM((tq, tq), F32),
                        pltpu.VMEM((N_META, tq), F32)],
        compiler_params=pltpu.CompilerParams(dimension_semantics=("arbitrary", "arbitrary"),
                                             vmem_limit_bytes=VMEM_LIMIT),
        name="attn_prompt",
    )(rel_bias.reshape(-1), diff_lambda, zlo, zkv, zkv, z_small, z_small,
      zkv, zkv, zkv, z_small, zkv, zkv, zkv, z_small)


def _head_rmsnorm(o, g):
    parts = []
    for hh in range(o.shape[1] // HEAD_W):
        oh = o[:, hh * HEAD_W:(hh + 1) * HEAD_W]
        ms = jnp.mean(oh * oh, axis=-1, keepdims=True)
        parts.append(oh * lax.rsqrt(ms + RMS_EPS) * g)
    return jnp.concatenate(parts, axis=1)


MERGE_SUB = 256


def _merge_kernel(ohg_ref, hg_ref, oat_ref, ag_ref, x_ref, gh_ref, ga_ref, w_ref, gf_ref, y_ref):
    tm = x_ref.shape[0]
    sub = min(MERGE_SUB, tm)
    subs = [slice(s, s + sub) for s in range(0, tm, sub)]
    mixed = []
    for r in subs:
        mix_h = _head_rmsnorm(ohg_ref[r, :], gh_ref[...]) * _silu(hg_ref[r, :].astype(F32))
        mix_a = (_head_rmsnorm(oat_ref[r, :], ga_ref[...]) * (1.0 - LAMBDA_INIT)
                 * _silu(ag_ref[r, :].astype(F32)))
        mixed.append(jnp.concatenate([mix_h, mix_a], axis=1).astype(BF16))
    proj = [jnp.dot(m, w_ref[...], preferred_element_type=F32) for m in mixed]
    for r, p in zip(subs, proj):
        hres = x_ref[r, :] + p
        ms = jnp.mean(hres * hres, axis=-1, keepdims=True)
        y_ref[r, :] = hres * lax.rsqrt(ms + RMS_EPS) * gf_ref[...]


def _merge(o_hg, o_at, z, slot_hg, slot_ag, x, g_hg, g_at, w_out_bf, g_final, tm, heads):
    rows, d = x.shape
    gw = heads * HEAD_W
    row_blk = lambda w: pl.BlockSpec((tm, w), lambda i: (i, 0))
    gate = lambda slot: pl.BlockSpec((None, tm, gw), lambda i, slot=slot: (slot, i, 0))
    const = lambda shape: pl.BlockSpec(shape, lambda i: (0, 0))
    return pl.pallas_call(
        _merge_kernel,
        grid=(rows // tm,),
        in_specs=[row_blk(gw), gate(slot_hg), row_blk(gw), gate(slot_ag), row_blk(d),
                  const((1, HEAD_W)), const((1, HEAD_W)),
                  pl.BlockSpec(w_out_bf.shape, lambda i: (0, 0), pipeline_mode=pl.Buffered(1)),
                  const((1, d))],
        out_specs=row_blk(d),
        out_shape=jax.ShapeDtypeStruct((rows, d), F32),
        compiler_params=pltpu.CompilerParams(dimension_semantics=("arbitrary",),
                                             vmem_limit_bytes=VMEM_LIMIT),
        name="merge",
    )(o_hg, z, o_at, z, x, g_hg, g_at, w_out_bf, g_final)


def _transpose_rows_to_cols(x):
    pad = jnp.zeros((HEAD_W - x.shape[0], HEAD_W), F32)
    return jnp.concatenate([x, pad], axis=0).T


def _hgrn_step_kernel(q_ref, f_ref, v_ref, lbp_ref, s_ref, o_ref, sn_ref):
    lb = _forget_lower_bound(lbp_ref[...])
    f = lb + (1.0 - lb) * _sigmoid(f_ref[...])
    k = 1.0 - f
    b = jnp.log(f)
    qe = q_ref[...] * jnp.exp(b)
    ke = k * jnp.exp(-b)
    a = jnp.sum(qe * ke, axis=-1, keepdims=True)
    v = v_ref[...]
    decay_t = _transpose_rows_to_cols(jnp.exp(b))
    kd_t = _transpose_rows_to_cols(k * jnp.exp(b - b))
    qe_t = _transpose_rows_to_cols(qe)
    for i in range(q_ref.shape[0]):
        s = s_ref[0, i, 0]
        vi = v[i:i + 1]
        o_ref[i:i + 1, :] = jnp.sum(qe_t[:, i:i + 1] * s, axis=0, keepdims=True) + a[i:i + 1] * vi
        sn_ref[0, i, 0] = decay_t[:, i:i + 1] * s + kd_t[:, i:i + 1] * vi


def _hgrn_step(z_samp, hgrn_lb, state, heads):
    db = z_samp.shape[1]
    col = lambda g: pl.BlockSpec((None, db, HEAD_W), lambda h, g=g: (g, 0, h))
    st_spec = pl.BlockSpec((1, db, 1, HEAD_W, HEAD_W), lambda h: (0, 0, h, 0, 0))
    return pl.pallas_call(
        _hgrn_step_kernel,
        grid=(heads,),
        in_specs=[col(G_HQ), col(G_HF), col(G_HI),
                  pl.BlockSpec((hgrn_lb.shape[0], HEAD_W), lambda h: (0, h)), st_spec],
        out_specs=[pl.BlockSpec((db, HEAD_W), lambda h: (0, h)), st_spec],
        out_shape=[jax.ShapeDtypeStruct((db, heads * HEAD_W), F32),
                   jax.ShapeDtypeStruct(state.shape, F32)],
        compiler_params=pltpu.CompilerParams(dimension_semantics=("arbitrary",),
                                             vmem_limit_bytes=VMEM_LIMIT),
        name="hgrn_step",
    )(z_samp, z_samp, z_samp, hgrn_lb, state)


DEC_PAGES_PER_STEP = 8
DEC_RING = 3


def _page_copies(pt_ref, ck_ref, cv_ref, kbuf_ref, vbuf_ref, sem_ref, step, n_steps):
    npp = DEC_PAGES_PER_STEP
    slot = step % DEC_RING
    seq = step // n_steps
    first_page = (step % n_steps) * npp
    copies = []
    for p in range(npp):
        phys = pt_ref[seq, first_page + p]
        copies.append(pltpu.make_async_copy(ck_ref.at[phys], kbuf_ref.at[slot, p], sem_ref.at[slot]))
        copies.append(pltpu.make_async_copy(cv_ref.at[phys], vbuf_ref.at[slot, p], sem_ref.at[slot]))
    return copies


def _proj_decode_kernel(pt_ref, x_ref, g_ref, w_ref, rbt_ref, lp_ref, q_ref, kn_ref, vn_ref, ck_ref, cv_ref,
                        z_ref, zkv_ref, o_ref, xn_ref, ztile_ref, qs_ref, m_ref, l_ref, acc_ref, blast_ref,
                        kbuf_ref, vbuf_ref, sem_ref, *, heads, page, n_steps):
    npp = DEC_PAGES_PER_STEP
    step = pl.program_id(0) * pl.num_programs(1) + pl.program_id(1)
    total_steps = pl.num_programs(0) * pl.num_programs(1)
    copies_for = lambda s: _page_copies(pt_ref, ck_ref, cv_ref, kbuf_ref, vbuf_ref, sem_ref, s, n_steps)

    @pl.when(step == 0)
    def _():
        for s in range(DEC_RING - 1):
            for cp in copies_for(s):
                cp.start()

    @pl.when(step + (DEC_RING - 1) < total_steps)
    def _():
        for cp in copies_for(step + (DEC_RING - 1)):
            cp.start()

    @pl.when(pl.program_id(1) == 0)
    def _():
        x = x_ref[...]
        ms = jnp.mean(x * x, axis=-1, keepdims=True)
        xn_ref[...] = (x * lax.rsqrt(ms + RMS_EPS) * g_ref[...]).astype(BF16)

    for cp in copies_for(step):
        cp.wait()
    slot = step % DEC_RING
    k_refs = [kbuf_ref.at[slot, p] for p in range(npp)]
    v_refs = [vbuf_ref.at[slot, p] for p in range(npp)]

    j = pl.program_id(1) % n_steps
    rows = 2 * heads
    lanes = page * heads
    rbt = jnp.concatenate([rbt_ref[...], rbt_ref[...]], axis=0)
    far_bias = rbt[:, N_BUCKETS - 1:N_BUCKETS]
    r_id = lax.broadcasted_iota(jnp.int32, (rows, lanes), 0)
    l_id = lax.broadcasted_iota(jnp.int32, (rows, lanes), 1)
    own_head = l_id % heads == r_id % heads

    def stacked(x):
        lane = lax.broadcasted_iota(jnp.int32, x.shape, 1)
        return jnp.concatenate([jnp.where(lane < AT_DQK, x, 0.0), jnp.where(lane >= AT_DQK, x, 0.0)], axis=0)

    @pl.when(j == 0)
    def _():
        qs_ref[...] = stacked(q_ref[0] * (AT_DQK ** -0.5)).astype(BF16)
        m_ref[...] = jnp.full(m_ref.shape, NEG_INF, F32)
        l_ref[...] = jnp.zeros(l_ref.shape, F32)
        acc_ref[...] = jnp.zeros(acc_ref.shape, F32)
        blast_ref[...] = _bias_by_distance(page - l_id // heads, lambda bkt: rbt[:, bkt:bkt + 1])

    qs = qs_ref[...]
    s_parts = []
    for i in range(npp):
        s_i = _dot_nt(qs, k_refs[i][...].astype(BF16))
        if i == npp - 1:
            s_i = s_i + jnp.where(j == n_steps - 1, blast_ref[...], far_bias)
        else:
            s_i = s_i + far_bias
        s_parts.append(jnp.where(own_head, s_i, NEG_INF))

    m_old = m_ref[...]
    m_blk = s_parts[0]
    for s_i in s_parts[1:]:
        m_blk = jnp.maximum(m_blk, s_i)
    m_new = jnp.maximum(m_old, jnp.max(m_blk, axis=-1, keepdims=True))
    alpha = jnp.exp(m_old - m_new)

    ztile_ref[...] = jnp.dot(xn_ref[...], w_ref[...], preferred_element_type=F32)

    l_blk = jnp.zeros((rows, lanes), F32)
    pv = jnp.zeros(acc_ref.shape, F32)
    for i in range(npp):
        p_i = jnp.exp(s_parts[i] - m_new)
        l_blk = l_blk + p_i
        pv = pv + jnp.dot(p_i.astype(BF16), v_refs[i][...].astype(BF16), preferred_element_type=F32)
    l_ref[...] = alpha * l_ref[...] + jnp.sum(l_blk, axis=-1, keepdims=True)
    acc_ref[...] = alpha * acc_ref[...] + pv
    m_ref[...] = m_new

    @pl.when(j == n_steps - 1)
    def _():
        kn = jnp.concatenate([kn_ref[0], kn_ref[0]], axis=0)
        vn = jnp.concatenate([vn_ref[0], vn_ref[0]], axis=0)
        s_new = jnp.sum(qs.astype(F32) * kn, axis=-1, keepdims=True) + rbt[:, 0:1]
        m_old = m_ref[...]
        m_fin = jnp.maximum(m_old, s_new)
        alpha = jnp.exp(m_old - m_fin)
        p_new = jnp.exp(s_new - m_fin)
        l_fin = alpha * l_ref[...] + p_new
        o = (alpha * acc_ref[...] + p_new * vn) / l_fin
        o_ref[0] = o[:heads] - _diff_lambda(lp_ref[...]) * o[heads:]

    group = pl.program_id(1) // (GROUP_W // z_ref.shape[1])
    is_kv = (group == G_AK) | (group == G_AV)

    @pl.when(is_kv)
    def _():
        zkv_ref[...] = ztile_ref[...]

    @pl.when(jnp.logical_not(is_kv))
    def _():
        z_ref[...] = ztile_ref[...].astype(BF16)


PD_TM = 1024
PD_TN = 256


def _proj_decode(x, g, w_bf, q_s, k_s, v_s, cache_k, cache_v, page_table, rel_bias, diff_lambda):
    assert (G_AK, G_AV) == (G_AQ + 1, G_AQ + 2)
    rows, d = x.shape
    n = w_bf.shape[1]
    db, n_pages = page_table.shape
    n_phys, page, heads, _ = cache_k.shape
    npp = DEC_PAGES_PER_STEP
    n_steps = n_pages // npp
    grid = (rows // PD_TM, n // PD_TN)
    seq_per_row_tile = grid[1] // n_steps
    tiles_per_group = GROUP_W // PD_TN
    assert CONST_BIAS_DIST <= page + 1 and n_pages % npp == 0 and GROUP_W % PD_TN == 0
    assert rows % PD_TM == 0 and n % PD_TN == 0 and grid[1] % n_steps == 0 and grid[0] * seq_per_row_tile == db
    ck = cache_k.reshape(n_phys, page * heads, HEAD_W)
    cv = cache_v.reshape(n_phys, page * heads, HEAD_W)
    seq_of = lambda i, t: i * seq_per_row_tile + t // n_steps
    vec = pl.BlockSpec((1, heads, HEAD_W), lambda i, t, pt: (seq_of(i, t), 0, 0))
    hbm = pl.BlockSpec(memory_space=pl.ANY)
    srows = 2 * heads

    last = tiles_per_group - 1

    def lo_block(i, t, pt):
        grp, col = t // tiles_per_group, t % tiles_per_group
        kv = (grp == G_AK) | (grp == G_AV)
        slot = jnp.where(grp < G_AK, grp, jnp.where(grp > G_AV, grp - 2, G_AK - 1))
        return slot, i, jnp.where(kv, last, col)

    def kv_block(i, t, pt):
        grp, col = t // tiles_per_group, t % tiles_per_group
        slot = jnp.clip(grp - G_AK, 0, 1)
        return slot, i, jnp.where(grp < G_AK, 0, jnp.where(grp > G_AV, last, col))

    kernel = functools.partial(_proj_decode_kernel, heads=heads, page=page, n_steps=n_steps)
    return pl.pallas_call(
        kernel,
        grid_spec=pltpu.PrefetchScalarGridSpec(
            num_scalar_prefetch=1,
            grid=grid,
            in_specs=[pl.BlockSpec((PD_TM, d), lambda i, t, pt: (i, 0)),
                      pl.BlockSpec((1, d), lambda i, t, pt: (0, 0)),
                      pl.BlockSpec((d, PD_TN), lambda i, t, pt: (0, t)),
                      pl.BlockSpec((heads, N_BUCKETS), lambda i, t, pt: (0, 0)),
                      pl.BlockSpec(diff_lambda.shape, lambda i, t, pt: (0, 0)),
                      vec, vec, vec, hbm, hbm],
            out_specs=[pl.BlockSpec((None, PD_TM, PD_TN), lo_block),
                       pl.BlockSpec((None, PD_TM, PD_TN), kv_block),
                       pl.BlockSpec((1, heads, HEAD_W), lambda i, t, pt: (seq_of(i, t), 0, 0))],
            scratch_shapes=[pltpu.VMEM((PD_TM, d), BF16), pltpu.VMEM((PD_TM, PD_TN), F32),
                            pltpu.VMEM((srows, HEAD_W), BF16), pltpu.VMEM((srows, 1), F32),
                            pltpu.VMEM((srows, 1), F32), pltpu.VMEM((srows, HEAD_W), F32),
                            pltpu.VMEM((srows, page * heads), F32),
                            pltpu.VMEM((DEC_RING, npp, page * heads, HEAD_W), F32),
                            pltpu.VMEM((DEC_RING, npp, page * heads, HEAD_W), F32),
                            pltpu.SemaphoreType.DMA((DEC_RING,))]),
        out_shape=[jax.ShapeDtypeStruct((n // GROUP_W - 2, rows, GROUP_W), BF16),
                   jax.ShapeDtypeStruct((2, rows, GROUP_W), F32),
                   jax.ShapeDtypeStruct((db, heads, HEAD_W), F32)],
        compiler_params=pltpu.CompilerParams(dimension_semantics=("arbitrary", "arbitrary"),
                                             vmem_limit_bytes=VMEM_LIMIT),
        name="proj_decode",
    )(page_table, x, g, w_bf, rel_bias.T, diff_lambda, q_s, k_s, v_s, ck, cv)


def kernel(x_prompt, x_sample, cache_k, cache_v, state_hgrn, page_table, meta_tokens, rel_bias, hgrn_lb,
           norm_g, w_in, hgrn_norm_g, diff_norm_g, diff_lambda, w_out, final_norm_g):
    batch, seq, d = x_prompt.shape
    db = x_sample.shape[0]
    heads = cache_k.shape[3]
    gw = heads * HEAD_W
    assert norm_g.shape[0] == 1 and x_sample.shape[1] == 1, "one layer, one sample token per sequence"
    assert gw == GROUP_W

    w_out_bf = w_out[0].astype(BF16)
    g_in = norm_g[0].reshape(1, d)
    g_final = final_norm_g.reshape(1, d)
    g_hg = hgrn_norm_g[0].reshape(1, HEAD_W)
    g_at = diff_norm_g[0].reshape(1, HEAD_W)
    lam_p = diff_lambda[0]

    xp = x_prompt.reshape(batch * seq, d)
    xs = x_sample.reshape(db, d)
    small = jnp.concatenate([meta_tokens.astype(F32), xs], axis=0)

    z_small, w_in_bf = _proj_small(small, g_in, w_in[0])
    z_samp = z_small[:, N_META:]
    samp_heads = lambda g: z_samp[g].reshape(db, heads, HEAD_W)
    q_s, k_s, v_s = samp_heads(G_AQ), samp_heads(G_AK), samp_heads(G_AV)

    z_lo, z_kv, o_at_s = _proj_decode(xp, g_in, w_in_bf, q_s, k_s, v_s, cache_k[0], cache_v[0], page_table,
                                      rel_bias, lam_p)

    o_hg_p, s_p = _hgrn_prompt(z_lo, z_small, hgrn_lb, batch, seq, heads)
    o_at_p, k_p, v_p = _attn_prompt(z_lo, z_kv, z_small, rel_bias, lam_p, batch, seq, heads)
    y_p = _merge(o_hg_p.reshape(batch * seq, gw), o_at_p.reshape(batch * seq, gw), z_lo, LO_SLOT[G_HG],
                 LO_SLOT[G_AG], xp, g_hg, g_at, w_out_bf, g_final, tm=512, heads=heads)

    o_hg_s, s_s = _hgrn_step(z_samp, hgrn_lb, state_hgrn, heads)
    y_s = _merge(o_hg_s, o_at_s.reshape(db, gw), z_samp, G_HG, G_AG, xs, g_hg, g_at, w_out_bf, g_final,
                 tm=db, heads=heads)

    return (y_p.reshape(batch, seq, d), y_s.reshape(db, 1, d), k_p[None], v_p[None],
            s_p[None], k_s.reshape(1, db, 1, heads, HEAD_W), v_s.reshape(1, db, 1, heads, HEAD_W), s_s)
```

```python
import functools
import math

import numpy as np
import jax
import jax.numpy as jnp
from jax import lax
from jax.experimental import pallas as pl
from jax.experimental.pallas import tpu as pltpu

F32 = jnp.float32
BF16 = jnp.bfloat16

N_META = 16
HEAD_W = 128
AT_DQK = 64
N_BUCKETS = 32
MAX_DISTANCE = 128
MAX_EXACT = N_BUCKETS // 2
HGRN_CHUNK = 64
RMS_EPS = 1e-6
NEG_INF = -1e30
LAMBDA_INIT = 0.8 - 0.6 * math.exp(-0.3 * 0)
G_HQ, G_HF, G_HI, G_HG, G_AQ, G_AK, G_AV, G_AG = range(8)
KV_SLOT = {G_AK: 0, G_AV: 1}
LO_SLOT = {g: (g if g < G_AK else g - 2) for g in range(8) if g not in KV_SLOT}

VMEM_LIMIT = 56 * 1024 * 1024


def _bucket_thresholds():
    n = np.arange(0, 2 * MAX_DISTANCE)
    nl = np.maximum(n, MAX_EXACT).astype(np.float32)
    large = MAX_EXACT + (np.log(nl / np.float32(MAX_EXACT)) / np.float32(math.log(MAX_DISTANCE / MAX_EXACT))
                         * np.float32(N_BUCKETS - MAX_EXACT)).astype(np.int32)
    bucket = np.where(n < MAX_EXACT, n, np.minimum(large, N_BUCKETS - 1))
    return [int(np.argmax(bucket >= b)) for b in range(N_BUCKETS)]


BUCKET_THR = _bucket_thresholds()
CONST_BIAS_DIST = BUCKET_THR[N_BUCKETS - 1]


def _sigmoid(x):
    return 1.0 / (1.0 + jnp.exp(-x))


def _silu(x):
    return x * _sigmoid(x)


def _dot_nt(a, b):
    return lax.dot_general(a, b, (((1,), (1,)), ((), ())), preferred_element_type=F32)


def _dot_tn(a, b):
    return lax.dot_general(a, b, (((0,), (0,)), ((), ())), preferred_element_type=F32)


GROUP_W = 1024


def _proj_small_kernel(x_ref, g_ref, w_ref, z_ref, wb_ref, xn_ref):
    @pl.when(pl.program_id(0) == 0)
    def _():
        x = x_ref[...]
        ms = jnp.mean(x * x, axis=-1, keepdims=True)
        xn_ref[...] = (x * lax.rsqrt(ms + RMS_EPS) * g_ref[...]).astype(BF16)

    w_bf = w_ref[...].astype(BF16)
    wb_ref[...] = w_bf
    z_ref[...] = jnp.dot(xn_ref[...], w_bf, preferred_element_type=F32)


def _proj_small(x, g, w):
    rows, d = x.shape
    n = w.shape[1]
    return pl.pallas_call(
        _proj_small_kernel,
        grid=(n // GROUP_W,),
        in_specs=[pl.BlockSpec((rows, d), lambda j: (0, 0)),
                  pl.BlockSpec((1, d), lambda j: (0, 0)),
                  pl.BlockSpec((d, GROUP_W), lambda j: (0, j))],
        out_specs=[pl.BlockSpec((None, rows, GROUP_W), lambda j: (j, 0, 0)),
                   pl.BlockSpec((d, GROUP_W), lambda j: (0, j))],
        out_shape=[jax.ShapeDtypeStruct((n // GROUP_W, rows, GROUP_W), F32),
                   jax.ShapeDtypeStruct((d, n), BF16)],
        scratch_shapes=[pltpu.VMEM((rows, d), BF16)],
        compiler_params=pltpu.CompilerParams(dimension_semantics=("arbitrary",),
                                             vmem_limit_bytes=VMEM_LIMIT),
        name="proj_small",
    )(x, g, w)


def _forget_lower_bound(lbp):
    e = jnp.exp(lbp - jnp.max(lbp, axis=0, keepdims=True))
    return e[0:1] / jnp.sum(e, axis=0, keepdims=True)


def _cumsum_rows(g, tri_bf):
    g1 = g.astype(BF16)
    r1 = g - g1.astype(F32)
    g2 = r1.astype(BF16)
    g3 = (r1 - g2.astype(F32)).astype(BF16)
    dot = functools.partial(jnp.dot, preferred_element_type=F32)
    return dot(tri_bf, g1) + dot(tri_bf, g2) + dot(tri_bf, g3)


def _hgrn_chunk_local(q, hf, v, lb):
    c = q.shape[0]
    row = lax.broadcasted_iota(jnp.int32, (c, c), 0)
    col = lax.broadcasted_iota(jnp.int32, (c, c), 1)
    causal = row >= col
    f = lb + (1.0 - lb) * _sigmoid(hf)
    k = 1.0 - f
    b = _cumsum_rows(jnp.log(f), causal.astype(BF16))
    qe = (q * jnp.exp(b)).astype(BF16)
    ke = (k * jnp.exp(-b)).astype(BF16)
    a = jnp.where(causal, _dot_nt(qe, ke), 0.0)
    v_bf = v.astype(BF16)
    intra = jnp.dot(a.astype(BF16), v_bf, preferred_element_type=F32)
    b_last = b[c - 1:c]
    kd = (k * jnp.exp(b_last - b)).astype(BF16)
    return qe, intra, _dot_tn(v_bf, kd), jnp.exp(b_last)


HGRN_SLAB = 256
HGRN_PAIR = 128


def _hgrn_prompt_kernel(q_ref, f_ref, v_ref, qm_ref, fm_ref, vm_ref, lbp_ref, o_ref, s_ref,
                        b_ref, qe_ref, ke_ref, kd_ref, vb_ref, vt_ref, a_ref, ut_ref, dec_ref, st_ref):
    ck, slab, pair = HGRN_CHUNK, HGRN_SLAB, HGRN_PAIR
    seq = q_ref.shape[1]
    n_chunks = seq // ck
    dot = functools.partial(jnp.dot, preferred_element_type=F32)
    lb = _forget_lower_bound(lbp_ref[...])

    _, _, st, _ = _hgrn_chunk_local(qm_ref[...], fm_ref[...], vm_ref[...], lb)

    f = lb + (1.0 - lb) * _sigmoid(f_ref[0].astype(F32))
    g = jnp.log(f)
    g1 = g.astype(BF16)
    r1 = g - g1.astype(F32)
    g2 = r1.astype(BF16)
    g3 = (r1 - g2.astype(F32)).astype(BF16)
    gcat = jnp.concatenate([g1, g2, g3], axis=1)
    r = lax.broadcasted_iota(jnp.int32, (slab, slab), 0)
    c = lax.broadcasted_iota(jnp.int32, (slab, slab), 1)
    tri_bd = ((r // ck == c // ck) & (r >= c)).astype(BF16)
    for si in range(seq // slab):
        rows = slice(si * slab, (si + 1) * slab)
        t = dot(tri_bd, gcat[rows])
        b_ref[rows, :] = t[:, :HEAD_W] + t[:, HEAD_W:2 * HEAD_W] + t[:, 2 * HEAD_W:]

    k = 1.0 - f
    b = b_ref[...]
    qe_ref[...] = (q_ref[0].astype(F32) * jnp.exp(b)).astype(BF16)
    ke_ref[...] = (k * jnp.exp(-b)).astype(BF16)
    v = v_ref[0].astype(F32)
    vb_ref[...] = v.astype(BF16)
    vt_ref[...] = v.T.astype(BF16)
    for ci in range(n_chunks):
        rows = slice(ci * ck, (ci + 1) * ck)
        b_last = b_ref[(ci + 1) * ck - 1:(ci + 1) * ck, :]
        dec_ref[ci] = jnp.exp(b_last)
        kd_ref[rows, :] = (k[rows] * jnp.exp(b_last - b_ref[rows, :])).astype(BF16)

    rp = lax.broadcasted_iota(jnp.int32, (pair, pair), 0)
    cp = lax.broadcasted_iota(jnp.int32, (pair, pair), 1)
    keep = (rp // ck == cp // ck) & (rp >= cp)
    for pi in range(seq // pair):
        rows = slice(pi * pair, (pi + 1) * pair)
        a_ref[rows, :] = jnp.where(keep, _dot_nt(qe_ref[rows, :], ke_ref[rows, :]), 0.0).astype(BF16)

    for pi in range(seq // pair):
        rows = slice(pi * pair, (pi + 1) * pair)
        o_ref[0, rows, :] = dot(a_ref[rows, :], vb_ref[rows, :])
    for ci in range(n_chunks):
        rows = slice(ci * ck, (ci + 1) * ck)
        ut_ref[ci] = dot(vt_ref[:, rows], kd_ref[rows, :])

    for ci in range(n_chunks):
        st_ref[ci] = st.astype(BF16)
        st = st * dec_ref[ci] + ut_ref[ci]
    s_ref[0, 0] = st.T

    for ci in range(n_chunks):
        rows = slice(ci * ck, (ci + 1) * ck)
        o_ref[0, rows, :] += _dot_nt(qe_ref[rows, :], st_ref[ci])


def _hgrn_prompt(z_lo, z_small, hgrn_lb, batch, seq, heads):
    zp3 = z_lo.reshape(z_lo.shape[0], batch, seq, GROUP_W)
    big = lambda g: pl.BlockSpec((None, 1, seq, HEAD_W), lambda b, h, g=g: (LO_SLOT[g], b, 0, h))
    meta = lambda g: pl.BlockSpec((None, N_META, HEAD_W), lambda b, h, g=g: (g, 0, h))
    return pl.pallas_call(
        _hgrn_prompt_kernel,
        grid=(batch, heads),
        in_specs=[big(G_HQ), big(G_HF), big(G_HI), meta(G_HQ), meta(G_HF), meta(G_HI),
                  pl.BlockSpec((hgrn_lb.shape[0], HEAD_W), lambda b, h: (0, h))],
        out_specs=[pl.BlockSpec((1, seq, HEAD_W), lambda b, h: (b, 0, h)),
                   pl.BlockSpec((1, 1, HEAD_W, HEAD_W), lambda b, h: (b, h, 0, 0))],
        out_shape=[jax.ShapeDtypeStruct((batch, seq, heads * HEAD_W), F32),
                   jax.ShapeDtypeStruct((batch, heads, HEAD_W, HEAD_W), F32)],
        scratch_shapes=[pltpu.VMEM((seq, HEAD_W), F32),
                        pltpu.VMEM((seq, HEAD_W), BF16), pltpu.VMEM((seq, HEAD_W), BF16),
                        pltpu.VMEM((seq, HEAD_W), BF16), pltpu.VMEM((seq, HEAD_W), BF16),
                        pltpu.VMEM((HEAD_W, seq), BF16), pltpu.VMEM((seq, HGRN_PAIR), BF16),
                        pltpu.VMEM((seq // HGRN_CHUNK, HEAD_W, HEAD_W), F32),
                        pltpu.VMEM((seq // HGRN_CHUNK, 1, HEAD_W), F32),
                        pltpu.VMEM((seq // HGRN_CHUNK, HEAD_W, HEAD_W), BF16)],
        compiler_params=pltpu.CompilerParams(dimension_semantics=("arbitrary", "arbitrary"),
                                             vmem_limit_bytes=VMEM_LIMIT),
        name="hgrn_prompt",
    )(zp3, zp3, zp3, z_small, z_small, z_small, hgrn_lb)


def _diff_lambda(lp):
    s01 = jnp.sum(lp[0:1] * lp[1:2], axis=-1, keepdims=True)
    s23 = jnp.sum(lp[2:3] * lp[3:4], axis=-1, keepdims=True)
    return jnp.exp(s01) - jnp.exp(s23) + LAMBDA_INIT


def _bias_by_distance(dist, bias_of_bucket):
    out = jnp.where(dist >= BUCKET_THR[1], bias_of_bucket(1), bias_of_bucket(0))
    for bkt in range(2, N_BUCKETS):
        out = jnp.where(dist >= BUCKET_THR[bkt], bias_of_bucket(bkt), out)
    return out


ATT_TQ = 256
ATT_AHEAD = 2


def _colmax(s):
    return jnp.max(s.reshape(s.shape[0] // 8, 8, s.shape[1]), axis=0)


def _colsum(s):
    return jnp.sum(s.reshape(s.shape[0] // 8, 8, s.shape[1]), axis=0)


def _assemble_kv_block(main_ref, lead_refs, meta_ref, out_ref, first):
    lead = jnp.where(first, meta_ref[...], jnp.concatenate([r[0] for r in lead_refs], axis=0))
    rows = jnp.concatenate([lead, main_ref[0, :out_ref.shape[1] - N_META, :]], axis=0)
    out_ref[...] = rows.reshape(out_ref.shape)


def _attn_prompt_kernel(rb_ref, lp_ref, q_ref, k_ref, v_ref, km_ref, vm_ref,
                        ka_ref, kl0_ref, kl1_ref, kma_ref, va_ref, vl0_ref, vl1_ref, vma_ref,
                        o_ref, ko_ref, vo_ref,
                        kb_ref, vt_ref, vmt_ref, s_ref, p_ref, bdiag_ref, bnear_ref, bmeta_ref):
    h = pl.program_id(0)
    b = pl.program_id(1)
    _assemble_kv_block(ka_ref, (kl0_ref, kl1_ref), kma_ref, ko_ref, h == 0)
    _assemble_kv_block(va_ref, (vl0_ref, vl1_ref), vma_ref, vo_ref, h == 0)
    tq = ATT_TQ
    n_q = q_ref.shape[1] // tq
    log2e = 1.0 / math.log(2.0)
    bias_of_bucket = lambda bkt: rb_ref[bkt * rb_ref.shape[0] // N_BUCKETS + h] * log2e
    far_bias = bias_of_bucket(N_BUCKETS - 1)
    key = lax.broadcasted_iota(jnp.int32, (tq, tq), 0)
    qry = lax.broadcasted_iota(jnp.int32, (tq, tq), 1)

    @pl.when(b == 0)
    def _():
        bdiag_ref[...] = _bias_by_distance(qry - key, bias_of_bucket)
        bnear_ref[...] = _bias_by_distance(qry - key + tq, bias_of_bucket)
        keym = lax.broadcasted_iota(jnp.int32, (N_META, tq), 0)
        qrym = lax.broadcasted_iota(jnp.int32, (N_META, tq), 1)
        bmeta_ref[...] = _bias_by_distance(qrym + N_META - keym, bias_of_bucket)

    kb_ref[...] = k_ref[0].astype(BF16)
    vt_ref[...] = v_ref[0].T.astype(BF16)
    pad = jnp.zeros((HEAD_W - N_META, HEAD_W), F32)
    vmt_ref[...] = jnp.concatenate([vm_ref[...], pad], axis=0).T.astype(BF16)
    km_bf = km_ref[...].astype(BF16)
    lam = _diff_lambda(lp_ref[...])
    causal = qry >= key
    dot = functools.partial(jnp.dot, preferred_element_type=F32)

    def pass_a(qi):
        par = qi % (ATT_AHEAD + 1)
        q = q_ref[0, qi * tq:(qi + 1) * tq, :].astype(F32) * (AT_DQK ** -0.5 * log2e)
        lane = lax.broadcasted_iota(jnp.int32, q.shape, 1)
        qc = (jnp.where(lane < AT_DQK, q, 0.0).astype(BF16), jnp.where(lane >= AT_DQK, q, 0.0).astype(BF16))
        n_keys = (qi + 1) * tq
        out = []
        for c in range(2):
            s_ref[par, c, :n_keys, :] = _dot_nt(kb_ref[:n_keys, :], qc[c])
            sm = _dot_nt(km_bf, qc[c]) + (bmeta_ref[...] if qi == 0 else far_bias)
            m8 = _colmax(sm)
            for j in range(qi + 1):
                m8 = jnp.maximum(m8, _colmax(biased(s_ref[par, c, j * tq:(j + 1) * tq, :], j, qi))
                                 + (far_bias if j < qi - 1 else 0.0))
            out.append((sm, jnp.max(m8, axis=0, keepdims=True)))
        return out

    def biased(s, j, qi):
        if j == qi:
            return jnp.where(causal, s + bdiag_ref[...], NEG_INF)
        if j == qi - 1:
            return s + bnear_ref[...]
        return s

    def pass_b(qi, stats):
        par = qi % (ATT_AHEAD + 1)
        n_keys = (qi + 1) * tq
        o_c = []
        for c in range(2):
            sm, m = stats[c]
            m_far = m - far_bias
            pm = jnp.exp2(sm - m)
            l8 = _colsum(pm)
            for j in range(qi + 1):
                rows = slice(j * tq, (j + 1) * tq)
                p = jnp.exp2(biased(s_ref[par, c, rows, :], j, qi) - (m_far if j < qi - 1 else m))
                l8 = l8 + _colsum(p)
                p_ref[qi % 2, c, rows, :] = p.astype(BF16)
            acc = dot(vt_ref[:, :n_keys], p_ref[qi % 2, c, :n_keys, :]) + dot(vmt_ref[:, :N_META], pm.astype(BF16))
            o_c.append(acc / jnp.sum(l8, axis=0, keepdims=True))
        o_ref[0, qi * tq:(qi + 1) * tq, :] = (o_c[0] - lam * o_c[1]).T

    stats = [pass_a(qi) for qi in range(min(ATT_AHEAD, n_q))]
    for qi in range(n_q):
        if qi + ATT_AHEAD < n_q:
            stats.append(pass_a(qi + ATT_AHEAD))
        pass_b(qi, stats.pop(0))


def _attn_prompt(z_lo, z_kv, z_small, rel_bias, diff_lambda, batch, seq, heads):
    assert CONST_BIAS_DIST <= ATT_TQ + 1 and CONST_BIAS_DIST <= N_META + 1 + ATT_TQ - N_META
    zlo = z_lo.reshape(z_lo.shape[0], batch, seq, GROUP_W)
    zkv = z_kv.reshape(z_kv.shape[0], batch, seq, GROUP_W)
    tq = ATT_TQ
    full = lambda slot: pl.BlockSpec((None, 1, seq, HEAD_W), lambda h, b, slot=slot: (slot, b, 0, h))
    meta = lambda g: pl.BlockSpec((None, N_META, HEAD_W), lambda h, b, g=g: (g, 0, h))

    tokens = N_META + seq
    blk = -(-tokens // (8 * heads)) * 8
    assert N_META == 16 and (heads - 1) * blk < seq
    asm_main = lambda g: pl.BlockSpec((None, 1, blk, GROUP_W), lambda h, b, g=g: (KV_SLOT[g], b, h, 0))
    asm_lead = lambda g, piece: pl.BlockSpec(
        (None, 1, 8, GROUP_W),
        lambda h, b, g=g, piece=piece: (KV_SLOT[g], b, jnp.maximum(h * (blk // 8) - 2 + piece, 0), 0))
    asm_meta = lambda g: pl.BlockSpec((None, N_META, GROUP_W), lambda h, b, g=g: (g, 0, 0))
    asm_in = lambda g: [asm_main(g), asm_lead(g, 0), asm_lead(g, 1), asm_meta(g)]
    asm_out = pl.BlockSpec((1, blk, heads, HEAD_W), lambda h, b: (b, h, 0, 0))
    kv_shape = jax.ShapeDtypeStruct((batch, tokens, heads, HEAD_W), F32)

    return pl.pallas_call(
        _attn_prompt_kernel,
        grid=(heads, batch),
        in_specs=[pl.BlockSpec(memory_space=pltpu.SMEM),
                  pl.BlockSpec(diff_lambda.shape, lambda h, b: (0, 0)),
                  full(LO_SLOT[G_AQ]), full(KV_SLOT[G_AK]), full(KV_SLOT[G_AV]), meta(G_AK), meta(G_AV)]
                 + asm_in(G_AK) + asm_in(G_AV),
        out_specs=[pl.BlockSpec((1, seq, HEAD_W), lambda h, b: (b, 0, h)), asm_out, asm_out],
        out_shape=[jax.ShapeDtypeStruct((batch, seq, heads * HEAD_W), F32), kv_shape, kv_shape],
        scratch_shapes=[pltpu.VMEM((seq, HEAD_W), BF16), pltpu.VMEM((HEAD_W, seq), BF16),
                        pltpu.VMEM((HEAD_W, HEAD_W), BF16),
                        pltpu.VMEM((ATT_AHEAD + 1, 2, seq, tq), F32), pltpu.VMEM((2, 2, seq, tq), BF16),
                        pltpu.VMEM((tq, tq), F32), pltpu.VMEM((tq, tq), F32),
                        pltpu.VMEM((N_META, tq), F32)],
        compiler_params=pltpu.CompilerParams(dimension_semantics=("arbitrary", "arbitrary"),
                                             vmem_limit_bytes=VMEM_LIMIT),
        name="attn_prompt",
    )(rel_bias.reshape(-1), diff_lambda, zlo, zkv, zkv, z_small, z_small,
      zkv, zkv, zkv, z_small, zkv, zkv, zkv, z_small)


def _head_rmsnorm(o, g):
    parts = []
    for hh in range(o.shape[1] // HEAD_W):
        oh = o[:, hh * HEAD_W:(hh + 1) * HEAD_W]
        ms = jnp.mean(oh * oh, axis=-1, keepdims=True)
        parts.append(oh * lax.rsqrt(ms + RMS_EPS) * g)
    return jnp.concatenate(parts, axis=1)


MERGE_SUB = 256


def _merge_kernel(ohg_ref, hg_ref, oat_ref, ag_ref, x_ref, gh_ref, ga_ref, w_ref, gf_ref, y_ref):
    tm = x_ref.shape[0]
    sub = min(MERGE_SUB, tm)
    subs = [slice(s, s + sub) for s in range(0, tm, sub)]
    mixed = []
    for r in subs:
        mix_h = _head_rmsnorm(ohg_ref[r, :], gh_ref[...]) * _silu(hg_ref[r, :].astype(F32))
        mix_a = (_head_rmsnorm(oat_ref[r, :], ga_ref[...]) * (1.0 - LAMBDA_INIT)
                 * _silu(ag_ref[r, :].astype(F32)))
        mixed.append(jnp.concatenate([mix_h, mix_a], axis=1).astype(BF16))
    proj = [jnp.dot(m, w_ref[...], preferred_element_type=F32) for m in mixed]
    for r, p in zip(subs, proj):
        hres = x_ref[r, :] + p
        ms = jnp.mean(hres * hres, axis=-1, keepdims=True)
        y_ref[r, :] = hres * lax.rsqrt(ms + RMS_EPS) * gf_ref[...]


def _merge(o_hg, o_at, z, slot_hg, slot_ag, x, g_hg, g_at, w_out_bf, g_final, tm, heads):
    rows, d = x.shape
    gw = heads * HEAD_W
    row_blk = lambda w: pl.BlockSpec((tm, w), lambda i: (i, 0))
    gate = lambda slot: pl.BlockSpec((None, tm, gw), lambda i, slot=slot: (slot, i, 0))
    const = lambda shape: pl.BlockSpec(shape, lambda i: (0, 0))
    return pl.pallas_call(
        _merge_kernel,
        grid=(rows // tm,),
        in_specs=[row_blk(gw), gate(slot_hg), row_blk(gw), gate(slot_ag), row_blk(d),
                  const((1, HEAD_W)), const((1, HEAD_W)),
                  pl.BlockSpec(w_out_bf.shape, lambda i: (0, 0), pipeline_mode=pl.Buffered(1)),
                  const((1, d))],
        out_specs=row_blk(d),
        out_shape=jax.ShapeDtypeStruct((rows, d), F32),
        compiler_params=pltpu.CompilerParams(dimension_semantics=("arbitrary",),
                                             vmem_limit_bytes=VMEM_LIMIT),
        name="merge",
    )(o_hg, z, o_at, z, x, g_hg, g_at, w_out_bf, g_final)


def _transpose_rows_to_cols(x):
    pad = jnp.zeros((HEAD_W - x.shape[0], HEAD_W), F32)
    return jnp.concatenate([x, pad], axis=0).T


def _hgrn_step_kernel(q_ref, f_ref, v_ref, lbp_ref, s_ref, w_ref, o_ref, sn_ref, wb_ref):
    wb_ref[...] = w_ref[...].astype(BF16)
    lb = _forget_lower_bound(lbp_ref[...])
    f = lb + (1.0 - lb) * _sigmoid(f_ref[...])
    v = v_ref[...]
    f_t = _transpose_rows_to_cols(f)
    q_t = _transpose_rows_to_cols(q_ref[...])
    for i in range(q_ref.shape[0]):
        fi = jnp.broadcast_to(f_t[:, i:i + 1], (HEAD_W, HEAD_W))
        s_new = fi * s_ref[0, i, 0] + (1.0 - fi) * v[i:i + 1]
        sn_ref[0, i, 0] = s_new
        o_ref[i:i + 1, :] = jnp.sum(q_t[:, i:i + 1] * s_new, axis=0, keepdims=True)


def _hgrn_step(z_samp, hgrn_lb, state, w_out, heads):
    db = z_samp.shape[1]
    d = w_out.shape[0]
    col = lambda g: pl.BlockSpec((None, db, HEAD_W), lambda h, g=g: (g, 0, h))
    st_spec = pl.BlockSpec((1, db, 1, HEAD_W, HEAD_W), lambda h: (0, 0, h, 0, 0))
    w_spec = pl.BlockSpec((d // heads, w_out.shape[1]), lambda h: (h, 0))
    return pl.pallas_call(
        _hgrn_step_kernel,
        grid=(heads,),
        in_specs=[col(G_HQ), col(G_HF), col(G_HI),
                  pl.BlockSpec((hgrn_lb.shape[0], HEAD_W), lambda h: (0, h)), st_spec, w_spec],
        out_specs=[pl.BlockSpec((db, HEAD_W), lambda h: (0, h)), st_spec, w_spec],
        out_shape=[jax.ShapeDtypeStruct((db, heads * HEAD_W), F32),
                   jax.ShapeDtypeStruct(state.shape, F32),
                   jax.ShapeDtypeStruct(w_out.shape, BF16)],
        compiler_params=pltpu.CompilerParams(dimension_semantics=("arbitrary",),
                                             vmem_limit_bytes=VMEM_LIMIT),
        name="hgrn_step",
    )(z_samp, z_samp, z_samp, hgrn_lb, state, w_out)


DEC_PAGES_PER_STEP = 8
DEC_RING = 3


def _page_copies(pt_ref, ck_ref, cv_ref, kbuf_ref, vbuf_ref, sem_ref, step, n_steps):
    npp = DEC_PAGES_PER_STEP
    slot = step % DEC_RING
    seq = step // n_steps
    first_page = (step % n_steps) * npp
    copies = []
    for p in range(npp):
        phys = pt_ref[seq, first_page + p]
        copies.append(pltpu.make_async_copy(ck_ref.at[phys], kbuf_ref.at[slot, p], sem_ref.at[slot]))
        copies.append(pltpu.make_async_copy(cv_ref.at[phys], vbuf_ref.at[slot, p], sem_ref.at[slot]))
    return copies


def _proj_decode_kernel(pt_ref, x_ref, g_ref, w_ref, rbt_ref, lp_ref, q_ref, kn_ref, vn_ref, ck_ref, cv_ref,
                        z_ref, zkv_ref, o_ref, xn_ref, ztile_ref, qs_ref, m_ref, l_ref, acc_ref, blast_ref,
                        kbuf_ref, vbuf_ref, sem_ref, *, heads, page, n_steps):
    npp = DEC_PAGES_PER_STEP
    step = pl.program_id(0) * pl.num_programs(1) + pl.program_id(1)
    total_steps = pl.num_programs(0) * pl.num_programs(1)
    copies_for = lambda s: _page_copies(pt_ref, ck_ref, cv_ref, kbuf_ref, vbuf_ref, sem_ref, s, n_steps)

    def start_all(copies):
        for n, cp in enumerate(copies):
            cp.start(priority=n % 2)

    @pl.when(step == 0)
    def _():
        for s in range(DEC_RING - 1):
            start_all(copies_for(s))

    @pl.when(step + (DEC_RING - 1) < total_steps)
    def _():
        start_all(copies_for(step + (DEC_RING - 1)))

    @pl.when(pl.program_id(1) == 0)
    def _():
        x = x_ref[...]
        ms = jnp.mean(x * x, axis=-1, keepdims=True)
        xn_ref[...] = (x * lax.rsqrt(ms + RMS_EPS) * g_ref[...]).astype(BF16)

    for cp in copies_for(step):
        cp.wait()
    slot = step % DEC_RING
    k_refs = [kbuf_ref.at[slot, p] for p in range(npp)]
    v_refs = [vbuf_ref.at[slot, p] for p in range(npp)]

    j = pl.program_id(1) % n_steps
    rows = 2 * heads
    lanes = page * heads
    rbt = jnp.concatenate([rbt_ref[...], rbt_ref[...]], axis=0)
    far_bias = rbt[:, N_BUCKETS - 1:N_BUCKETS]
    r_id = lax.broadcasted_iota(jnp.int32, (rows, lanes), 0)
    l_id = lax.broadcasted_iota(jnp.int32, (rows, lanes), 1)
    own_head = l_id % heads == r_id % heads

    def stacked(x):
        lane = lax.broadcasted_iota(jnp.int32, x.shape, 1)
        return jnp.concatenate([jnp.where(lane < AT_DQK, x, 0.0), jnp.where(lane >= AT_DQK, x, 0.0)], axis=0)

    @pl.when(j == 0)
    def _():
        qs_ref[...] = stacked(q_ref[0] * (AT_DQK ** -0.5)).astype(BF16)
        m_ref[...] = jnp.full(m_ref.shape, NEG_INF, F32)
        l_ref[...] = jnp.zeros(l_ref.shape, F32)
        acc_ref[...] = jnp.zeros(acc_ref.shape, F32)
        blast_ref[...] = _bias_by_distance(page - l_id // heads, lambda bkt: rbt[:, bkt:bkt + 1])

    qs = qs_ref[...]
    s_parts = []
    for i in range(npp):
        s_i = _dot_nt(qs, k_refs[i][...].astype(BF16))
        if i == npp - 1:
            s_i = s_i + jnp.where(j == n_steps - 1, blast_ref[...], far_bias)
        else:
            s_i = s_i + far_bias
        s_parts.append(jnp.where(own_head, s_i, NEG_INF))

    m_old = m_ref[...]
    m_blk = s_parts[0]
    for s_i in s_parts[1:]:
        m_blk = jnp.maximum(m_blk, s_i)
    m_new = jnp.maximum(m_old, jnp.max(m_blk, axis=-1, keepdims=True))
    alpha = jnp.exp(m_old - m_new)

    ztile_ref[...] = jnp.dot(xn_ref[...], w_ref[...], preferred_element_type=F32)

    l_blk = jnp.zeros((rows, lanes), F32)
    pv = jnp.zeros(acc_ref.shape, F32)
    for i in range(npp):
        p_i = jnp.exp(s_parts[i] - m_new)
        l_blk = l_blk + p_i
        pv = pv + jnp.dot(p_i.astype(BF16), v_refs[i][...].astype(BF16), preferred_element_type=F32)
    l_ref[...] = alpha * l_ref[...] + jnp.sum(l_blk, axis=-1, keepdims=True)
    acc_ref[...] = alpha * acc_ref[...] + pv
    m_ref[...] = m_new

    @pl.when(j == n_steps - 1)
    def _():
        kn = jnp.concatenate([kn_ref[0], kn_ref[0]], axis=0)
        vn = jnp.concatenate([vn_ref[0], vn_ref[0]], axis=0)
        s_new = jnp.sum(qs.astype(F32) * kn, axis=-1, keepdims=True) + rbt[:, 0:1]
        m_old = m_ref[...]
        m_fin = jnp.maximum(m_old, s_new)
        alpha = jnp.exp(m_old - m_fin)
        p_new = jnp.exp(s_new - m_fin)
        l_fin = alpha * l_ref[...] + p_new
        o = (alpha * acc_ref[...] + p_new * vn) / l_fin
        o_ref[0] = o[:heads] - _diff_lambda(lp_ref[...]) * o[heads:]

    group = pl.program_id(1) // (GROUP_W // z_ref.shape[1])
    is_kv = (group == G_AK) | (group == G_AV)

    @pl.when(is_kv)
    def _():
        zkv_ref[...] = ztile_ref[...]

    @pl.when(jnp.logical_not(is_kv))
    def _():
        z_ref[...] = ztile_ref[...].astype(BF16)


PD_TM = 1024
PD_TN = 256


def _proj_decode(x, g, w_bf, q_s, k_s, v_s, cache_k, cache_v, page_table, rel_bias, diff_lambda):
    assert (G_AK, G_AV) == (G_AQ + 1, G_AQ + 2)
    rows, d = x.shape
    n = w_bf.shape[1]
    db, n_pages = page_table.shape
    n_phys, page, heads, _ = cache_k.shape
    npp = DEC_PAGES_PER_STEP
    n_steps = n_pages // npp
    grid = (rows // PD_TM, n // PD_TN)
    seq_per_row_tile = grid[1] // n_steps
    tiles_per_group = GROUP_W // PD_TN
    assert CONST_BIAS_DIST <= page + 1 and n_pages % npp == 0 and GROUP_W % PD_TN == 0
    assert rows % PD_TM == 0 and n % PD_TN == 0 and grid[1] % n_steps == 0 and grid[0] * seq_per_row_tile == db
    ck = cache_k.reshape(n_phys, page * heads, HEAD_W)
    cv = cache_v.reshape(n_phys, page * heads, HEAD_W)
    seq_of = lambda i, t: i * seq_per_row_tile + t // n_steps
    vec = pl.BlockSpec((1, heads, HEAD_W), lambda i, t, pt: (seq_of(i, t), 0, 0))
    hbm = pl.BlockSpec(memory_space=pl.ANY)
    srows = 2 * heads

    last = tiles_per_group - 1

    def lo_block(i, t, pt):
        grp, col = t // tiles_per_group, t % tiles_per_group
        kv = (grp == G_AK) | (grp == G_AV)
        slot = jnp.where(grp < G_AK, grp, jnp.where(grp > G_AV, grp - 2, G_AK - 1))
        return slot, i, jnp.where(kv, last, col)

    def kv_block(i, t, pt):
        grp, col = t // tiles_per_group, t % tiles_per_group
        slot = jnp.clip(grp - G_AK, 0, 1)
        return slot, i, jnp.where(grp < G_AK, 0, jnp.where(grp > G_AV, last, col))

    kernel = functools.partial(_proj_decode_kernel, heads=heads, page=page, n_steps=n_steps)
    return pl.pallas_call(
        kernel,
        grid_spec=pltpu.PrefetchScalarGridSpec(
            num_scalar_prefetch=1,
            grid=grid,
            in_specs=[pl.BlockSpec((PD_TM, d), lambda i, t, pt: (i, 0)),
                      pl.BlockSpec((1, d), lambda i, t, pt: (0, 0)),
                      pl.BlockSpec((d, PD_TN), lambda i, t, pt: (0, t)),
                      pl.BlockSpec((heads, N_BUCKETS), lambda i, t, pt: (0, 0)),
                      pl.BlockSpec(diff_lambda.shape, lambda i, t, pt: (0, 0)),
                      vec, vec, vec, hbm, hbm],
            out_specs=[pl.BlockSpec((None, PD_TM, PD_TN), lo_block),
                       pl.BlockSpec((None, PD_TM, PD_TN), kv_block),
                       pl.BlockSpec((1, heads, HEAD_W), lambda i, t, pt: (seq_of(i, t), 0, 0))],
            scratch_shapes=[pltpu.VMEM((PD_TM, d), BF16), pltpu.VMEM((PD_TM, PD_TN), F32),
                            pltpu.VMEM((srows, HEAD_W), BF16), pltpu.VMEM((srows, 1), F32),
                            pltpu.VMEM((srows, 1), F32), pltpu.VMEM((srows, HEAD_W), F32),
                            pltpu.VMEM((srows, page * heads), F32),
                            pltpu.VMEM((DEC_RING, npp, page * heads, HEAD_W), F32),
                            pltpu.VMEM((DEC_RING, npp, page * heads, HEAD_W), F32),
                            pltpu.SemaphoreType.DMA((DEC_RING,))]),
        out_shape=[jax.ShapeDtypeStruct((n // GROUP_W - 2, rows, GROUP_W), BF16),
                   jax.ShapeDtypeStruct((2, rows, GROUP_W), F32),
                   jax.ShapeDtypeStruct((db, heads, HEAD_W), F32)],
        compiler_params=pltpu.CompilerParams(dimension_semantics=("arbitrary", "arbitrary"),
                                             vmem_limit_bytes=VMEM_LIMIT),
        name="proj_decode",
    )(page_table, x, g, w_bf, rel_bias.T, diff_lambda, q_s, k_s, v_s, ck, cv)


def kernel(x_prompt, x_sample, cache_k, cache_v, state_hgrn, page_table, meta_tokens, rel_bias, hgrn_lb,
           norm_g, w_in, hgrn_norm_g, diff_norm_g, diff_lambda, w_out, final_norm_g):
    batch, seq, d = x_prompt.shape
    db = x_sample.shape[0]
    heads = cache_k.shape[3]
    gw = heads * HEAD_W
    assert norm_g.shape[0] == 1 and x_sample.shape[1] == 1, "one layer, one sample token per sequence"
    assert gw == GROUP_W

    g_in = norm_g[0].reshape(1, d)
    g_final = final_norm_g.reshape(1, d)
    g_hg = hgrn_norm_g[0].reshape(1, HEAD_W)
    g_at = diff_norm_g[0].reshape(1, HEAD_W)
    lam_p = diff_lambda[0]

    xp = x_prompt.reshape(batch * seq, d)
    xs = x_sample.reshape(db, d)
    small = jnp.concatenate([meta_tokens.astype(F32), xs], axis=0)

    z_small, w_in_bf = _proj_small(small, g_in, w_in[0])
    z_samp = z_small[:, N_META:]
    samp_heads = lambda g: z_samp[g].reshape(db, heads, HEAD_W)
    q_s, k_s, v_s = samp_heads(G_AQ), samp_heads(G_AK), samp_heads(G_AV)
    o_hg_s, s_s, w_out_bf = _hgrn_step(z_samp, hgrn_lb, state_hgrn, w_out[0], heads)

    z_lo, z_kv, o_at_s = _proj_decode(xp, g_in, w_in_bf, q_s, k_s, v_s, cache_k[0], cache_v[0], page_table,
                                      rel_bias, lam_p)

    o_hg_p, s_p = _hgrn_prompt(z_lo, z_small, hgrn_lb, batch, seq, heads)
    o_at_p, k_p, v_p = _attn_prompt(z_lo, z_kv, z_small, rel_bias, lam_p, batch, seq, heads)
    y_p = _merge(o_hg_p.reshape(batch * seq, gw), o_at_p.reshape(batch * seq, gw), z_lo, LO_SLOT[G_HG],
                 LO_SLOT[G_AG], xp, g_hg, g_at, w_out_bf, g_final, tm=512, heads=heads)

    y_s = _merge(o_hg_s, o_at_s.reshape(db, gw), z_samp, G_HG, G_AG, xs, g_hg, g_at, w_out_bf, g_final,
                 tm=db, heads=heads)

    return (y_p.reshape(batch, seq, d), y_s.reshape(db, 1, d), k_p[None], v_p[None],
            s_p[None], k_s.reshape(1, db, 1, heads, HEAD_W), v_s.reshape(1, db, 1, heads, HEAD_W), s_s)
```

```python
import functools
import math

import numpy as np
import jax
import jax.numpy as jnp
from jax import lax
from jax.experimental import pallas as pl
from jax.experimental.pallas import tpu as pltpu

F32 = jnp.float32
BF16 = jnp.bfloat16

N_META = 16
HEAD_W = 128
AT_DQK = 64
N_BUCKETS = 32
MAX_DISTANCE = 128
MAX_EXACT = N_BUCKETS // 2
HGRN_CHUNK = 64
RMS_EPS = 1e-6
NEG_INF = -1e30
LAMBDA_INIT = 0.8 - 0.6 * math.exp(-0.3 * 0)
G_HQ, G_HF, G_HI, G_HG, G_AQ, G_AK, G_AV, G_AG = range(8)
KV_SLOT = {G_AK: 0, G_AV: 1}
LO_SLOT = {g: (g if g < G_AK else g - 2) for g in range(8) if g not in KV_SLOT}

VMEM_LIMIT = 56 * 1024 * 1024


def _bucket_thresholds():
    n = np.arange(0, 2 * MAX_DISTANCE)
    nl = np.maximum(n, MAX_EXACT).astype(np.float32)
    large = MAX_EXACT + (np.log(nl / np.float32(MAX_EXACT)) / np.float32(math.log(MAX_DISTANCE / MAX_EXACT))
                         * np.float32(N_BUCKETS - MAX_EXACT)).astype(np.int32)
    bucket = np.where(n < MAX_EXACT, n, np.minimum(large, N_BUCKETS - 1))
    return [int(np.argmax(bucket >= b)) for b in range(N_BUCKETS)]


BUCKET_THR = _bucket_thresholds()
CONST_BIAS_DIST = BUCKET_THR[N_BUCKETS - 1]


def _sigmoid(x):
    return 1.0 / (1.0 + jnp.exp(-x))


def _silu(x):
    return x * _sigmoid(x)


def _dot_nt(a, b):
    return lax.dot_general(a, b, (((1,), (1,)), ((), ())), preferred_element_type=F32)


def _dot_tn(a, b):
    return lax.dot_general(a, b, (((0,), (0,)), ((), ())), preferred_element_type=F32)


GROUP_W = 1024


def _proj_small_kernel(x_ref, g_ref, w_ref, z_ref, wb_ref, xn_ref):
    @pl.when(pl.program_id(0) == 0)
    def _():
        x = x_ref[...]
        ms = jnp.mean(x * x, axis=-1, keepdims=True)
        xn_ref[...] = (x * lax.rsqrt(ms + RMS_EPS) * g_ref[...]).astype(BF16)

    w_bf = w_ref[...].astype(BF16)
    wb_ref[...] = w_bf
    z_ref[...] = jnp.dot(xn_ref[...], w_bf, preferred_element_type=F32)


def _proj_small(x, g, w):
    rows, d = x.shape
    n = w.shape[1]
    return pl.pallas_call(
        _proj_small_kernel,
        grid=(n // GROUP_W,),
        in_specs=[pl.BlockSpec((rows, d), lambda j: (0, 0)),
                  pl.BlockSpec((1, d), lambda j: (0, 0)),
                  pl.BlockSpec((d, GROUP_W), lambda j: (0, j))],
        out_specs=[pl.BlockSpec((None, rows, GROUP_W), lambda j: (j, 0, 0)),
                   pl.BlockSpec((d, GROUP_W), lambda j: (0, j))],
        out_shape=[jax.ShapeDtypeStruct((n // GROUP_W, rows, GROUP_W), F32),
                   jax.ShapeDtypeStruct((d, n), BF16)],
        scratch_shapes=[pltpu.VMEM((rows, d), BF16)],
        compiler_params=pltpu.CompilerParams(dimension_semantics=("arbitrary",),
                                             vmem_limit_bytes=VMEM_LIMIT),
        name="proj_small",
    )(x, g, w)


def _forget_lower_bound(lbp):
    e = jnp.exp(lbp - jnp.max(lbp, axis=0, keepdims=True))
    return e[0:1] / jnp.sum(e, axis=0, keepdims=True)


def _cumsum_rows(g, tri_bf):
    g1 = g.astype(BF16)
    r1 = g - g1.astype(F32)
    g2 = r1.astype(BF16)
    g3 = (r1 - g2.astype(F32)).astype(BF16)
    dot = functools.partial(jnp.dot, preferred_element_type=F32)
    return dot(tri_bf, g1) + dot(tri_bf, g2) + dot(tri_bf, g3)


def _hgrn_chunk_local(q, hf, v, lb):
    c = q.shape[0]
    row = lax.broadcasted_iota(jnp.int32, (c, c), 0)
    col = lax.broadcasted_iota(jnp.int32, (c, c), 1)
    causal = row >= col
    f = lb + (1.0 - lb) * _sigmoid(hf)
    k = 1.0 - f
    b = _cumsum_rows(jnp.log(f), causal.astype(BF16))
    qe = (q * jnp.exp(b)).astype(BF16)
    ke = (k * jnp.exp(-b)).astype(BF16)
    a = jnp.where(causal, _dot_nt(qe, ke), 0.0)
    v_bf = v.astype(BF16)
    intra = jnp.dot(a.astype(BF16), v_bf, preferred_element_type=F32)
    b_last = b[c - 1:c]
    kd = (k * jnp.exp(b_last - b)).astype(BF16)
    return qe, intra, _dot_tn(v_bf, kd), jnp.exp(b_last)


HGRN_SLAB = 256
HGRN_PAIR = 128


def _hgrn_prompt_kernel(q_ref, f_ref, v_ref, qm_ref, fm_ref, vm_ref, lbp_ref, o_ref, s_ref,
                        b_ref, qe_ref, ke_ref, kd_ref, vb_ref, vt_ref, a_ref, ut_ref, dec_ref, st_ref):
    ck, slab, pair = HGRN_CHUNK, HGRN_SLAB, HGRN_PAIR
    seq = q_ref.shape[1]
    n_chunks = seq // ck
    dot = functools.partial(jnp.dot, preferred_element_type=F32)
    lb = _forget_lower_bound(lbp_ref[...])

    _, _, st, _ = _hgrn_chunk_local(qm_ref[...], fm_ref[...], vm_ref[...], lb)

    f = lb + (1.0 - lb) * _sigmoid(f_ref[0].astype(F32))
    g = jnp.log(f)
    g1 = g.astype(BF16)
    r1 = g - g1.astype(F32)
    g2 = r1.astype(BF16)
    g3 = (r1 - g2.astype(F32)).astype(BF16)
    gcat = jnp.concatenate([g1, g2, g3], axis=1)
    r = lax.broadcasted_iota(jnp.int32, (slab, slab), 0)
    c = lax.broadcasted_iota(jnp.int32, (slab, slab), 1)
    tri_bd = ((r // ck == c // ck) & (r >= c)).astype(BF16)
    for si in range(seq // slab):
        rows = slice(si * slab, (si + 1) * slab)
        t = dot(tri_bd, gcat[rows])
        b_ref[rows, :] = t[:, :HEAD_W] + t[:, HEAD_W:2 * HEAD_W] + t[:, 2 * HEAD_W:]

    k = 1.0 - f
    b = b_ref[...]
    qe_ref[...] = (q_ref[0].astype(F32) * jnp.exp(b)).astype(BF16)
    ke_ref[...] = (k * jnp.exp(-b)).astype(BF16)
    v = v_ref[0].astype(F32)
    vb_ref[...] = v.astype(BF16)
    vt_ref[...] = v.T.astype(BF16)
    for ci in range(n_chunks):
        rows = slice(ci * ck, (ci + 1) * ck)
        b_last = b_ref[(ci + 1) * ck - 1:(ci + 1) * ck, :]
        dec_ref[ci] = jnp.exp(b_last)
        kd_ref[rows, :] = (k[rows] * jnp.exp(b_last - b_ref[rows, :])).astype(BF16)

    rp = lax.broadcasted_iota(jnp.int32, (pair, pair), 0)
    cp = lax.broadcasted_iota(jnp.int32, (pair, pair), 1)
    keep = (rp // ck == cp // ck) & (rp >= cp)
    for pi in range(seq // pair):
        rows = slice(pi * pair, (pi + 1) * pair)
        a_ref[rows, :] = jnp.where(keep, _dot_nt(qe_ref[rows, :], ke_ref[rows, :]), 0.0).astype(BF16)

    for pi in range(seq // pair):
        rows = slice(pi * pair, (pi + 1) * pair)
        o_ref[0, rows, :] = dot(a_ref[rows, :], vb_ref[rows, :])
    for ci in range(n_chunks):
        rows = slice(ci * ck, (ci + 1) * ck)
        ut_ref[ci] = dot(vt_ref[:, rows], kd_ref[rows, :])

    for ci in range(n_chunks):
        st_ref[ci] = st.astype(BF16)
        st = st * dec_ref[ci] + ut_ref[ci]
    s_ref[0, 0] = st.T

    for ci in range(n_chunks):
        rows = slice(ci * ck, (ci + 1) * ck)
        o_ref[0, rows, :] += _dot_nt(qe_ref[rows, :], st_ref[ci])


def _hgrn_prompt(z_lo, z_small, hgrn_lb, batch, seq, heads):
    zp3 = z_lo.reshape(z_lo.shape[0], batch, seq, GROUP_W)
    big = lambda g: pl.BlockSpec((None, 1, seq, HEAD_W), lambda b, h, g=g: (LO_SLOT[g], b, 0, h))
    meta = lambda g: pl.BlockSpec((None, N_META, HEAD_W), lambda b, h, g=g: (g, 0, h))
    return pl.pallas_call(
        _hgrn_prompt_kernel,
        grid=(batch, heads),
        in_specs=[big(G_HQ), big(G_HF), big(G_HI), meta(G_HQ), meta(G_HF), meta(G_HI),
                  pl.BlockSpec((hgrn_lb.shape[0], HEAD_W), lambda b, h: (0, h))],
        out_specs=[pl.BlockSpec((1, seq, HEAD_W), lambda b, h: (b, 0, h)),
                   pl.BlockSpec((1, 1, HEAD_W, HEAD_W), lambda b, h: (b, h, 0, 0))],
        out_shape=[jax.ShapeDtypeStruct((batch, seq, heads * HEAD_W), F32),
                   jax.ShapeDtypeStruct((batch, heads, HEAD_W, HEAD_W), F32)],
        scratch_shapes=[pltpu.VMEM((seq, HEAD_W), F32),
                        pltpu.VMEM((seq, HEAD_W), BF16), pltpu.VMEM((seq, HEAD_W), BF16),
                        pltpu.VMEM((seq, HEAD_W), BF16), pltpu.VMEM((seq, HEAD_W), BF16),
                        pltpu.VMEM((HEAD_W, seq), BF16), pltpu.VMEM((seq, HGRN_PAIR), BF16),
                        pltpu.VMEM((seq // HGRN_CHUNK, HEAD_W, HEAD_W), F32),
                        pltpu.VMEM((seq // HGRN_CHUNK, 1, HEAD_W), F32),
                        pltpu.VMEM((seq // HGRN_CHUNK, HEAD_W, HEAD_W), BF16)],
        compiler_params=pltpu.CompilerParams(dimension_semantics=("arbitrary", "arbitrary"),
                                             vmem_limit_bytes=VMEM_LIMIT),
        name="hgrn_prompt",
    )(zp3, zp3, zp3, z_small, z_small, z_small, hgrn_lb)


def _diff_lambda(lp):
    s01 = jnp.sum(lp[0:1] * lp[1:2], axis=-1, keepdims=True)
    s23 = jnp.sum(lp[2:3] * lp[3:4], axis=-1, keepdims=True)
    return jnp.exp(s01) - jnp.exp(s23) + LAMBDA_INIT


def _bias_by_distance(dist, bias_of_bucket):
    out = jnp.where(dist >= BUCKET_THR[1], bias_of_bucket(1), bias_of_bucket(0))
    for bkt in range(2, N_BUCKETS):
        out = jnp.where(dist >= BUCKET_THR[bkt], bias_of_bucket(bkt), out)
    return out


ATT_TQ = 256
ATT_AHEAD = 2


def _colmax(s):
    return jnp.max(s.reshape(s.shape[0] // 8, 8, s.shape[1]), axis=0)


def _colsum(s):
    return jnp.sum(s.reshape(s.shape[0] // 8, 8, s.shape[1]), axis=0)


def _assemble_kv_block(main_ref, lead_refs, meta_ref, out_ref, first):
    lead = jnp.where(first, meta_ref[...], jnp.concatenate([r[0] for r in lead_refs], axis=0))
    rows = jnp.concatenate([lead, main_ref[0, :out_ref.shape[1] - N_META, :]], axis=0)
    out_ref[...] = rows.reshape(out_ref.shape)


def _attn_prompt_kernel(rb_ref, lp_ref, q_ref, k_ref, v_ref, km_ref, vm_ref,
                        ka_ref, kl0_ref, kl1_ref, kma_ref, va_ref, vl0_ref, vl1_ref, vma_ref,
                        o_ref, ko_ref, vo_ref,
                        kb_ref, vt_ref, vmt_ref, s_ref, p_ref, bdiag_ref, bnear_ref, bmeta_ref):
    h = pl.program_id(0)
    b = pl.program_id(1)
    _assemble_kv_block(ka_ref, (kl0_ref, kl1_ref), kma_ref, ko_ref, h == 0)
    _assemble_kv_block(va_ref, (vl0_ref, vl1_ref), vma_ref, vo_ref, h == 0)
    tq = ATT_TQ
    n_q = q_ref.shape[1] // tq
    log2e = 1.0 / math.log(2.0)
    bias_of_bucket = lambda bkt: rb_ref[bkt * rb_ref.shape[0] // N_BUCKETS + h] * log2e
    far_bias = bias_of_bucket(N_BUCKETS - 1)
    key = lax.broadcasted_iota(jnp.int32, (tq, tq), 0)
    qry = lax.broadcasted_iota(jnp.int32, (tq, tq), 1)

    @pl.when(b == 0)
    def _():
        bdiag_ref[...] = _bias_by_distance(qry - key, bias_of_bucket)
        bnear_ref[...] = _bias_by_distance(qry - key + tq, bias_of_bucket)
        keym = lax.broadcasted_iota(jnp.int32, (N_META, tq), 0)
        qrym = lax.broadcasted_iota(jnp.int32, (N_META, tq), 1)
        bmeta_ref[...] = _bias_by_distance(qrym + N_META - keym, bias_of_bucket)

    kb_ref[...] = k_ref[0].astype(BF16)
    vt_ref[...] = v_ref[0].T.astype(BF16)
    pad = jnp.zeros((HEAD_W - N_META, HEAD_W), F32)
    vmt_ref[...] = jnp.concatenate([vm_ref[...], pad], axis=0).T.astype(BF16)
    km_bf = km_ref[...].astype(BF16)
    lam = _diff_lambda(lp_ref[...])
    causal = qry >= key
    dot = functools.partial(jnp.dot, preferred_element_type=F32)

    def pass_a(qi):
        par = qi % (ATT_AHEAD + 1)
        q = q_ref[0, qi * tq:(qi + 1) * tq, :].astype(F32) * (AT_DQK ** -0.5 * log2e)
        lane = lax.broadcasted_iota(jnp.int32, q.shape, 1)
        qc = (jnp.where(lane < AT_DQK, q, 0.0).astype(BF16), jnp.where(lane >= AT_DQK, q, 0.0).astype(BF16))
        n_keys = (qi + 1) * tq
        out = []
        for c in range(2):
            s_ref[par, c, :n_keys, :] = _dot_nt(kb_ref[:n_keys, :], qc[c])
            sm = _dot_nt(km_bf, qc[c]) + (bmeta_ref[...] if qi == 0 else far_bias)
            m8 = _colmax(sm)
            for j in range(qi + 1):
                m8 = jnp.maximum(m8, _colmax(biased(s_ref[par, c, j * tq:(j + 1) * tq, :], j, qi))
                                 + (far_bias if j < qi - 1 else 0.0))
            out.append((sm, jnp.max(m8, axis=0, keepdims=True)))
        return out

    def biased(s, j, qi):
        if j == qi:
            return jnp.where(causal, s + bdiag_ref[...], NEG_INF)
        if j == qi - 1:
            return s + bnear_ref[...]
        return s

    def pass_b(qi, stats):
        par = qi % (ATT_AHEAD + 1)
        n_keys = (qi + 1) * tq
        o_c = []
        for c in range(2):
            sm, m = stats[c]
            m_far = m - far_bias
            pm = jnp.exp2(sm - m)
            l8 = _colsum(pm)
            for j in range(qi + 1):
                rows = slice(j * tq, (j + 1) * tq)
                p = jnp.exp2(biased(s_ref[par, c, rows, :], j, qi) - (m_far if j < qi - 1 else m))
                l8 = l8 + _colsum(p)
                p_ref[qi % 2, c, rows, :] = p.astype(BF16)
            acc = dot(vt_ref[:, :n_keys], p_ref[qi % 2, c, :n_keys, :]) + dot(vmt_ref[:, :N_META], pm.astype(BF16))
            o_c.append(acc / jnp.sum(l8, axis=0, keepdims=True))
        o_ref[0, qi * tq:(qi + 1) * tq, :] = (o_c[0] - lam * o_c[1]).T

    stats = [pass_a(qi) for qi in range(min(ATT_AHEAD, n_q))]
    for qi in range(n_q):
        if qi + ATT_AHEAD < n_q:
            stats.append(pass_a(qi + ATT_AHEAD))
        pass_b(qi, stats.pop(0))


def _attn_prompt(z_lo, z_kv, z_small, rel_bias, diff_lambda, batch, seq, heads):
    assert CONST_BIAS_DIST <= ATT_TQ + 1 and CONST_BIAS_DIST <= N_META + 1 + ATT_TQ - N_META
    zlo = z_lo.reshape(z_lo.shape[0], batch, seq, GROUP_W)
    zkv = z_kv.reshape(z_kv.shape[0], batch, seq, GROUP_W)
    tq = ATT_TQ
    full = lambda slot: pl.BlockSpec((None, 1, seq, HEAD_W), lambda h, b, slot=slot: (slot, b, 0, h))
    meta = lambda g: pl.BlockSpec((None, N_META, HEAD_W), lambda h, b, g=g: (g, 0, h))

    tokens = N_META + seq
    blk = -(-tokens // (8 * heads)) * 8
    assert N_META == 16 and (heads - 1) * blk < seq
    asm_main = lambda g: pl.BlockSpec((None, 1, blk, GROUP_W), lambda h, b, g=g: (KV_SLOT[g], b, h, 0))
    asm_lead = lambda g, piece: pl.BlockSpec(
        (None, 1, 8, GROUP_W),
        lambda h, b, g=g, piece=piece: (KV_SLOT[g], b, jnp.maximum(h * (blk // 8) - 2 + piece, 0), 0))
    asm_meta = lambda g: pl.BlockSpec((None, N_META, GROUP_W), lambda h, b, g=g: (g, 0, 0))
    asm_in = lambda g: [asm_main(g), asm_lead(g, 0), asm_lead(g, 1), asm_meta(g)]
    asm_out = pl.BlockSpec((1, blk, heads, HEAD_W), lambda h, b: (b, h, 0, 0))
    kv_shape = jax.ShapeDtypeStruct((batch, tokens, heads, HEAD_W), F32)

    return pl.pallas_call(
        _attn_prompt_kernel,
        grid=(heads, batch),
        in_specs=[pl.BlockSpec(memory_space=pltpu.SMEM),
                  pl.BlockSpec(diff_lambda.shape, lambda h, b: (0, 0)),
                  full(LO_SLOT[G_AQ]), full(KV_SLOT[G_AK]), full(KV_SLOT[G_AV]), meta(G_AK), meta(G_AV)]
                 + asm_in(G_AK) + asm_in(G_AV),
        out_specs=[pl.BlockSpec((1, seq, HEAD_W), lambda h, b: (b, 0, h)), asm_out, asm_out],
        out_shape=[jax.ShapeDtypeStruct((batch, seq, heads * HEAD_W), F32), kv_shape, kv_shape],
        scratch_shapes=[pltpu.VMEM((seq, HEAD_W), BF16), pltpu.VMEM((HEAD_W, seq), BF16),
                        pltpu.VMEM((HEAD_W, HEAD_W), BF16),
                        pltpu.VMEM((ATT_AHEAD + 1, 2, seq, tq), F32), pltpu.VMEM((2, 2, seq, tq), BF16),
                        pltpu.VMEM((tq, tq), F32), pltpu.VMEM((tq, tq), F32),
                        pltpu.VMEM((N_META, tq), F32)],
        compiler_params=pltpu.CompilerParams(dimension_semantics=("arbitrary", "arbitrary"),
                                             vmem_limit_bytes=VMEM_LIMIT),
        name="attn_prompt",
    )(rel_bias.reshape(-1), diff_lambda, zlo, zkv, zkv, z_small, z_small,
      zkv, zkv, zkv, z_small, zkv, zkv, zkv, z_small)


def _head_rmsnorm(o, g):
    parts = []
    for hh in range(o.shape[1] // HEAD_W):
        oh = o[:, hh * HEAD_W:(hh + 1) * HEAD_W]
        ms = jnp.mean(oh * oh, axis=-1, keepdims=True)
        parts.append(oh * lax.rsqrt(ms + RMS_EPS) * g)
    return jnp.concatenate(parts, axis=1)


MERGE_SUB = 256


def _merge_kernel(ohg_ref, hg_ref, oat_ref, ag_ref, x_ref, gh_ref, ga_ref, w_ref, gf_ref, y_ref):
    tm = x_ref.shape[0]
    sub = min(MERGE_SUB, tm)
    subs = [slice(s, s + sub) for s in range(0, tm, sub)]
    mixed = []
    for r in subs:
        mix_h = _head_rmsnorm(ohg_ref[r, :], gh_ref[...]) * _silu(hg_ref[r, :].astype(F32))
        mix_a = (_head_rmsnorm(oat_ref[r, :], ga_ref[...]) * (1.0 - LAMBDA_INIT)
                 * _silu(ag_ref[r, :].astype(F32)))
        mixed.append(jnp.concatenate([mix_h, mix_a], axis=1).astype(BF16))
    proj = [jnp.dot(m, w_ref[...], preferred_element_type=F32) for m in mixed]
    for r, p in zip(subs, proj):
        hres = x_ref[r, :] + p
        ms = jnp.mean(hres * hres, axis=-1, keepdims=True)
        y_ref[r, :] = hres * lax.rsqrt(ms + RMS_EPS) * gf_ref[...]


def _merge(o_hg, o_at, z, slot_hg, slot_ag, x, g_hg, g_at, w_out_bf, g_final, tm, heads):
    rows, d = x.shape
    gw = heads * HEAD_W
    row_blk = lambda w: pl.BlockSpec((tm, w), lambda i: (i, 0))
    gate = lambda slot: pl.BlockSpec((None, tm, gw), lambda i, slot=slot: (slot, i, 0))
    const = lambda shape: pl.BlockSpec(shape, lambda i: (0, 0))
    return pl.pallas_call(
        _merge_kernel,
        grid=(rows // tm,),
        in_specs=[row_blk(gw), gate(slot_hg), row_blk(gw), gate(slot_ag), row_blk(d),
                  const((1, HEAD_W)), const((1, HEAD_W)),
                  pl.BlockSpec(w_out_bf.shape, lambda i: (0, 0), pipeline_mode=pl.Buffered(1)),
                  const((1, d))],
        out_specs=row_blk(d),
        out_shape=jax.ShapeDtypeStruct((rows, d), F32),
        compiler_params=pltpu.CompilerParams(dimension_semantics=("arbitrary",),
                                             vmem_limit_bytes=VMEM_LIMIT),
        name="merge",
    )(o_hg, z, o_at, z, x, g_hg, g_at, w_out_bf, g_final)


def _transpose_rows_to_cols(x):
    pad = jnp.zeros((HEAD_W - x.shape[0], HEAD_W), F32)
    return jnp.concatenate([x, pad], axis=0).T


def _hgrn_step_kernel(q_ref, f_ref, v_ref, lbp_ref, s_ref, w_ref, o_ref, sn_ref, wb_ref):
    wb_ref[...] = w_ref[...].astype(BF16)
    lb = _forget_lower_bound(lbp_ref[...])
    f = lb + (1.0 - lb) * _sigmoid(f_ref[...])
    v = v_ref[...]
    f_t = _transpose_rows_to_cols(f)
    q_t = _transpose_rows_to_cols(q_ref[...])
    for i in range(q_ref.shape[0]):
        fi = jnp.broadcast_to(f_t[:, i:i + 1], (HEAD_W, HEAD_W))
        s_new = fi * s_ref[0, i, 0] + (1.0 - fi) * v[i:i + 1]
        sn_ref[0, i, 0] = s_new
        o_ref[i:i + 1, :] = jnp.sum(q_t[:, i:i + 1] * s_new, axis=0, keepdims=True)


def _hgrn_step(z_samp, hgrn_lb, state, w_out, heads):
    db = z_samp.shape[1]
    d = w_out.shape[0]
    col = lambda g: pl.BlockSpec((None, db, HEAD_W), lambda h, g=g: (g, 0, h))
    st_spec = pl.BlockSpec((1, db, 1, HEAD_W, HEAD_W), lambda h: (0, 0, h, 0, 0))
    w_spec = pl.BlockSpec((d // heads, w_out.shape[1]), lambda h: (h, 0))
    return pl.pallas_call(
        _hgrn_step_kernel,
        grid=(heads,),
        in_specs=[col(G_HQ), col(G_HF), col(G_HI),
                  pl.BlockSpec((hgrn_lb.shape[0], HEAD_W), lambda h: (0, h)), st_spec, w_spec],
        out_specs=[pl.BlockSpec((db, HEAD_W), lambda h: (0, h)), st_spec, w_spec],
        out_shape=[jax.ShapeDtypeStruct((db, heads * HEAD_W), F32),
                   jax.ShapeDtypeStruct(state.shape, F32),
                   jax.ShapeDtypeStruct(w_out.shape, BF16)],
        compiler_params=pltpu.CompilerParams(dimension_semantics=("arbitrary",),
                                             vmem_limit_bytes=VMEM_LIMIT),
        name="hgrn_step",
    )(z_samp, z_samp, z_samp, hgrn_lb, state, w_out)


DEC_PAGES_PER_STEP = 8
DEC_RING = 3


def _page_copies(pt_ref, ck_ref, cv_ref, kbuf_ref, vbuf_ref, sem_ref, step, n_steps):
    npp = DEC_PAGES_PER_STEP
    slot = step % DEC_RING
    seq = step // n_steps
    first_page = (step % n_steps) * npp
    copies = []
    for p in range(npp):
        phys = pt_ref[seq, first_page + p]
        copies.append(pltpu.make_async_copy(ck_ref.at[phys], kbuf_ref.at[slot, p], sem_ref.at[slot]))
        copies.append(pltpu.make_async_copy(cv_ref.at[phys], vbuf_ref.at[slot, p], sem_ref.at[slot]))
    return copies


def _proj_decode_kernel(pt_ref, x_ref, g_ref, w_ref, rbt_ref, lp_ref, q_ref, kn_ref, vn_ref, ck_ref, cv_ref,
                        z_ref, zkv_ref, o_ref, xn_ref, ztile_ref, qs_ref, m_ref, l_ref, acc_ref, blast_ref,
                        kbuf_ref, vbuf_ref, sem_ref, *, heads, page, n_steps):
    npp = DEC_PAGES_PER_STEP
    col_tile = pl.program_id(1)
    row_sub = pl.program_id(2)
    step = (pl.program_id(0) * pl.num_programs(1) + col_tile) * pl.num_programs(2) + row_sub
    total_steps = pl.num_programs(0) * pl.num_programs(1) * pl.num_programs(2)
    copies_for = lambda s: _page_copies(pt_ref, ck_ref, cv_ref, kbuf_ref, vbuf_ref, sem_ref, s, n_steps)

    def start_all(copies):
        for cp in copies:
            cp.start()

    @pl.when(step == 0)
    def _():
        for s in range(DEC_RING - 1):
            start_all(copies_for(s))

    @pl.when(step + (DEC_RING - 1) < total_steps)
    def _():
        start_all(copies_for(step + (DEC_RING - 1)))

    @pl.when(col_tile == 0)
    def _():
        x = x_ref[...]
        ms = jnp.mean(x * x, axis=-1, keepdims=True)
        xn_ref[row_sub] = (x * lax.rsqrt(ms + RMS_EPS) * g_ref[...]).astype(BF16)

    for cp in copies_for(step):
        cp.wait()
    slot = step % DEC_RING
    k_refs = [kbuf_ref.at[slot, p] for p in range(npp)]
    v_refs = [vbuf_ref.at[slot, p] for p in range(npp)]

    j = step % n_steps
    rows = 2 * heads
    lanes = page * heads
    rbt = jnp.concatenate([rbt_ref[...], rbt_ref[...]], axis=0)
    far_bias = rbt[:, N_BUCKETS - 1:N_BUCKETS]
    r_id = lax.broadcasted_iota(jnp.int32, (rows, lanes), 0)
    l_id = lax.broadcasted_iota(jnp.int32, (rows, lanes), 1)
    own_head = l_id % heads == r_id % heads

    def stacked(x):
        lane = lax.broadcasted_iota(jnp.int32, x.shape, 1)
        return jnp.concatenate([jnp.where(lane < AT_DQK, x, 0.0), jnp.where(lane >= AT_DQK, x, 0.0)], axis=0)

    @pl.when(j == 0)
    def _():
        qs_ref[...] = stacked(q_ref[0] * (AT_DQK ** -0.5)).astype(BF16)
        m_ref[...] = jnp.full(m_ref.shape, NEG_INF, F32)
        l_ref[...] = jnp.zeros(l_ref.shape, F32)
        acc_ref[...] = jnp.zeros(acc_ref.shape, F32)
        blast_ref[...] = _bias_by_distance(page - l_id // heads, lambda bkt: rbt[:, bkt:bkt + 1])

    qs = qs_ref[...]
    s_parts = []
    for i in range(npp):
        s_i = _dot_nt(qs, k_refs[i][...].astype(BF16))
        if i == npp - 1:
            s_i = s_i + jnp.where(j == n_steps - 1, blast_ref[...], far_bias)
        else:
            s_i = s_i + far_bias
        s_parts.append(jnp.where(own_head, s_i, NEG_INF))

    m_old = m_ref[...]
    m_blk = s_parts[0]
    for s_i in s_parts[1:]:
        m_blk = jnp.maximum(m_blk, s_i)
    m_new = jnp.maximum(m_old, jnp.max(m_blk, axis=-1, keepdims=True))
    alpha = jnp.exp(m_old - m_new)

    ztile_ref[...] = jnp.dot(xn_ref[row_sub], w_ref[...], preferred_element_type=F32)

    l_blk = jnp.zeros((rows, lanes), F32)
    pv = jnp.zeros(acc_ref.shape, F32)
    for i in range(npp):
        p_i = jnp.exp(s_parts[i] - m_new)
        l_blk = l_blk + p_i
        pv = pv + jnp.dot(p_i.astype(BF16), v_refs[i][...].astype(BF16), preferred_element_type=F32)
    l_ref[...] = alpha * l_ref[...] + jnp.sum(l_blk, axis=-1, keepdims=True)
    acc_ref[...] = alpha * acc_ref[...] + pv
    m_ref[...] = m_new

    @pl.when(j == n_steps - 1)
    def _():
        kn = jnp.concatenate([kn_ref[0], kn_ref[0]], axis=0)
        vn = jnp.concatenate([vn_ref[0], vn_ref[0]], axis=0)
        s_new = jnp.sum(qs.astype(F32) * kn, axis=-1, keepdims=True) + rbt[:, 0:1]
        m_old = m_ref[...]
        m_fin = jnp.maximum(m_old, s_new)
        alpha = jnp.exp(m_old - m_fin)
        p_new = jnp.exp(s_new - m_fin)
        l_fin = alpha * l_ref[...] + p_new
        o = (alpha * acc_ref[...] + p_new * vn) / l_fin
        o_ref[0] = o[:heads] - _diff_lambda(lp_ref[...]) * o[heads:]

    group = col_tile // (GROUP_W // z_ref.shape[1])
    is_kv = (group == G_AK) | (group == G_AV)

    @pl.when(is_kv)
    def _():
        zkv_ref[...] = ztile_ref[...]

    @pl.when(jnp.logical_not(is_kv))
    def _():
        z_ref[...] = ztile_ref[...].astype(BF16)


PD_TM = 1024
PD_TN = 256
PD_ROW_TILES_PER_W = 2


def _proj_decode(x, g, w_bf, q_s, k_s, v_s, cache_k, cache_v, page_table, rel_bias, diff_lambda):
    assert (G_AK, G_AV) == (G_AQ + 1, G_AQ + 2)
    rows, d = x.shape
    n = w_bf.shape[1]
    db, n_pages = page_table.shape
    n_phys, page, heads, _ = cache_k.shape
    npp = DEC_PAGES_PER_STEP
    n_steps = n_pages // npp
    nr = PD_ROW_TILES_PER_W
    grid = (rows // (PD_TM * nr), n // PD_TN, nr)
    tiles_per_group = GROUP_W // PD_TN
    assert CONST_BIAS_DIST <= page + 1 and n_pages % npp == 0 and GROUP_W % PD_TN == 0
    assert rows % (PD_TM * nr) == 0 and n % PD_TN == 0 and grid[0] * grid[1] * nr == db * n_steps
    ck = cache_k.reshape(n_phys, page * heads, HEAD_W)
    cv = cache_v.reshape(n_phys, page * heads, HEAD_W)
    seq_of = lambda p, t, r: ((p * grid[1] + t) * nr + r) // n_steps
    vec = pl.BlockSpec((1, heads, HEAD_W), lambda p, t, r, pt: (seq_of(p, t, r), 0, 0))
    hbm = pl.BlockSpec(memory_space=pl.ANY)
    srows = 2 * heads

    x_block = lambda p, t, r, pt: (jnp.where(t == 0, p * nr + r, p * nr + nr - 1), 0)

    last = tiles_per_group - 1

    def lo_block(p, t, r, pt):
        grp, col = t // tiles_per_group, t % tiles_per_group
        kv = (grp == G_AK) | (grp == G_AV)
        slot = jnp.where(grp < G_AK, grp, jnp.where(grp > G_AV, grp - 2, G_AK - 1))
        return slot, jnp.where(kv, p * nr + nr - 1, p * nr + r), jnp.where(kv, last, col)

    def kv_block(p, t, r, pt):
        grp, col = t // tiles_per_group, t % tiles_per_group
        slot = jnp.clip(grp - G_AK, 0, 1)
        row = jnp.where(grp < G_AK, p * nr, jnp.where(grp > G_AV, p * nr + nr - 1, p * nr + r))
        return slot, row, jnp.where(grp < G_AK, 0, jnp.where(grp > G_AV, last, col))

    kernel = functools.partial(_proj_decode_kernel, heads=heads, page=page, n_steps=n_steps)
    return pl.pallas_call(
        kernel,
        grid_spec=pltpu.PrefetchScalarGridSpec(
            num_scalar_prefetch=1,
            grid=grid,
            in_specs=[pl.BlockSpec((PD_TM, d), x_block),
                      pl.BlockSpec((1, d), lambda p, t, r, pt: (0, 0)),
                      pl.BlockSpec((d, PD_TN), lambda p, t, r, pt: (0, t)),
                      pl.BlockSpec((heads, N_BUCKETS), lambda p, t, r, pt: (0, 0)),
                      pl.BlockSpec(diff_lambda.shape, lambda p, t, r, pt: (0, 0)),
                      vec, vec, vec, hbm, hbm],
            out_specs=[pl.BlockSpec((None, PD_TM, PD_TN), lo_block),
                       pl.BlockSpec((None, PD_TM, PD_TN), kv_block),
                       pl.BlockSpec((1, heads, HEAD_W), lambda p, t, r, pt: (seq_of(p, t, r), 0, 0))],
            scratch_shapes=[pltpu.VMEM((nr, PD_TM, d), BF16), pltpu.VMEM((PD_TM, PD_TN), F32),
                            pltpu.VMEM((srows, HEAD_W), BF16), pltpu.VMEM((srows, 1), F32),
                            pltpu.VMEM((srows, 1), F32), pltpu.VMEM((srows, HEAD_W), F32),
                            pltpu.VMEM((srows, page * heads), F32),
                            pltpu.VMEM((DEC_RING, npp, page * heads, HEAD_W), F32),
                            pltpu.VMEM((DEC_RING, npp, page * heads, HEAD_W), F32),
                            pltpu.SemaphoreType.DMA((DEC_RING,))]),
        out_shape=[jax.ShapeDtypeStruct((n // GROUP_W - 2, rows, GROUP_W), BF16),
                   jax.ShapeDtypeStruct((2, rows, GROUP_W), F32),
                   jax.ShapeDtypeStruct((db, heads, HEAD_W), F32)],
        compiler_params=pltpu.CompilerParams(dimension_semantics=("arbitrary", "arbitrary", "arbitrary"),
                                             vmem_limit_bytes=VMEM_LIMIT),
        name="proj_decode",
    )(page_table, x, g, w_bf, rel_bias.T, diff_lambda, q_s, k_s, v_s, ck, cv)


def kernel(x_prompt, x_sample, cache_k, cache_v, state_hgrn, page_table, meta_tokens, rel_bias, hgrn_lb,
           norm_g, w_in, hgrn_norm_g, diff_norm_g, diff_lambda, w_out, final_norm_g):
    batch, seq, d = x_prompt.shape
    db = x_sample.shape[0]
    heads = cache_k.shape[3]
    gw = heads * HEAD_W
    assert norm_g.shape[0] == 1 and x_sample.shape[1] == 1, "one layer, one sample token per sequence"
    assert gw == GROUP_W

    g_in = norm_g[0].reshape(1, d)
    g_final = final_norm_g.reshape(1, d)
    g_hg = hgrn_norm_g[0].reshape(1, HEAD_W)
    g_at = diff_norm_g[0].reshape(1, HEAD_W)
    lam_p = diff_lambda[0]

    xp = x_prompt.reshape(batch * seq, d)
    xs = x_sample.reshape(db, d)
    small = jnp.concatenate([meta_tokens.astype(F32), xs], axis=0)

    z_small, w_in_bf = _proj_small(small, g_in, w_in[0])
    z_samp = z_small[:, N_META:]
    samp_heads = lambda g: z_samp[g].reshape(db, heads, HEAD_W)
    q_s, k_s, v_s = samp_heads(G_AQ), samp_heads(G_AK), samp_heads(G_AV)
    o_hg_s, s_s, w_out_bf = _hgrn_step(z_samp, hgrn_lb, state_hgrn, w_out[0], heads)

    z_lo, z_kv, o_at_s = _proj_decode(xp, g_in, w_in_bf, q_s, k_s, v_s, cache_k[0], cache_v[0], page_table,
                                      rel_bias, lam_p)

    o_hg_p, s_p = _hgrn_prompt(z_lo, z_small, hgrn_lb, batch, seq, heads)
    o_at_p, k_p, v_p = _attn_prompt(z_lo, z_kv, z_small, rel_bias, lam_p, batch, seq, heads)
    y_p = _merge(o_hg_p.reshape(batch * seq, gw), o_at_p.reshape(batch * seq, gw), z_lo, LO_SLOT[G_HG],
                 LO_SLOT[G_AG], xp, g_hg, g_at, w_out_bf, g_final, tm=512, heads=heads)

    y_s = _merge(o_hg_s, o_at_s.reshape(db, gw), z_samp, G_HG, G_AG, xs, g_hg, g_at, w_out_bf, g_final,
                 tm=db, heads=heads)

    return (y_p.reshape(batch, seq, d), y_s.reshape(db, 1, d), k_p[None], v_p[None],
            s_p[None], k_s.reshape(1, db, 1, heads, HEAD_W), v_s.reshape(1, db, 1, heads, HEAD_W), s_s)
```

```python
import functools
import math

import numpy as np
import jax
import jax.numpy as jnp
from jax import lax
from jax.experimental import pallas as pl
from jax.experimental.pallas import tpu as pltpu

F32 = jnp.float32
BF16 = jnp.bfloat16

N_META = 16
HEAD_W = 128
AT_DQK = 64
N_BUCKETS = 32
MAX_DISTANCE = 128
MAX_EXACT = N_BUCKETS // 2
HGRN_CHUNK = 64
RMS_EPS = 1e-6
NEG_INF = -1e30
LAMBDA_INIT = 0.8 - 0.6 * math.exp(-0.3 * 0)
G_HQ, G_HF, G_HI, G_HG, G_AQ, G_AK, G_AV, G_AG = range(8)
KV_SLOT = {G_AK: 0, G_AV: 1}
LO_SLOT = {g: (g if g < G_AK else g - 2) for g in range(8) if g not in KV_SLOT}

VMEM_LIMIT = 56 * 1024 * 1024


def _bucket_thresholds():
    n = np.arange(0, 2 * MAX_DISTANCE)
    nl = np.maximum(n, MAX_EXACT).astype(np.float32)
    large = MAX_EXACT + (np.log(nl / np.float32(MAX_EXACT)) / np.float32(math.log(MAX_DISTANCE / MAX_EXACT))
                         * np.float32(N_BUCKETS - MAX_EXACT)).astype(np.int32)
    bucket = np.where(n < MAX_EXACT, n, np.minimum(large, N_BUCKETS - 1))
    return [int(np.argmax(bucket >= b)) for b in range(N_BUCKETS)]


BUCKET_THR = _bucket_thresholds()
CONST_BIAS_DIST = BUCKET_THR[N_BUCKETS - 1]


def _sigmoid(x):
    return 1.0 / (1.0 + jnp.exp(-x))


def _silu(x):
    return x * _sigmoid(x)


def _dot_nt(a, b):
    return lax.dot_general(a, b, (((1,), (1,)), ((), ())), preferred_element_type=F32)


def _dot_tn(a, b):
    return lax.dot_general(a, b, (((0,), (0,)), ((), ())), preferred_element_type=F32)


GROUP_W = 1024


def _proj_small_kernel(x_ref, g_ref, w_ref, z_ref, wb_ref, xn_ref):
    @pl.when(pl.program_id(0) == 0)
    def _():
        x = x_ref[...]
        ms = jnp.mean(x * x, axis=-1, keepdims=True)
        xn_ref[...] = (x * lax.rsqrt(ms + RMS_EPS) * g_ref[...]).astype(BF16)

    w_bf = w_ref[...].astype(BF16)
    wb_ref[...] = w_bf
    z_ref[...] = jnp.dot(xn_ref[...], w_bf, preferred_element_type=F32)


def _proj_small(x, g, w):
    rows, d = x.shape
    n = w.shape[1]
    return pl.pallas_call(
        _proj_small_kernel,
        grid=(n // GROUP_W,),
        in_specs=[pl.BlockSpec((rows, d), lambda j: (0, 0)),
                  pl.BlockSpec((1, d), lambda j: (0, 0)),
                  pl.BlockSpec((d, GROUP_W), lambda j: (0, j))],
        out_specs=[pl.BlockSpec((None, rows, GROUP_W), lambda j: (j, 0, 0)),
                   pl.BlockSpec((d, GROUP_W), lambda j: (0, j))],
        out_shape=[jax.ShapeDtypeStruct((n // GROUP_W, rows, GROUP_W), F32),
                   jax.ShapeDtypeStruct((d, n), BF16)],
        scratch_shapes=[pltpu.VMEM((rows, d), BF16)],
        compiler_params=pltpu.CompilerParams(dimension_semantics=("arbitrary",),
                                             vmem_limit_bytes=VMEM_LIMIT),
        name="proj_small",
    )(x, g, w)


def _forget_lower_bound(lbp):
    e = jnp.exp(lbp - jnp.max(lbp, axis=0, keepdims=True))
    return e[0:1] / jnp.sum(e, axis=0, keepdims=True)


def _cumsum_rows(g, tri_bf):
    g1 = g.astype(BF16)
    r1 = g - g1.astype(F32)
    g2 = r1.astype(BF16)
    g3 = (r1 - g2.astype(F32)).astype(BF16)
    dot = functools.partial(jnp.dot, preferred_element_type=F32)
    return dot(tri_bf, g1) + dot(tri_bf, g2) + dot(tri_bf, g3)


def _hgrn_chunk_local(q, hf, v, lb):
    c = q.shape[0]
    row = lax.broadcasted_iota(jnp.int32, (c, c), 0)
    col = lax.broadcasted_iota(jnp.int32, (c, c), 1)
    causal = row >= col
    f = lb + (1.0 - lb) * _sigmoid(hf)
    k = 1.0 - f
    b = _cumsum_rows(jnp.log(f), causal.astype(BF16))
    qe = (q * jnp.exp(b)).astype(BF16)
    ke = (k * jnp.exp(-b)).astype(BF16)
    a = jnp.where(causal, _dot_nt(qe, ke), 0.0)
    v_bf = v.astype(BF16)
    intra = jnp.dot(a.astype(BF16), v_bf, preferred_element_type=F32)
    b_last = b[c - 1:c]
    kd = (k * jnp.exp(b_last - b)).astype(BF16)
    return qe, intra, _dot_tn(v_bf, kd), jnp.exp(b_last)


HGRN_SLAB = 256
HGRN_PAIR = 128


def _hgrn_prompt_kernel(q_ref, f_ref, v_ref, qm_ref, fm_ref, vm_ref, lbp_ref, o_ref, s_ref,
                        b_ref, qe_ref, ke_ref, kd_ref, vb_ref, vt_ref, a_ref, ut_ref, dec_ref, st_ref):
    ck, slab, pair = HGRN_CHUNK, HGRN_SLAB, HGRN_PAIR
    seq = q_ref.shape[1]
    n_chunks = seq // ck
    dot = functools.partial(jnp.dot, preferred_element_type=F32)
    lb = _forget_lower_bound(lbp_ref[...])

    _, _, st, _ = _hgrn_chunk_local(qm_ref[...], fm_ref[...], vm_ref[...], lb)

    f = lb + (1.0 - lb) * _sigmoid(f_ref[0].astype(F32))
    g = jnp.log(f)
    g1 = g.astype(BF16)
    r1 = g - g1.astype(F32)
    g2 = r1.astype(BF16)
    g3 = (r1 - g2.astype(F32)).astype(BF16)
    gcat = jnp.concatenate([g1, g2, g3], axis=1)
    r = lax.broadcasted_iota(jnp.int32, (slab, slab), 0)
    c = lax.broadcasted_iota(jnp.int32, (slab, slab), 1)
    tri_bd = ((r // ck == c // ck) & (r >= c)).astype(BF16)
    for si in range(seq // slab):
        rows = slice(si * slab, (si + 1) * slab)
        t = dot(tri_bd, gcat[rows])
        b_ref[rows, :] = t[:, :HEAD_W] + t[:, HEAD_W:2 * HEAD_W] + t[:, 2 * HEAD_W:]

    k = 1.0 - f
    b = b_ref[...]
    qe_ref[...] = (q_ref[0].astype(F32) * jnp.exp(b)).astype(BF16)
    ke_ref[...] = (k * jnp.exp(-b)).astype(BF16)
    v = v_ref[0].astype(F32)
    vb_ref[...] = v.astype(BF16)
    vt_ref[...] = v.T.astype(BF16)
    for ci in range(n_chunks):
        rows = slice(ci * ck, (ci + 1) * ck)
        b_last = b_ref[(ci + 1) * ck - 1:(ci + 1) * ck, :]
        dec_ref[ci] = jnp.exp(b_last)
        kd_ref[rows, :] = (k[rows] * jnp.exp(b_last - b_ref[rows, :])).astype(BF16)

    rp = lax.broadcasted_iota(jnp.int32, (pair, pair), 0)
    cp = lax.broadcasted_iota(jnp.int32, (pair, pair), 1)
    keep = (rp // ck == cp // ck) & (rp >= cp)
    for pi in range(seq // pair):
        rows = slice(pi * pair, (pi + 1) * pair)
        a_ref[rows, :] = jnp.where(keep, _dot_nt(qe_ref[rows, :], ke_ref[rows, :]), 0.0).astype(BF16)

    for pi in range(seq // pair):
        rows = slice(pi * pair, (pi + 1) * pair)
        o_ref[0, rows, :] = dot(a_ref[rows, :], vb_ref[rows, :])
    for ci in range(n_chunks):
        rows = slice(ci * ck, (ci + 1) * ck)
        ut_ref[ci] = dot(vt_ref[:, rows], kd_ref[rows, :])

    for ci in range(n_chunks):
        st_ref[ci] = st.astype(BF16)
        st = st * dec_ref[ci] + ut_ref[ci]
    s_ref[0, 0] = st.T

    for ci in range(n_chunks):
        rows = slice(ci * ck, (ci + 1) * ck)
        o_ref[0, rows, :] += _dot_nt(qe_ref[rows, :], st_ref[ci])


def _hgrn_prompt(z_lo, z_small, hgrn_lb, batch, seq, heads):
    zp3 = z_lo.reshape(z_lo.shape[0], batch, seq, GROUP_W)
    big = lambda g: pl.BlockSpec((None, 1, seq, HEAD_W), lambda b, h, g=g: (LO_SLOT[g], b, 0, h))
    meta = lambda g: pl.BlockSpec((None, N_META, HEAD_W), lambda b, h, g=g: (g, 0, h))
    return pl.pallas_call(
        _hgrn_prompt_kernel,
        grid=(batch, heads),
        in_specs=[big(G_HQ), big(G_HF), big(G_HI), meta(G_HQ), meta(G_HF), meta(G_HI),
                  pl.BlockSpec((hgrn_lb.shape[0], HEAD_W), lambda b, h: (0, h))],
        out_specs=[pl.BlockSpec((1, seq, HEAD_W), lambda b, h: (b, 0, h)),
                   pl.BlockSpec((1, 1, HEAD_W, HEAD_W), lambda b, h: (b, h, 0, 0))],
        out_shape=[jax.ShapeDtypeStruct((batch, seq, heads * HEAD_W), F32),
                   jax.ShapeDtypeStruct((batch, heads, HEAD_W, HEAD_W), F32)],
        scratch_shapes=[pltpu.VMEM((seq, HEAD_W), F32),
                        pltpu.VMEM((seq, HEAD_W), BF16), pltpu.VMEM((seq, HEAD_W), BF16),
                        pltpu.VMEM((seq, HEAD_W), BF16), pltpu.VMEM((seq, HEAD_W), BF16),
                        pltpu.VMEM((HEAD_W, seq), BF16), pltpu.VMEM((seq, HGRN_PAIR), BF16),
                        pltpu.VMEM((seq // HGRN_CHUNK, HEAD_W, HEAD_W), F32),
                        pltpu.VMEM((seq // HGRN_CHUNK, 1, HEAD_W), F32),
                        pltpu.VMEM((seq // HGRN_CHUNK, HEAD_W, HEAD_W), BF16)],
        compiler_params=pltpu.CompilerParams(dimension_semantics=("arbitrary", "arbitrary"),
                                             vmem_limit_bytes=VMEM_LIMIT),
        name="hgrn_prompt",
    )(zp3, zp3, zp3, z_small, z_small, z_small, hgrn_lb)


def _diff_lambda(lp):
    s01 = jnp.sum(lp[0:1] * lp[1:2], axis=-1, keepdims=True)
    s23 = jnp.sum(lp[2:3] * lp[3:4], axis=-1, keepdims=True)
    return jnp.exp(s01) - jnp.exp(s23) + LAMBDA_INIT


def _bias_by_distance(dist, bias_of_bucket):
    out = jnp.where(dist >= BUCKET_THR[1], bias_of_bucket(1), bias_of_bucket(0))
    for bkt in range(2, N_BUCKETS):
        out = jnp.where(dist >= BUCKET_THR[bkt], bias_of_bucket(bkt), out)
    return out


ATT_TQ = 256
ATT_AHEAD = 2


def _colmax(s):
    return jnp.max(s.reshape(s.shape[0] // 8, 8, s.shape[1]), axis=0)


def _colsum(s):
    return jnp.sum(s.reshape(s.shape[0] // 8, 8, s.shape[1]), axis=0)


def _assemble_kv_block(main_ref, lead_refs, meta_ref, out_ref, first):
    lead = jnp.where(first, meta_ref[...], jnp.concatenate([r[0] for r in lead_refs], axis=0))
    rows = jnp.concatenate([lead, main_ref[0, :out_ref.shape[1] - N_META, :]], axis=0)
    out_ref[...] = rows.reshape(out_ref.shape)


def _attn_prompt_kernel(rb_ref, lp_ref, q_ref, k_ref, v_ref, km_ref, vm_ref,
                        ka_ref, kl0_ref, kl1_ref, kma_ref, va_ref, vl0_ref, vl1_ref, vma_ref,
                        o_ref, ko_ref, vo_ref,
                        kb_ref, vt_ref, vmt_ref, s_ref, p_ref, bdiag_ref, bnear_ref, bmeta_ref):
    h = pl.program_id(0)
    b = pl.program_id(1)
    tq = ATT_TQ
    n_q = q_ref.shape[1] // tq
    log2e = 1.0 / math.log(2.0)
    bias_of_bucket = lambda bkt: rb_ref[bkt * rb_ref.shape[0] // N_BUCKETS + h] * log2e
    far_bias = bias_of_bucket(N_BUCKETS - 1)
    key = lax.broadcasted_iota(jnp.int32, (tq, tq), 0)
    qry = lax.broadcasted_iota(jnp.int32, (tq, tq), 1)

    @pl.when(b == 0)
    def _():
        bdiag_ref[...] = _bias_by_distance(qry - key, bias_of_bucket)
        bnear_ref[...] = _bias_by_distance(qry - key + tq, bias_of_bucket)
        keym = lax.broadcasted_iota(jnp.int32, (N_META, tq), 0)
        qrym = lax.broadcasted_iota(jnp.int32, (N_META, tq), 1)
        bmeta_ref[...] = _bias_by_distance(qrym + N_META - keym, bias_of_bucket)

    kb_ref[...] = k_ref[0].astype(BF16)
    vt_ref[...] = v_ref[0].T.astype(BF16)
    pad = jnp.zeros((HEAD_W - N_META, HEAD_W), F32)
    vmt_ref[...] = jnp.concatenate([vm_ref[...], pad], axis=0).T.astype(BF16)
    km_bf = km_ref[...].astype(BF16)
    lam = _diff_lambda(lp_ref[...])
    causal = qry >= key
    dot = functools.partial(jnp.dot, preferred_element_type=F32)

    def pass_a(qi):
        par = qi % (ATT_AHEAD + 1)
        q = q_ref[0, qi * tq:(qi + 1) * tq, :].astype(F32) * (AT_DQK ** -0.5 * log2e)
        lane = lax.broadcasted_iota(jnp.int32, q.shape, 1)
        qc = (jnp.where(lane < AT_DQK, q, 0.0).astype(BF16), jnp.where(lane >= AT_DQK, q, 0.0).astype(BF16))
        n_keys = (qi + 1) * tq
        out = []
        for c in range(2):
            s_ref[par, c, :n_keys, :] = _dot_nt(kb_ref[:n_keys, :], qc[c])
            sm = _dot_nt(km_bf, qc[c]) + (bmeta_ref[...] if qi == 0 else far_bias)
            m8 = _colmax(sm)
            for j in range(qi + 1):
                m8 = jnp.maximum(m8, _colmax(biased(s_ref[par, c, j * tq:(j + 1) * tq, :], j, qi))
                                 + (far_bias if j < qi - 1 else 0.0))
            out.append((sm, jnp.max(m8, axis=0, keepdims=True)))
        return out

    def biased(s, j, qi):
        if j == qi:
            return jnp.where(causal, s + bdiag_ref[...], NEG_INF)
        if j == qi - 1:
            return s + bnear_ref[...]
        return s

    def pass_b(qi, stats):
        par = qi % (ATT_AHEAD + 1)
        n_keys = (qi + 1) * tq
        o_c = []
        for c in range(2):
            sm, m = stats[c]
            m_far = m - far_bias
            pm = jnp.exp2(sm - m)
            l8 = _colsum(pm)
            for j in range(qi + 1):
                rows = slice(j * tq, (j + 1) * tq)
                p = jnp.exp2(biased(s_ref[par, c, rows, :], j, qi) - (m_far if j < qi - 1 else m))
                l8 = l8 + _colsum(p)
                p_ref[qi % 2, c, rows, :] = p.astype(BF16)
            acc = dot(vt_ref[:, :n_keys], p_ref[qi % 2, c, :n_keys, :]) + dot(vmt_ref[:, :N_META], pm.astype(BF16))
            o_c.append(acc / jnp.sum(l8, axis=0, keepdims=True))
        o_ref[0, qi * tq:(qi + 1) * tq, :] = (o_c[0] - lam * o_c[1]).T

    stats = [pass_a(qi) for qi in range(min(ATT_AHEAD, n_q))]
    for qi in range(n_q):
        if qi + ATT_AHEAD < n_q:
            stats.append(pass_a(qi + ATT_AHEAD))
        pass_b(qi, stats.pop(0))

    _assemble_kv_block(ka_ref, (kl0_ref, kl1_ref), kma_ref, ko_ref, h == 0)
    _assemble_kv_block(va_ref, (vl0_ref, vl1_ref), vma_ref, vo_ref, h == 0)


def _attn_prompt(z_lo, z_kv, z_small, rel_bias, diff_lambda, batch, seq, heads):
    assert CONST_BIAS_DIST <= ATT_TQ + 1 and CONST_BIAS_DIST <= N_META + 1 + ATT_TQ - N_META
    zlo = z_lo.reshape(z_lo.shape[0], batch, seq, GROUP_W)
    zkv = z_kv.reshape(z_kv.shape[0], batch, seq, GROUP_W)
    tq = ATT_TQ
    full = lambda slot: pl.BlockSpec((None, 1, seq, HEAD_W), lambda h, b, slot=slot: (slot, b, 0, h))
    meta = lambda g: pl.BlockSpec((None, N_META, HEAD_W), lambda h, b, g=g: (g, 0, h))

    tokens = N_META + seq
    blk = -(-tokens // (8 * heads)) * 8
    assert N_META == 16 and (heads - 1) * blk < seq
    asm_main = lambda g: pl.BlockSpec((None, 1, blk, GROUP_W), lambda h, b, g=g: (KV_SLOT[g], b, h, 0))
    asm_lead = lambda g, piece: pl.BlockSpec(
        (None, 1, 8, GROUP_W),
        lambda h, b, g=g, piece=piece: (KV_SLOT[g], b, jnp.maximum(h * (blk // 8) - 2 + piece, 0), 0))
    asm_meta = lambda g: pl.BlockSpec((None, N_META, GROUP_W), lambda h, b, g=g: (g, 0, 0))
    asm_in = lambda g: [asm_main(g), asm_lead(g, 0), asm_lead(g, 1), asm_meta(g)]
    asm_out = pl.BlockSpec((1, blk, heads, HEAD_W), lambda h, b: (b, h, 0, 0))
    kv_shape = jax.ShapeDtypeStruct((batch, tokens, heads, HEAD_W), F32)

    return pl.pallas_call(
        _attn_prompt_kernel,
        grid=(heads, batch),
        in_specs=[pl.BlockSpec(memory_space=pltpu.SMEM),
                  pl.BlockSpec(diff_lambda.shape, lambda h, b: (0, 0)),
                  full(LO_SLOT[G_AQ]), full(KV_SLOT[G_AK]), full(KV_SLOT[G_AV]), meta(G_AK), meta(G_AV)]
                 + asm_in(G_AK) + asm_in(G_AV),
        out_specs=[pl.BlockSpec((1, seq, HEAD_W), lambda h, b: (b, 0, h)), asm_out, asm_out],
        out_shape=[jax.ShapeDtypeStruct((batch, seq, heads * HEAD_W), F32), kv_shape, kv_shape],
        scratch_shapes=[pltpu.VMEM((seq, HEAD_W), BF16), pltpu.VMEM((HEAD_W, seq), BF16),
                        pltpu.VMEM((HEAD_W, HEAD_W), BF16),
                        pltpu.VMEM((ATT_AHEAD + 1, 2, seq, tq), F32), pltpu.VMEM((2, 2, seq, tq), BF16),
                        pltpu.VMEM((tq, tq), F32), pltpu.VMEM((tq, tq), F32),
                        pltpu.VMEM((N_META, tq), F32)],
        compiler_params=pltpu.CompilerParams(dimension_semantics=("arbitrary", "arbitrary"),
                                             vmem_limit_bytes=VMEM_LIMIT),
        name="attn_prompt",
    )(rel_bias.reshape(-1), diff_lambda, zlo, zkv, zkv, z_small, z_small,
      zkv, zkv, zkv, z_small, zkv, zkv, zkv, z_small)


def _head_rmsnorm(o, g):
    parts = []
    for hh in range(o.shape[1] // HEAD_W):
        oh = o[:, hh * HEAD_W:(hh + 1) * HEAD_W]
        ms = jnp.mean(oh * oh, axis=-1, keepdims=True)
        parts.append(oh * lax.rsqrt(ms + RMS_EPS) * g)
    return jnp.concatenate(parts, axis=1)


MERGE_SUB = 256


def _merge_kernel(ohg_ref, hg_ref, oat_ref, ag_ref, x_ref, gh_ref, ga_ref, w_ref, gf_ref, y_ref):
    tm = x_ref.shape[0]
    sub = min(MERGE_SUB, tm)
    subs = [slice(s, s + sub) for s in range(0, tm, sub)]
    mixed = []
    for r in subs:
        mix_h = _head_rmsnorm(ohg_ref[r, :], gh_ref[...]) * _silu(hg_ref[r, :].astype(F32))
        mix_a = (_head_rmsnorm(oat_ref[r, :], ga_ref[...]) * (1.0 - LAMBDA_INIT)
                 * _silu(ag_ref[r, :].astype(F32)))
        mixed.append(jnp.concatenate([mix_h, mix_a], axis=1).astype(BF16))
    proj = [jnp.dot(m, w_ref[...], preferred_element_type=F32) for m in mixed]
    for r, p in zip(subs, proj):
        hres = x_ref[r, :] + p
        ms = jnp.mean(hres * hres, axis=-1, keepdims=True)
        y_ref[r, :] = hres * lax.rsqrt(ms + RMS_EPS) * gf_ref[...]


def _merge(o_hg, o_at, z, slot_hg, slot_ag, x, g_hg, g_at, w_out_bf, g_final, tm, heads):
    rows, d = x.shape
    gw = heads * HEAD_W
    row_blk = lambda w: pl.BlockSpec((tm, w), lambda i: (i, 0))
    gate = lambda slot: pl.BlockSpec((None, tm, gw), lambda i, slot=slot: (slot, i, 0))
    const = lambda shape: pl.BlockSpec(shape, lambda i: (0, 0))
    return pl.pallas_call(
        _merge_kernel,
        grid=(rows // tm,),
        in_specs=[row_blk(gw), gate(slot_hg), row_blk(gw), gate(slot_ag), row_blk(d),
                  const((1, HEAD_W)), const((1, HEAD_W)),
                  pl.BlockSpec(w_out_bf.shape, lambda i: (0, 0), pipeline_mode=pl.Buffered(1)),
                  const((1, d))],
        out_specs=row_blk(d),
        out_shape=jax.ShapeDtypeStruct((rows, d), F32),
        compiler_params=pltpu.CompilerParams(dimension_semantics=("arbitrary",),
                                             vmem_limit_bytes=VMEM_LIMIT),
        name="merge",
    )(o_hg, z, o_at, z, x, g_hg, g_at, w_out_bf, g_final)


def _transpose_rows_to_cols(x):
    pad = jnp.zeros((HEAD_W - x.shape[0], HEAD_W), F32)
    return jnp.concatenate([x, pad], axis=0).T


def _hgrn_step_kernel(q_ref, f_ref, v_ref, lbp_ref, s_ref, w_ref, o_ref, sn_ref, wb_ref):
    wb_ref[...] = w_ref[...].astype(BF16)
    lb = _forget_lower_bound(lbp_ref[...])
    f = lb + (1.0 - lb) * _sigmoid(f_ref[...])
    v = v_ref[...]
    f_t = _transpose_rows_to_cols(f)
    q_t = _transpose_rows_to_cols(q_ref[...])
    for i in range(q_ref.shape[0]):
        fi = jnp.broadcast_to(f_t[:, i:i + 1], (HEAD_W, HEAD_W))
        s_new = fi * s_ref[0, i, 0] + (1.0 - fi) * v[i:i + 1]
        sn_ref[0, i, 0] = s_new
        o_ref[i:i + 1, :] = jnp.sum(q_t[:, i:i + 1] * s_new, axis=0, keepdims=True)


def _hgrn_step(z_samp, hgrn_lb, state, w_out, heads):
    db = z_samp.shape[1]
    d = w_out.shape[0]
    col = lambda g: pl.BlockSpec((None, db, HEAD_W), lambda h, g=g: (g, 0, h))
    st_spec = pl.BlockSpec((1, db, 1, HEAD_W, HEAD_W), lambda h: (0, 0, h, 0, 0))
    w_spec = pl.BlockSpec((d // heads, w_out.shape[1]), lambda h: (h, 0))
    return pl.pallas_call(
        _hgrn_step_kernel,
        grid=(heads,),
        in_specs=[col(G_HQ), col(G_HF), col(G_HI),
                  pl.BlockSpec((hgrn_lb.shape[0], HEAD_W), lambda h: (0, h)), st_spec, w_spec],
        out_specs=[pl.BlockSpec((db, HEAD_W), lambda h: (0, h)), st_spec, w_spec],
        out_shape=[jax.ShapeDtypeStruct((db, heads * HEAD_W), F32),
                   jax.ShapeDtypeStruct(state.shape, F32),
                   jax.ShapeDtypeStruct(w_out.shape, BF16)],
        compiler_params=pltpu.CompilerParams(dimension_semantics=("arbitrary",),
                                             vmem_limit_bytes=VMEM_LIMIT),
        name="hgrn_step",
    )(z_samp, z_samp, z_samp, hgrn_lb, state, w_out)


DEC_PAGES_PER_STEP = 8
DEC_RING = 3


def _page_copies(pt_ref, ck_ref, cv_ref, kbuf_ref, vbuf_ref, sem_ref, step, n_steps):
    npp = DEC_PAGES_PER_STEP
    slot = step % DEC_RING
    seq = step // n_steps
    first_page = (step % n_steps) * npp
    copies = []
    for p in range(npp):
        phys = pt_ref[seq, first_page + p]
        copies.append(pltpu.make_async_copy(ck_ref.at[phys], kbuf_ref.at[slot, p], sem_ref.at[slot]))
        copies.append(pltpu.make_async_copy(cv_ref.at[phys], vbuf_ref.at[slot, p], sem_ref.at[slot]))
    return copies


def _proj_decode_kernel(pt_ref, x_ref, g_ref, w_ref, rbt_ref, lp_ref, q_ref, kn_ref, vn_ref, ck_ref, cv_ref,
                        z_ref, zkv_ref, o_ref, xn_ref, ztile_ref, qs_ref, m_ref, l_ref, acc_ref, blast_ref,
                        kbuf_ref, vbuf_ref, sem_ref, *, heads, page, n_steps):
    npp = DEC_PAGES_PER_STEP
    col_tile = pl.program_id(1)
    row_sub = pl.program_id(2)
    step = (pl.program_id(0) * pl.num_programs(1) + col_tile) * pl.num_programs(2) + row_sub
    total_steps = pl.num_programs(0) * pl.num_programs(1) * pl.num_programs(2)
    copies_for = lambda s: _page_copies(pt_ref, ck_ref, cv_ref, kbuf_ref, vbuf_ref, sem_ref, s, n_steps)

    def start_all(copies):
        for cp in copies:
            cp.start()

    @pl.when(step == 0)
    def _():
        for s in range(DEC_RING - 1):
            start_all(copies_for(s))

    @pl.when(step + (DEC_RING - 1) < total_steps)
    def _():
        start_all(copies_for(step + (DEC_RING - 1)))

    @pl.when(col_tile == 0)
    def _():
        x = x_ref[...]
        ms = jnp.mean(x * x, axis=-1, keepdims=True)
        xn_ref[row_sub] = (x * lax.rsqrt(ms + RMS_EPS) * g_ref[...]).astype(BF16)

    for cp in copies_for(step):
        cp.wait()
    slot = step % DEC_RING
    k_refs = [kbuf_ref.at[slot, p] for p in range(npp)]
    v_refs = [vbuf_ref.at[slot, p] for p in range(npp)]

    j = step % n_steps
    rows = 2 * heads
    lanes = page * heads
    rbt = jnp.concatenate([rbt_ref[...], rbt_ref[...]], axis=0)
    far_bias = rbt[:, N_BUCKETS - 1:N_BUCKETS]
    r_id = lax.broadcasted_iota(jnp.int32, (rows, lanes), 0)
    l_id = lax.broadcasted_iota(jnp.int32, (rows, lanes), 1)
    own_head = l_id % heads == r_id % heads

    def stacked(x):
        lane = lax.broadcasted_iota(jnp.int32, x.shape, 1)
        return jnp.concatenate([jnp.where(lane < AT_DQK, x, 0.0), jnp.where(lane >= AT_DQK, x, 0.0)], axis=0)

    @pl.when(j == 0)
    def _():
        qs_ref[...] = stacked(q_ref[0] * (AT_DQK ** -0.5)).astype(BF16)
        m_ref[...] = jnp.full(m_ref.shape, NEG_INF, F32)
        l_ref[...] = jnp.zeros(l_ref.shape, F32)
        acc_ref[...] = jnp.zeros(acc_ref.shape, F32)
        blast_ref[...] = _bias_by_distance(page - l_id // heads, lambda bkt: rbt[:, bkt:bkt + 1])

    qs = qs_ref[...]
    s_parts = []
    for i in range(npp):
        s_i = _dot_nt(qs, k_refs[i][...].astype(BF16))
        if i == npp - 1:
            s_i = s_i + jnp.where(j == n_steps - 1, blast_ref[...], far_bias)
        else:
            s_i = s_i + far_bias
        s_parts.append(jnp.where(own_head, s_i, NEG_INF))

    m_old = m_ref[...]
    m_blk = s_parts[0]
    for s_i in s_parts[1:]:
        m_blk = jnp.maximum(m_blk, s_i)
    m_new = jnp.maximum(m_old, jnp.max(m_blk, axis=-1, keepdims=True))
    alpha = jnp.exp(m_old - m_new)

    ztile_ref[...] = jnp.dot(xn_ref[row_sub], w_ref[...], preferred_element_type=F32)

    l_blk = jnp.zeros((rows, lanes), F32)
    pv = jnp.zeros(acc_ref.shape, F32)
    for i in range(npp):
        p_i = jnp.exp(s_parts[i] - m_new)
        l_blk = l_blk + p_i
        pv = pv + jnp.dot(p_i.astype(BF16), v_refs[i][...].astype(BF16), preferred_element_type=F32)
    l_ref[...] = alpha * l_ref[...] + jnp.sum(l_blk, axis=-1, keepdims=True)
    acc_ref[...] = alpha * acc_ref[...] + pv
    m_ref[...] = m_new

    @pl.when(j == n_steps - 1)
    def _():
        kn = jnp.concatenate([kn_ref[0], kn_ref[0]], axis=0)
        vn = jnp.concatenate([vn_ref[0], vn_ref[0]], axis=0)
        s_new = jnp.sum(qs.astype(F32) * kn, axis=-1, keepdims=True) + rbt[:, 0:1]
        m_old = m_ref[...]
        m_fin = jnp.maximum(m_old, s_new)
        alpha = jnp.exp(m_old - m_fin)
        p_new = jnp.exp(s_new - m_fin)
        l_fin = alpha * l_ref[...] + p_new
        o = (alpha * acc_ref[...] + p_new * vn) / l_fin
        o_ref[0] = o[:heads] - _diff_lambda(lp_ref[...]) * o[heads:]

    group = col_tile // (GROUP_W // z_ref.shape[1])
    is_kv = (group == G_AK) | (group == G_AV)

    @pl.when(is_kv)
    def _():
        zkv_ref[...] = ztile_ref[...]

    @pl.when(jnp.logical_not(is_kv))
    def _():
        z_ref[...] = ztile_ref[...].astype(BF16)


PD_TM = 1024
PD_TN = 256
PD_ROW_TILES_PER_W = 2


def _proj_decode(x, g, w_bf, q_s, k_s, v_s, cache_k, cache_v, page_table, rel_bias, diff_lambda):
    assert (G_AK, G_AV) == (G_AQ + 1, G_AQ + 2)
    rows, d = x.shape
    n = w_bf.shape[1]
    db, n_pages = page_table.shape
    n_phys, page, heads, _ = cache_k.shape
    npp = DEC_PAGES_PER_STEP
    n_steps = n_pages // npp
    nr = PD_ROW_TILES_PER_W
    grid = (rows // (PD_TM * nr), n // PD_TN, nr)
    tiles_per_group = GROUP_W // PD_TN
    assert CONST_BIAS_DIST <= page + 1 and n_pages % npp == 0 and GROUP_W % PD_TN == 0
    assert rows % (PD_TM * nr) == 0 and n % PD_TN == 0 and grid[0] * grid[1] * nr == db * n_steps
    ck = cache_k.reshape(n_phys, page * heads, HEAD_W)
    cv = cache_v.reshape(n_phys, page * heads, HEAD_W)
    seq_of = lambda p, t, r: ((p * grid[1] + t) * nr + r) // n_steps
    vec = pl.BlockSpec((1, heads, HEAD_W), lambda p, t, r, pt: (seq_of(p, t, r), 0, 0))
    hbm = pl.BlockSpec(memory_space=pl.ANY)
    srows = 2 * heads

    x_block = lambda p, t, r, pt: (jnp.where(t == 0, p * nr + r, p * nr + nr - 1), 0)

    last = tiles_per_group - 1

    def lo_block(p, t, r, pt):
        grp, col = t // tiles_per_group, t % tiles_per_group
        kv = (grp == G_AK) | (grp == G_AV)
        slot = jnp.where(grp < G_AK, grp, jnp.where(grp > G_AV, grp - 2, G_AK - 1))
        return slot, jnp.where(kv, p * nr + nr - 1, p * nr + r), jnp.where(kv, last, col)

    def kv_block(p, t, r, pt):
        grp, col = t // tiles_per_group, t % tiles_per_group
        slot = jnp.clip(grp - G_AK, 0, 1)
        row = jnp.where(grp < G_AK, p * nr, jnp.where(grp > G_AV, p * nr + nr - 1, p * nr + r))
        return slot, row, jnp.where(grp < G_AK, 0, jnp.where(grp > G_AV, last, col))

    kernel = functools.partial(_proj_decode_kernel, heads=heads, page=page, n_steps=n_steps)
    return pl.pallas_call(
        kernel,
        grid_spec=pltpu.PrefetchScalarGridSpec(
            num_scalar_prefetch=1,
            grid=grid,
            in_specs=[pl.BlockSpec((PD_TM, d), x_block),
                      pl.BlockSpec((1, d), lambda p, t, r, pt: (0, 0)),
                      pl.BlockSpec((d, PD_TN), lambda p, t, r, pt: (0, t)),
                      pl.BlockSpec((heads, N_BUCKETS), lambda p, t, r, pt: (0, 0)),
                      pl.BlockSpec(diff_lambda.shape, lambda p, t, r, pt: (0, 0)),
                      vec, vec, vec, hbm, hbm],
            out_specs=[pl.BlockSpec((None, PD_TM, PD_TN), lo_block),
                       pl.BlockSpec((None, PD_TM, PD_TN), kv_block),
                       pl.BlockSpec((1, heads, HEAD_W), lambda p, t, r, pt: (seq_of(p, t, r), 0, 0))],
            scratch_shapes=[pltpu.VMEM((nr, PD_TM, d), BF16), pltpu.VMEM((PD_TM, PD_TN), F32),
                            pltpu.VMEM((srows, HEAD_W), BF16), pltpu.VMEM((srows, 1), F32),
                            pltpu.VMEM((srows, 1), F32), pltpu.VMEM((srows, HEAD_W), F32),
                            pltpu.VMEM((srows, page * heads), F32),
                            pltpu.VMEM((DEC_RING, npp, page * heads, HEAD_W), F32),
                            pltpu.VMEM((DEC_RING, npp, page * heads, HEAD_W), F32),
                            pltpu.SemaphoreType.DMA((DEC_RING,))]),
        out_shape=[jax.ShapeDtypeStruct((n // GROUP_W - 2, rows, GROUP_W), BF16),
                   jax.ShapeDtypeStruct((2, rows, GROUP_W), F32),
                   jax.ShapeDtypeStruct((db, heads, HEAD_W), F32)],
        compiler_params=pltpu.CompilerParams(dimension_semantics=("arbitrary", "arbitrary", "arbitrary"),
                                             vmem_limit_bytes=VMEM_LIMIT),
        name="proj_decode",
    )(page_table, x, g, w_bf, rel_bias.T, diff_lambda, q_s, k_s, v_s, ck, cv)


def kernel(x_prompt, x_sample, cache_k, cache_v, state_hgrn, page_table, meta_tokens, rel_bias, hgrn_lb,
           norm_g, w_in, hgrn_norm_g, diff_norm_g, diff_lambda, w_out, final_norm_g):
    batch, seq, d = x_prompt.shape
    db = x_sample.shape[0]
    heads = cache_k.shape[3]
    gw = heads * HEAD_W
    assert norm_g.shape[0] == 1 and x_sample.shape[1] == 1, "one layer, one sample token per sequence"
    assert gw == GROUP_W

    g_in = norm_g[0].reshape(1, d)
    g_final = final_norm_g.reshape(1, d)
    g_hg = hgrn_norm_g[0].reshape(1, HEAD_W)
    g_at = diff_norm_g[0].reshape(1, HEAD_W)
    lam_p = diff_lambda[0]

    xp = x_prompt.reshape(batch * seq, d)
    xs = x_sample.reshape(db, d)
    small = jnp.concatenate([meta_tokens.astype(F32), xs], axis=0)

    z_small, w_in_bf = _proj_small(small, g_in, w_in[0])
    z_samp = z_small[:, N_META:]
    samp_heads = lambda g: z_samp[g].reshape(db, heads, HEAD_W)
    q_s, k_s, v_s = samp_heads(G_AQ), samp_heads(G_AK), samp_heads(G_AV)
    o_hg_s, s_s, w_out_bf = _hgrn_step(z_samp, hgrn_lb, state_hgrn, w_out[0], heads)

    z_lo, z_kv, o_at_s = _proj_decode(xp, g_in, w_in_bf, q_s, k_s, v_s, cache_k[0], cache_v[0], page_table,
                                      rel_bias, lam_p)

    o_hg_p, s_p = _hgrn_prompt(z_lo, z_small, hgrn_lb, batch, seq, heads)
    o_at_p, k_p, v_p = _attn_prompt(z_lo, z_kv, z_small, rel_bias, lam_p, batch, seq, heads)
    y_p = _merge(o_hg_p.reshape(batch * seq, gw), o_at_p.reshape(batch * seq, gw), z_lo, LO_SLOT[G_HG],
                 LO_SLOT[G_AG], xp, g_hg, g_at, w_out_bf, g_final, tm=512, heads=heads)

    y_s = _merge(o_hg_s, o_at_s.reshape(db, gw), z_samp, G_HG, G_AG, xs, g_hg, g_at, w_out_bf, g_final,
                 tm=db, heads=heads)

    return (y_p.reshape(batch, seq, d), y_s.reshape(db, 1, d), k_p[None], v_p[None],
            s_p[None], k_s.reshape(1, db, 1, heads, HEAD_W), v_s.reshape(1, db, 1, heads, HEAD_W), s_s)
```

```python
import functools
import math

import numpy as np
import jax
import jax.numpy as jnp
from jax import lax
from jax.experimental import pallas as pl
from jax.experimental.pallas import tpu as pltpu

F32 = jnp.float32
BF16 = jnp.bfloat16

N_META = 16
HEAD_W = 128
AT_DQK = 64
N_BUCKETS = 32
MAX_DISTANCE = 128
MAX_EXACT = N_BUCKETS // 2
HGRN_CHUNK = 64
RMS_EPS = 1e-6
NEG_INF = -1e30
LAMBDA_INIT = 0.8 - 0.6 * math.exp(-0.3 * 0)
G_HQ, G_HF, G_HI, G_HG, G_AQ, G_AK, G_AV, G_AG = range(8)
KV_SLOT = {G_AK: 0, G_AV: 1}
LO_SLOT = {g: (g if g < G_AK else g - 2) for g in range(8) if g not in KV_SLOT}

VMEM_LIMIT = 56 * 1024 * 1024


def _bucket_thresholds():
    n = np.arange(0, 2 * MAX_DISTANCE)
    nl = np.maximum(n, MAX_EXACT).astype(np.float32)
    large = MAX_EXACT + (np.log(nl / np.float32(MAX_EXACT)) / np.float32(math.log(MAX_DISTANCE / MAX_EXACT))
                         * np.float32(N_BUCKETS - MAX_EXACT)).astype(np.int32)
    bucket = np.where(n < MAX_EXACT, n, np.minimum(large, N_BUCKETS - 1))
    return [int(np.argmax(bucket >= b)) for b in range(N_BUCKETS)]


BUCKET_THR = _bucket_thresholds()
CONST_BIAS_DIST = BUCKET_THR[N_BUCKETS - 1]


def _sigmoid(x):
    return 1.0 / (1.0 + jnp.exp(-x))


def _silu(x):
    return x * _sigmoid(x)


def _dot_nt(a, b):
    return lax.dot_general(a, b, (((1,), (1,)), ((), ())), preferred_element_type=F32)


def _dot_tn(a, b):
    return lax.dot_general(a, b, (((0,), (0,)), ((), ())), preferred_element_type=F32)


GROUP_W = 1024


def _proj_small_kernel(x_ref, g_ref, w_ref, z_ref, wb_ref, xn_ref):
    @pl.when(pl.program_id(0) == 0)
    def _():
        x = x_ref[...]
        ms = jnp.mean(x * x, axis=-1, keepdims=True)
        xn_ref[...] = (x * lax.rsqrt(ms + RMS_EPS) * g_ref[...]).astype(BF16)

    w_bf = w_ref[...].astype(BF16)
    wb_ref[...] = w_bf
    z_ref[...] = jnp.dot(xn_ref[...], w_bf, preferred_element_type=F32)


def _proj_small(x, g, w):
    rows, d = x.shape
    n = w.shape[1]
    return pl.pallas_call(
        _proj_small_kernel,
        grid=(n // GROUP_W,),
        in_specs=[pl.BlockSpec((rows, d), lambda j: (0, 0)),
                  pl.BlockSpec((1, d), lambda j: (0, 0)),
                  pl.BlockSpec((d, GROUP_W), lambda j: (0, j))],
        out_specs=[pl.BlockSpec((None, rows, GROUP_W), lambda j: (j, 0, 0)),
                   pl.BlockSpec((d, GROUP_W), lambda j: (0, j))],
        out_shape=[jax.ShapeDtypeStruct((n // GROUP_W, rows, GROUP_W), F32),
                   jax.ShapeDtypeStruct((d, n), BF16)],
        scratch_shapes=[pltpu.VMEM((rows, d), BF16)],
        compiler_params=pltpu.CompilerParams(dimension_semantics=("arbitrary",),
                                             vmem_limit_bytes=VMEM_LIMIT),
        name="proj_small",
    )(x, g, w)


def _forget_lower_bound(lbp):
    e = jnp.exp(lbp - jnp.max(lbp, axis=0, keepdims=True))
    return e[0:1] / jnp.sum(e, axis=0, keepdims=True)


def _cumsum_rows(g, tri_bf):
    g1 = g.astype(BF16)
    r1 = g - g1.astype(F32)
    g2 = r1.astype(BF16)
    g3 = (r1 - g2.astype(F32)).astype(BF16)
    dot = functools.partial(jnp.dot, preferred_element_type=F32)
    return dot(tri_bf, g1) + dot(tri_bf, g2) + dot(tri_bf, g3)


def _hgrn_chunk_local(q, hf, v, lb):
    c = q.shape[0]
    row = lax.broadcasted_iota(jnp.int32, (c, c), 0)
    col = lax.broadcasted_iota(jnp.int32, (c, c), 1)
    causal = row >= col
    f = lb + (1.0 - lb) * _sigmoid(hf)
    k = 1.0 - f
    b = _cumsum_rows(jnp.log(f), causal.astype(BF16))
    qe = (q * jnp.exp(b)).astype(BF16)
    ke = (k * jnp.exp(-b)).astype(BF16)
    a = jnp.where(causal, _dot_nt(qe, ke), 0.0)
    v_bf = v.astype(BF16)
    intra = jnp.dot(a.astype(BF16), v_bf, preferred_element_type=F32)
    b_last = b[c - 1:c]
    kd = (k * jnp.exp(b_last - b)).astype(BF16)
    return qe, intra, _dot_tn(v_bf, kd), jnp.exp(b_last)


HGRN_SLAB = 256
HGRN_PAIR = 128


def _hgrn_prompt_kernel(q_ref, f_ref, v_ref, qm_ref, fm_ref, vm_ref, lbp_ref, o_ref, s_ref,
                        b_ref, qe_ref, ke_ref, kd_ref, vb_ref, vt_ref, a_ref, ut_ref, dec_ref, st_ref):
    ck, slab, pair = HGRN_CHUNK, HGRN_SLAB, HGRN_PAIR
    seq = q_ref.shape[1]
    n_chunks = seq // ck
    dot = functools.partial(jnp.dot, preferred_element_type=F32)
    lb = _forget_lower_bound(lbp_ref[...])

    _, _, st, _ = _hgrn_chunk_local(qm_ref[...], fm_ref[...], vm_ref[...], lb)

    f = lb + (1.0 - lb) * _sigmoid(f_ref[0].astype(F32))
    g = jnp.log(f)
    g1 = g.astype(BF16)
    r1 = g - g1.astype(F32)
    g2 = r1.astype(BF16)
    g3 = (r1 - g2.astype(F32)).astype(BF16)
    gcat = jnp.concatenate([g1, g2, g3], axis=1)
    r = lax.broadcasted_iota(jnp.int32, (slab, slab), 0)
    c = lax.broadcasted_iota(jnp.int32, (slab, slab), 1)
    tri_bd = ((r // ck == c // ck) & (r >= c)).astype(BF16)
    for si in range(seq // slab):
        rows = slice(si * slab, (si + 1) * slab)
        t = dot(tri_bd, gcat[rows])
        b_ref[rows, :] = t[:, :HEAD_W] + t[:, HEAD_W:2 * HEAD_W] + t[:, 2 * HEAD_W:]

    k = 1.0 - f
    b = b_ref[...]
    qe_ref[...] = (q_ref[0].astype(F32) * jnp.exp(b)).astype(BF16)
    ke_ref[...] = (k * jnp.exp(-b)).astype(BF16)
    v = v_ref[0].astype(F32)
    vb_ref[...] = v.astype(BF16)
    vt_ref[...] = v.T.astype(BF16)
    for ci in range(n_chunks):
        rows = slice(ci * ck, (ci + 1) * ck)
        b_last = b_ref[(ci + 1) * ck - 1:(ci + 1) * ck, :]
        dec_ref[ci] = jnp.exp(b_last)
        kd_ref[rows, :] = (k[rows] * jnp.exp(b_last - b_ref[rows, :])).astype(BF16)

    rp = lax.broadcasted_iota(jnp.int32, (pair, pair), 0)
    cp = lax.broadcasted_iota(jnp.int32, (pair, pair), 1)
    keep = (rp // ck == cp // ck) & (rp >= cp)
    for pi in range(seq // pair):
        rows = slice(pi * pair, (pi + 1) * pair)
        a_ref[rows, :] = jnp.where(keep, _dot_nt(qe_ref[rows, :], ke_ref[rows, :]), 0.0).astype(BF16)

    for pi in range(seq // pair):
        rows = slice(pi * pair, (pi + 1) * pair)
        o_ref[0, rows, :] = dot(a_ref[rows, :], vb_ref[rows, :])
    for ci in range(n_chunks):
        rows = slice(ci * ck, (ci + 1) * ck)
        ut_ref[ci] = dot(vt_ref[:, rows], kd_ref[rows, :])

    for ci in range(n_chunks):
        st_ref[ci] = st.astype(BF16)
        st = st * dec_ref[ci] + ut_ref[ci]
    s_ref[0, 0] = st.T

    for ci in range(n_chunks):
        rows = slice(ci * ck, (ci + 1) * ck)
        o_ref[0, rows, :] += _dot_nt(qe_ref[rows, :], st_ref[ci])


def _hgrn_prompt_call_parts(z_lo, z_small, hgrn_lb, batch, seq, heads):
    zp3 = z_lo.reshape(z_lo.shape[0], batch, seq, GROUP_W)
    big = lambda g: pl.BlockSpec((None, 1, seq, HEAD_W), lambda h, b, g=g: (LO_SLOT[g], b, 0, h))
    meta = lambda g: pl.BlockSpec((None, N_META, HEAD_W), lambda h, b, g=g: (g, 0, h))
    in_specs = [big(G_HQ), big(G_HF), big(G_HI), meta(G_HQ), meta(G_HF), meta(G_HI),
                pl.BlockSpec((hgrn_lb.shape[0], HEAD_W), lambda h, b: (0, h))]
    operands = [zp3, zp3, zp3, z_small, z_small, z_small, hgrn_lb]
    out_specs = [pl.BlockSpec((1, seq, HEAD_W), lambda h, b: (b, 0, h)),
                 pl.BlockSpec((1, 1, HEAD_W, HEAD_W), lambda h, b: (b, h, 0, 0))]
    out_shapes = [jax.ShapeDtypeStruct((batch, seq, heads * HEAD_W), F32),
                  jax.ShapeDtypeStruct((batch, heads, HEAD_W, HEAD_W), F32)]
    scratch = [pltpu.VMEM((seq, HEAD_W), F32),
               pltpu.VMEM((seq, HEAD_W), BF16), pltpu.VMEM((seq, HEAD_W), BF16),
               pltpu.VMEM((seq, HEAD_W), BF16), pltpu.VMEM((seq, HEAD_W), BF16),
               pltpu.VMEM((HEAD_W, seq), BF16), pltpu.VMEM((seq, HGRN_PAIR), BF16),
               pltpu.VMEM((seq // HGRN_CHUNK, HEAD_W, HEAD_W), F32),
               pltpu.VMEM((seq // HGRN_CHUNK, 1, HEAD_W), F32),
               pltpu.VMEM((seq // HGRN_CHUNK, HEAD_W, HEAD_W), BF16)]
    return in_specs, operands, out_specs, out_shapes, scratch


def _diff_lambda(lp):
    s01 = jnp.sum(lp[0:1] * lp[1:2], axis=-1, keepdims=True)
    s23 = jnp.sum(lp[2:3] * lp[3:4], axis=-1, keepdims=True)
    return jnp.exp(s01) - jnp.exp(s23) + LAMBDA_INIT


def _bias_by_distance(dist, bias_of_bucket):
    out = jnp.where(dist >= BUCKET_THR[1], bias_of_bucket(1), bias_of_bucket(0))
    for bkt in range(2, N_BUCKETS):
        out = jnp.where(dist >= BUCKET_THR[bkt], bias_of_bucket(bkt), out)
    return out


ATT_TQ = 256
ATT_AHEAD = 2


def _colmax(s):
    return jnp.max(s.reshape(s.shape[0] // 8, 8, s.shape[1]), axis=0)


def _colsum(s):
    return jnp.sum(s.reshape(s.shape[0] // 8, 8, s.shape[1]), axis=0)


def _assemble_kv_block(main_ref, lead_refs, meta_ref, out_ref, first):
    lead = jnp.where(first, meta_ref[...], jnp.concatenate([r[0] for r in lead_refs], axis=0))
    rows = jnp.concatenate([lead, main_ref[0, :out_ref.shape[1] - N_META, :]], axis=0)
    out_ref[...] = rows.reshape(out_ref.shape)


def _attn_prompt_kernel(rb_ref, lp_ref, q_ref, k_ref, v_ref, km_ref, vm_ref,
                        ka_ref, kl0_ref, kl1_ref, kma_ref, va_ref, vl0_ref, vl1_ref, vma_ref,
                        o_ref, ko_ref, vo_ref,
                        kb_ref, vt_ref, vmt_ref, s_ref, p_ref, bdiag_ref, bnear_ref, bmeta_ref):
    h = pl.program_id(0)
    b = pl.program_id(1)
    tq = ATT_TQ
    n_q = q_ref.shape[1] // tq
    log2e = 1.0 / math.log(2.0)
    bias_of_bucket = lambda bkt: rb_ref[bkt * rb_ref.shape[0] // N_BUCKETS + h] * log2e
    far_bias = bias_of_bucket(N_BUCKETS - 1)
    key = lax.broadcasted_iota(jnp.int32, (tq, tq), 0)
    qry = lax.broadcasted_iota(jnp.int32, (tq, tq), 1)

    @pl.when(b == 0)
    def _():
        bdiag_ref[...] = _bias_by_distance(qry - key, bias_of_bucket)
        bnear_ref[...] = _bias_by_distance(qry - key + tq, bias_of_bucket)
        keym = lax.broadcasted_iota(jnp.int32, (N_META, tq), 0)
        qrym = lax.broadcasted_iota(jnp.int32, (N_META, tq), 1)
        bmeta_ref[...] = _bias_by_distance(qrym + N_META - keym, bias_of_bucket)

    kb_ref[...] = k_ref[0].astype(BF16)
    vt_ref[...] = v_ref[0].T.astype(BF16)
    pad = jnp.zeros((HEAD_W - N_META, HEAD_W), F32)
    vmt_ref[...] = jnp.concatenate([vm_ref[...], pad], axis=0).T.astype(BF16)
    km_bf = km_ref[...].astype(BF16)
    lam = _diff_lambda(lp_ref[...])
    causal = qry >= key
    dot = functools.partial(jnp.dot, preferred_element_type=F32)

    def pass_a(qi):
        par = qi % (ATT_AHEAD + 1)
        q = q_ref[0, qi * tq:(qi + 1) * tq, :].astype(F32) * (AT_DQK ** -0.5 * log2e)
        lane = lax.broadcasted_iota(jnp.int32, q.shape, 1)
        qc = (jnp.where(lane < AT_DQK, q, 0.0).astype(BF16), jnp.where(lane >= AT_DQK, q, 0.0).astype(BF16))
        n_keys = (qi + 1) * tq
        out = []
        for c in range(2):
            s_ref[par, c, :n_keys, :] = _dot_nt(kb_ref[:n_keys, :], qc[c])
            sm = _dot_nt(km_bf, qc[c]) + (bmeta_ref[...] if qi == 0 else far_bias)
            m8 = _colmax(sm)
            for j in range(qi + 1):
                m8 = jnp.maximum(m8, _colmax(biased(s_ref[par, c, j * tq:(j + 1) * tq, :], j, qi))
                                 + (far_bias if j < qi - 1 else 0.0))
            out.append((sm, jnp.max(m8, axis=0, keepdims=True)))
        return out

    def biased(s, j, qi):
        if j == qi:
            return jnp.where(causal, s + bdiag_ref[...], NEG_INF)
        if j == qi - 1:
            return s + bnear_ref[...]
        return s

    def pass_b(qi, stats):
        par = qi % (ATT_AHEAD + 1)
        n_keys = (qi + 1) * tq
        o_c = []
        for c in range(2):
            sm, m = stats[c]
            m_far = m - far_bias
            pm = jnp.exp2(sm - m)
            l8 = _colsum(pm)
            for j in range(qi + 1):
                rows = slice(j * tq, (j + 1) * tq)
                p = jnp.exp2(biased(s_ref[par, c, rows, :], j, qi) - (m_far if j < qi - 1 else m))
                l8 = l8 + _colsum(p)
                p_ref[qi % 2, c, rows, :] = p.astype(BF16)
            acc = dot(vt_ref[:, :n_keys], p_ref[qi % 2, c, :n_keys, :]) + dot(vmt_ref[:, :N_META], pm.astype(BF16))
            o_c.append(acc / jnp.sum(l8, axis=0, keepdims=True))
        o_ref[0, qi * tq:(qi + 1) * tq, :] = (o_c[0] - lam * o_c[1]).T

    stats = [pass_a(qi) for qi in range(min(ATT_AHEAD, n_q))]
    for qi in range(n_q):
        if qi + ATT_AHEAD < n_q:
            stats.append(pass_a(qi + ATT_AHEAD))
        pass_b(qi, stats.pop(0))

    _assemble_kv_block(ka_ref, (kl0_ref, kl1_ref), kma_ref, ko_ref, h == 0)
    _assemble_kv_block(va_ref, (vl0_ref, vl1_ref), vma_ref, vo_ref, h == 0)


def _attn_prompt_call_parts(z_lo, z_kv, z_small, rel_bias, diff_lambda, batch, seq, heads):
    assert CONST_BIAS_DIST <= ATT_TQ + 1 and CONST_BIAS_DIST <= N_META + 1 + ATT_TQ - N_META
    zlo = z_lo.reshape(z_lo.shape[0], batch, seq, GROUP_W)
    zkv = z_kv.reshape(z_kv.shape[0], batch, seq, GROUP_W)
    tq = ATT_TQ
    full = lambda slot: pl.BlockSpec((None, 1, seq, HEAD_W), lambda h, b, slot=slot: (slot, b, 0, h))
    meta = lambda g: pl.BlockSpec((None, N_META, HEAD_W), lambda h, b, g=g: (g, 0, h))

    tokens = N_META + seq
    blk = -(-tokens // (8 * heads)) * 8
    assert N_META == 16 and (heads - 1) * blk < seq
    asm_main = lambda g: pl.BlockSpec((None, 1, blk, GROUP_W), lambda h, b, g=g: (KV_SLOT[g], b, h, 0))
    asm_lead = lambda g, piece: pl.BlockSpec(
        (None, 1, 8, GROUP_W),
        lambda h, b, g=g, piece=piece: (KV_SLOT[g], b, jnp.maximum(h * (blk // 8) - 2 + piece, 0), 0))
    asm_meta = lambda g: pl.BlockSpec((None, N_META, GROUP_W), lambda h, b, g=g: (g, 0, 0))
    asm_in = lambda g: [asm_main(g), asm_lead(g, 0), asm_lead(g, 1), asm_meta(g)]
    asm_out = pl.BlockSpec((1, blk, heads, HEAD_W), lambda h, b: (b, h, 0, 0))
    kv_shape = jax.ShapeDtypeStruct((batch, tokens, heads, HEAD_W), F32)

    in_specs = ([pl.BlockSpec(memory_space=pltpu.SMEM),
                 pl.BlockSpec(diff_lambda.shape, lambda h, b: (0, 0)),
                 full(LO_SLOT[G_AQ]), full(KV_SLOT[G_AK]), full(KV_SLOT[G_AV]), meta(G_AK), meta(G_AV)]
                + asm_in(G_AK) + asm_in(G_AV))
    operands = [rel_bias.reshape(-1), diff_lambda, zlo, zkv, zkv, z_small, z_small,
                zkv, zkv, zkv, z_small, zkv, zkv, zkv, z_small]
    out_specs = [pl.BlockSpec((1, seq, HEAD_W), lambda h, b: (b, 0, h)), asm_out, asm_out]
    out_shapes = [jax.ShapeDtypeStruct((batch, seq, heads * HEAD_W), F32), kv_shape, kv_shape]
    scratch = [pltpu.VMEM((seq, HEAD_W), BF16), pltpu.VMEM((HEAD_W, seq), BF16),
               pltpu.VMEM((HEAD_W, HEAD_W), BF16),
               pltpu.VMEM((ATT_AHEAD + 1, 2, seq, tq), F32), pltpu.VMEM((2, 2, seq, tq), BF16),
               pltpu.VMEM((tq, tq), F32), pltpu.VMEM((tq, tq), F32),
               pltpu.VMEM((N_META, tq), F32)]
    return in_specs, operands, out_specs, out_shapes, scratch


def _prompt_mixers(z_lo, z_kv, z_small, hgrn_lb, rel_bias, diff_lambda, batch, seq, heads):
    hg = _hgrn_prompt_call_parts(z_lo, z_small, hgrn_lb, batch, seq, heads)
    at = _attn_prompt_call_parts(z_lo, z_kv, z_small, rel_bias, diff_lambda, batch, seq, heads)
    n_in, n_out = (len(hg[0]), len(at[0])), (len(hg[2]), len(at[2]))

    def body(*refs):
        ins, outs, scr = refs[:sum(n_in)], refs[sum(n_in):sum(n_in) + sum(n_out)], refs[sum(n_in) + sum(n_out):]
        n_hg_scr = len(hg[4])
        _attn_prompt_kernel(*ins[n_in[0]:], *outs[n_out[0]:], *scr[n_hg_scr:])
        _hgrn_prompt_kernel(*ins[:n_in[0]], *outs[:n_out[0]], *scr[:n_hg_scr])

    return pl.pallas_call(
        body,
        grid=(heads, batch),
        in_specs=hg[0] + at[0],
        out_specs=hg[2] + at[2],
        out_shape=hg[3] + at[3],
        scratch_shapes=hg[4] + at[4],
        compiler_params=pltpu.CompilerParams(dimension_semantics=("arbitrary", "arbitrary"),
                                             vmem_limit_bytes=VMEM_LIMIT),
        name="prompt_mixers",
    )(*hg[1], *at[1])


def _head_rmsnorm(o, g):
    parts = []
    for hh in range(o.shape[1] // HEAD_W):
        oh = o[:, hh * HEAD_W:(hh + 1) * HEAD_W]
        ms = jnp.mean(oh * oh, axis=-1, keepdims=True)
        parts.append(oh * lax.rsqrt(ms + RMS_EPS) * g)
    return jnp.concatenate(parts, axis=1)


MERGE_SUB = 128


def _merge_kernel(ohg_ref, hg_ref, oat_ref, ag_ref, x_ref, gh_ref, ga_ref, w_ref, gf_ref, y_ref):
    tm = x_ref.shape[0]
    sub = min(MERGE_SUB, tm)
    subs = [slice(s, s + sub) for s in range(0, tm, sub)]
    mixed = []
    for r in subs:
        mix_h = _head_rmsnorm(ohg_ref[r, :], gh_ref[...]) * _silu(hg_ref[r, :].astype(F32))
        mix_a = (_head_rmsnorm(oat_ref[r, :], ga_ref[...]) * (1.0 - LAMBDA_INIT)
                 * _silu(ag_ref[r, :].astype(F32)))
        mixed.append(jnp.concatenate([mix_h, mix_a], axis=1).astype(BF16))
    proj = [jnp.dot(m, w_ref[...], preferred_element_type=F32) for m in mixed]
    for r, p in zip(subs, proj):
        hres = x_ref[r, :] + p
        ms = jnp.mean(hres * hres, axis=-1, keepdims=True)
        y_ref[r, :] = hres * lax.rsqrt(ms + RMS_EPS) * gf_ref[...]


def _merge(o_hg, o_at, z, slot_hg, slot_ag, x, g_hg, g_at, w_out_bf, g_final, tm, heads):
    rows, d = x.shape
    gw = heads * HEAD_W
    row_blk = lambda w: pl.BlockSpec((tm, w), lambda i: (i, 0))
    gate = lambda slot: pl.BlockSpec((None, tm, gw), lambda i, slot=slot: (slot, i, 0))
    const = lambda shape: pl.BlockSpec(shape, lambda i: (0, 0))
    return pl.pallas_call(
        _merge_kernel,
        grid=(rows // tm,),
        in_specs=[row_blk(gw), gate(slot_hg), row_blk(gw), gate(slot_ag), row_blk(d),
                  const((1, HEAD_W)), const((1, HEAD_W)),
                  pl.BlockSpec(w_out_bf.shape, lambda i: (0, 0), pipeline_mode=pl.Buffered(1)),
                  const((1, d))],
        out_specs=row_blk(d),
        out_shape=jax.ShapeDtypeStruct((rows, d), F32),
        compiler_params=pltpu.CompilerParams(dimension_semantics=("arbitrary",),
                                             vmem_limit_bytes=VMEM_LIMIT),
        name="merge",
    )(o_hg, z, o_at, z, x, g_hg, g_at, w_out_bf, g_final)


def _transpose_rows_to_cols(x):
    pad = jnp.zeros((HEAD_W - x.shape[0], HEAD_W), F32)
    return jnp.concatenate([x, pad], axis=0).T


def _hgrn_step_kernel(q_ref, f_ref, v_ref, lbp_ref, s_ref, w_ref, o_ref, sn_ref, wb_ref):
    wb_ref[...] = w_ref[...].astype(BF16)
    lb = _forget_lower_bound(lbp_ref[...])
    f = lb + (1.0 - lb) * _sigmoid(f_ref[...])
    v = v_ref[...]
    f_t = _transpose_rows_to_cols(f)
    q_t = _transpose_rows_to_cols(q_ref[...])
    for i in range(q_ref.shape[0]):
        fi = jnp.broadcast_to(f_t[:, i:i + 1], (HEAD_W, HEAD_W))
        s_new = fi * s_ref[0, i, 0] + (1.0 - fi) * v[i:i + 1]
        sn_ref[0, i, 0] = s_new
        o_ref[i:i + 1, :] = jnp.sum(q_t[:, i:i + 1] * s_new, axis=0, keepdims=True)


def _hgrn_step(z_samp, hgrn_lb, state, w_out, heads):
    db = z_samp.shape[1]
    d = w_out.shape[0]
    col = lambda g: pl.BlockSpec((None, db, HEAD_W), lambda h, g=g: (g, 0, h))
    st_spec = pl.BlockSpec((1, db, 1, HEAD_W, HEAD_W), lambda h: (0, 0, h, 0, 0))
    w_spec = pl.BlockSpec((d // heads, w_out.shape[1]), lambda h: (h, 0))
    return pl.pallas_call(
        _hgrn_step_kernel,
        grid=(heads,),
        in_specs=[col(G_HQ), col(G_HF), col(G_HI),
                  pl.BlockSpec((hgrn_lb.shape[0], HEAD_W), lambda h: (0, h)), st_spec, w_spec],
        out_specs=[pl.BlockSpec((db, HEAD_W), lambda h: (0, h)), st_spec, w_spec],
        out_shape=[jax.ShapeDtypeStruct((db, heads * HEAD_W), F32),
                   jax.ShapeDtypeStruct(state.shape, F32),
                   jax.ShapeDtypeStruct(w_out.shape, BF16)],
        compiler_params=pltpu.CompilerParams(dimension_semantics=("arbitrary",),
                                             vmem_limit_bytes=VMEM_LIMIT),
        name="hgrn_step",
    )(z_samp, z_samp, z_samp, hgrn_lb, state, w_out)


DEC_PAGES_PER_STEP = 8
DEC_RING = 3


def _page_copies(pt_ref, ck_ref, cv_ref, kbuf_ref, vbuf_ref, sem_ref, step, n_steps):
    npp = DEC_PAGES_PER_STEP
    slot = step % DEC_RING
    seq = step // n_steps
    first_page = (step % n_steps) * npp
    copies = []
    for p in range(npp):
        phys = pt_ref[seq, first_page + p]
        copies.append(pltpu.make_async_copy(ck_ref.at[phys], kbuf_ref.at[slot, p], sem_ref.at[slot]))
        copies.append(pltpu.make_async_copy(cv_ref.at[phys], vbuf_ref.at[slot, p], sem_ref.at[slot]))
    return copies


def _proj_decode_kernel(pt_ref, x_ref, g_ref, w_ref, rbt_ref, lp_ref, q_ref, kn_ref, vn_ref, ck_ref, cv_ref,
                        z_ref, zkv_ref, o_ref, xn_ref, ztile_ref, qs_ref, m_ref, l_ref, acc_ref, blast_ref,
                        kbuf_ref, vbuf_ref, sem_ref, *, heads, page, n_steps):
    npp = DEC_PAGES_PER_STEP
    col_tile = pl.program_id(1)
    row_sub = pl.program_id(2)
    step = (pl.program_id(0) * pl.num_programs(1) + col_tile) * pl.num_programs(2) + row_sub
    total_steps = pl.num_programs(0) * pl.num_programs(1) * pl.num_programs(2)
    copies_for = lambda s: _page_copies(pt_ref, ck_ref, cv_ref, kbuf_ref, vbuf_ref, sem_ref, s, n_steps)

    def start_all(copies):
        for cp in copies:
            cp.start()

    @pl.when(step == 0)
    def _():
        for s in range(DEC_RING - 1):
            start_all(copies_for(s))

    @pl.when(step + (DEC_RING - 1) < total_steps)
    def _():
        start_all(copies_for(step + (DEC_RING - 1)))

    @pl.when(col_tile == 0)
    def _():
        x = x_ref[...]
        ms = jnp.mean(x * x, axis=-1, keepdims=True)
        xn_ref[row_sub] = (x * lax.rsqrt(ms + RMS_EPS) * g_ref[...]).astype(BF16)

    for cp in copies_for(step):
        cp.wait()
    slot = step % DEC_RING
    k_refs = [kbuf_ref.at[slot, p] for p in range(npp)]
    v_refs = [vbuf_ref.at[slot, p] for p in range(npp)]

    j = step % n_steps
    rows = 2 * heads
    lanes = page * heads
    rbt = jnp.concatenate([rbt_ref[...], rbt_ref[...]], axis=0)
    far_bias = rbt[:, N_BUCKETS - 1:N_BUCKETS]
    r_id = lax.broadcasted_iota(jnp.int32, (rows, lanes), 0)
    l_id = lax.broadcasted_iota(jnp.int32, (rows, lanes), 1)
    own_head = l_id % heads == r_id % heads

    def stacked(x):
        lane = lax.broadcasted_iota(jnp.int32, x.shape, 1)
        return jnp.concatenate([jnp.where(lane < AT_DQK, x, 0.0), jnp.where(lane >= AT_DQK, x, 0.0)], axis=0)

    @pl.when(j == 0)
    def _():
        qs_ref[...] = stacked(q_ref[0] * (AT_DQK ** -0.5)).astype(BF16)
        m_ref[...] = jnp.full(m_ref.shape, NEG_INF, F32)
        l_ref[...] = jnp.zeros(l_ref.shape, F32)
        acc_ref[...] = jnp.zeros(acc_ref.shape, F32)
        blast_ref[...] = _bias_by_distance(page - l_id // heads, lambda bkt: rbt[:, bkt:bkt + 1])

    qs = qs_ref[...]
    s_parts = []
    for i in range(npp):
        s_i = _dot_nt(qs, k_refs[i][...].astype(BF16))
        if i == npp - 1:
            s_i = s_i + jnp.where(j == n_steps - 1, blast_ref[...], far_bias)
        else:
            s_i = s_i + far_bias
        s_parts.append(jnp.where(own_head, s_i, NEG_INF))

    m_old = m_ref[...]
    m_blk = s_parts[0]
    for s_i in s_parts[1:]:
        m_blk = jnp.maximum(m_blk, s_i)
    m_new = jnp.maximum(m_old, jnp.max(m_blk, axis=-1, keepdims=True))
    alpha = jnp.exp(m_old - m_new)

    ztile_ref[...] = jnp.dot(xn_ref[row_sub], w_ref[...], preferred_element_type=F32)

    l_blk = jnp.zeros((rows, lanes), F32)
    pv = jnp.zeros(acc_ref.shape, F32)
    for i in range(npp):
        p_i = jnp.exp(s_parts[i] - m_new)
        l_blk = l_blk + p_i
        pv = pv + jnp.dot(p_i.astype(BF16), v_refs[i][...].astype(BF16), preferred_element_type=F32)
    l_ref[...] = alpha * l_ref[...] + jnp.sum(l_blk, axis=-1, keepdims=True)
    acc_ref[...] = alpha * acc_ref[...] + pv
    m_ref[...] = m_new

    @pl.when(j == n_steps - 1)
    def _():
        kn = jnp.concatenate([kn_ref[0], kn_ref[0]], axis=0)
        vn = jnp.concatenate([vn_ref[0], vn_ref[0]], axis=0)
        s_new = jnp.sum(qs.astype(F32) * kn, axis=-1, keepdims=True) + rbt[:, 0:1]
        m_old = m_ref[...]
        m_fin = jnp.maximum(m_old, s_new)
        alpha = jnp.exp(m_old - m_fin)
        p_new = jnp.exp(s_new - m_fin)
        l_fin = alpha * l_ref[...] + p_new
        o = (alpha * acc_ref[...] + p_new * vn) / l_fin
        o_ref[0] = o[:heads] - _diff_lambda(lp_ref[...]) * o[heads:]

    group = col_tile // (GROUP_W // z_ref.shape[1])
    is_kv = (group == G_AK) | (group == G_AV)

    @pl.when(is_kv)
    def _():
        zkv_ref[...] = ztile_ref[...]

    @pl.when(jnp.logical_not(is_kv))
    def _():
        z_ref[...] = ztile_ref[...].astype(BF16)


PD_TM = 1024
PD_TN = 256
PD_ROW_TILES_PER_W = 2


def _proj_decode(x, g, w_bf, q_s, k_s, v_s, cache_k, cache_v, page_table, rel_bias, diff_lambda):
    assert (G_AK, G_AV) == (G_AQ + 1, G_AQ + 2)
    rows, d = x.shape
    n = w_bf.shape[1]
    db, n_pages = page_table.shape
    n_phys, page, heads, _ = cache_k.shape
    npp = DEC_PAGES_PER_STEP
    n_steps = n_pages // npp
    nr = PD_ROW_TILES_PER_W
    grid = (rows // (PD_TM * nr), n // PD_TN, nr)
    tiles_per_group = GROUP_W // PD_TN
    assert CONST_BIAS_DIST <= page + 1 and n_pages % npp == 0 and GROUP_W % PD_TN == 0
    assert rows % (PD_TM * nr) == 0 and n % PD_TN == 0 and grid[0] * grid[1] * nr == db * n_steps
    ck = cache_k.reshape(n_phys, page * heads, HEAD_W)
    cv = cache_v.reshape(n_phys, page * heads, HEAD_W)
    seq_of = lambda p, t, r: ((p * grid[1] + t) * nr + r) // n_steps
    vec = pl.BlockSpec((1, heads, HEAD_W), lambda p, t, r, pt: (seq_of(p, t, r), 0, 0))
    hbm = pl.BlockSpec(memory_space=pl.ANY)
    srows = 2 * heads

    x_block = lambda p, t, r, pt: (jnp.where(t == 0, p * nr + r, p * nr + nr - 1), 0)

    last = tiles_per_group - 1

    def lo_block(p, t, r, pt):
        grp, col = t // tiles_per_group, t % tiles_per_group
        kv = (grp == G_AK) | (grp == G_AV)
        slot = jnp.where(grp < G_AK, grp, jnp.where(grp > G_AV, grp - 2, G_AK - 1))
        return slot, jnp.where(kv, p * nr + nr - 1, p * nr + r), jnp.where(kv, last, col)

    def kv_block(p, t, r, pt):
        grp, col = t // tiles_per_group, t % tiles_per_group
        slot = jnp.clip(grp - G_AK, 0, 1)
        row = jnp.where(grp < G_AK, p * nr, jnp.where(grp > G_AV, p * nr + nr - 1, p * nr + r))
        return slot, row, jnp.where(grp < G_AK, 0, jnp.where(grp > G_AV, last, col))

    kernel = functools.partial(_proj_decode_kernel, heads=heads, page=page, n_steps=n_steps)
    return pl.pallas_call(
        kernel,
        grid_spec=pltpu.PrefetchScalarGridSpec(
            num_scalar_prefetch=1,
            grid=grid,
            in_specs=[pl.BlockSpec((PD_TM, d), x_block),
                      pl.BlockSpec((1, d), lambda p, t, r, pt: (0, 0)),
                      pl.BlockSpec((d, PD_TN), lambda p, t, r, pt: (0, t)),
                      pl.BlockSpec((heads, N_BUCKETS), lambda p, t, r, pt: (0, 0)),
                      pl.BlockSpec(diff_lambda.shape, lambda p, t, r, pt: (0, 0)),
                      vec, vec, vec, hbm, hbm],
            out_specs=[pl.BlockSpec((None, PD_TM, PD_TN), lo_block),
                       pl.BlockSpec((None, PD_TM, PD_TN), kv_block),
                       pl.BlockSpec((1, heads, HEAD_W), lambda p, t, r, pt: (seq_of(p, t, r), 0, 0))],
            scratch_shapes=[pltpu.VMEM((nr, PD_TM, d), BF16), pltpu.VMEM((PD_TM, PD_TN), F32),
                            pltpu.VMEM((srows, HEAD_W), BF16), pltpu.VMEM((srows, 1), F32),
                            pltpu.VMEM((srows, 1), F32), pltpu.VMEM((srows, HEAD_W), F32),
                            pltpu.VMEM((srows, page * heads), F32),
                            pltpu.VMEM((DEC_RING, npp, page * heads, HEAD_W), F32),
                            pltpu.VMEM((DEC_RING, npp, page * heads, HEAD_W), F32),
                            pltpu.SemaphoreType.DMA((DEC_RING,))]),
        out_shape=[jax.ShapeDtypeStruct((n // GROUP_W - 2, rows, GROUP_W), BF16),
                   jax.ShapeDtypeStruct((2, rows, GROUP_W), F32),
                   jax.ShapeDtypeStruct((db, heads, HEAD_W), F32)],
        compiler_params=pltpu.CompilerParams(dimension_semantics=("arbitrary", "arbitrary", "arbitrary"),
                                             vmem_limit_bytes=VMEM_LIMIT),
        name="proj_decode",
    )(page_table, x, g, w_bf, rel_bias.T, diff_lambda, q_s, k_s, v_s, ck, cv)


def kernel(x_prompt, x_sample, cache_k, cache_v, state_hgrn, page_table, meta_tokens, rel_bias, hgrn_lb,
           norm_g, w_in, hgrn_norm_g, diff_norm_g, diff_lambda, w_out, final_norm_g):
    batch, seq, d = x_prompt.shape
    db = x_sample.shape[0]
    heads = cache_k.shape[3]
    gw = heads * HEAD_W
    assert norm_g.shape[0] == 1 and x_sample.shape[1] == 1, "one layer, one sample token per sequence"
    assert gw == GROUP_W

    g_in = norm_g[0].reshape(1, d)
    g_final = final_norm_g.reshape(1, d)
    g_hg = hgrn_norm_g[0].reshape(1, HEAD_W)
    g_at = diff_norm_g[0].reshape(1, HEAD_W)
    lam_p = diff_lambda[0]

    xp = x_prompt.reshape(batch * seq, d)
    xs = x_sample.reshape(db, d)
    small = jnp.concatenate([meta_tokens.astype(F32), xs], axis=0)

    z_small, w_in_bf = _proj_small(small, g_in, w_in[0])
    z_samp = z_small[:, N_META:]
    samp_heads = lambda g: z_samp[g].reshape(db, heads, HEAD_W)
    q_s, k_s, v_s = samp_heads(G_AQ), samp_heads(G_AK), samp_heads(G_AV)
    o_hg_s, s_s, w_out_bf = _hgrn_step(z_samp, hgrn_lb, state_hgrn, w_out[0], heads)

    z_lo, z_kv, o_at_s = _proj_decode(xp, g_in, w_in_bf, q_s, k_s, v_s, cache_k[0], cache_v[0], page_table,
                                      rel_bias, lam_p)

    o_hg_p, s_p, o_at_p, k_p, v_p = _prompt_mixers(z_lo, z_kv, z_small, hgrn_lb, rel_bias, lam_p,
                                                   batch, seq, heads)
    y_p = _merge(o_hg_p.reshape(batch * seq, gw), o_at_p.reshape(batch * seq, gw), z_lo, LO_SLOT[G_HG],
                 LO_SLOT[G_AG], xp, g_hg, g_at, w_out_bf, g_final, tm=512, heads=heads)

    y_s = _merge(o_hg_s, o_at_s.reshape(db, gw), z_samp, G_HG, G_AG, xs, g_hg, g_at, w_out_bf, g_final,
                 tm=db, heads=heads)

    return (y_p.reshape(batch, seq, d), y_s.reshape(db, 1, d), k_p[None], v_p[None],
            s_p[None], k_s.reshape(1, db, 1, heads, HEAD_W), v_s.reshape(1, db, 1, heads, HEAD_W), s_s)
```

```python
import functools
import math

import numpy as np
import jax
import jax.numpy as jnp
from jax import lax
from jax.experimental import pallas as pl
from jax.experimental.pallas import tpu as pltpu

F32 = jnp.float32
BF16 = jnp.bfloat16

N_META = 16
HEAD_W = 128
AT_DQK = 64
N_BUCKETS = 32
MAX_DISTANCE = 128
MAX_EXACT = N_BUCKETS // 2
HGRN_CHUNK = 64
RMS_EPS = 1e-6
NEG_INF = -1e30
LAMBDA_INIT = 0.8 - 0.6 * math.exp(-0.3 * 0)
G_HQ, G_HF, G_HI, G_HG, G_AQ, G_AK, G_AV, G_AG = range(8)
KV_SLOT = {G_AK: 0, G_AV: 1}
LO_SLOT = {g: (g if g < G_AK else g - 2) for g in range(8) if g not in KV_SLOT}

VMEM_LIMIT = 58 * 1024 * 1024


def _bucket_thresholds():
    n = np.arange(0, 2 * MAX_DISTANCE)
    nl = np.maximum(n, MAX_EXACT).astype(np.float32)
    large = MAX_EXACT + (np.log(nl / np.float32(MAX_EXACT)) / np.float32(math.log(MAX_DISTANCE / MAX_EXACT))
                         * np.float32(N_BUCKETS - MAX_EXACT)).astype(np.int32)
    bucket = np.where(n < MAX_EXACT, n, np.minimum(large, N_BUCKETS - 1))
    return [int(np.argmax(bucket >= b)) for b in range(N_BUCKETS)]


BUCKET_THR = _bucket_thresholds()
CONST_BIAS_DIST = BUCKET_THR[N_BUCKETS - 1]


def _sigmoid(x):
    return 1.0 / (1.0 + jnp.exp(-x))


def _silu(x):
    return x * _sigmoid(x)


def _dot_nt(a, b):
    return lax.dot_general(a, b, (((1,), (1,)), ((), ())), preferred_element_type=F32)


def _dot_tn(a, b):
    return lax.dot_general(a, b, (((0,), (0,)), ((), ())), preferred_element_type=F32)


GROUP_W = 1024


def _proj_small_kernel(x_ref, g_ref, w_ref, z_ref, wb_ref, xn_ref):
    @pl.when(pl.program_id(0) == 0)
    def _():
        x = x_ref[...]
        ms = jnp.mean(x * x, axis=-1, keepdims=True)
        xn_ref[...] = (x * lax.rsqrt(ms + RMS_EPS) * g_ref[...]).astype(BF16)

    w_bf = w_ref[...].astype(BF16)
    wb_ref[...] = w_bf
    z_ref[...] = jnp.dot(xn_ref[...], w_bf, preferred_element_type=F32)


def _proj_small(x, g, w):
    rows, d = x.shape
    n = w.shape[1]
    return pl.pallas_call(
        _proj_small_kernel,
        grid=(n // GROUP_W,),
        in_specs=[pl.BlockSpec((rows, d), lambda j: (0, 0)),
                  pl.BlockSpec((1, d), lambda j: (0, 0)),
                  pl.BlockSpec((d, GROUP_W), lambda j: (0, j))],
        out_specs=[pl.BlockSpec((None, rows, GROUP_W), lambda j: (j, 0, 0)),
                   pl.BlockSpec((d, GROUP_W), lambda j: (0, j))],
        out_shape=[jax.ShapeDtypeStruct((n // GROUP_W, rows, GROUP_W), F32),
                   jax.ShapeDtypeStruct((d, n), BF16)],
        scratch_shapes=[pltpu.VMEM((rows, d), BF16)],
        compiler_params=pltpu.CompilerParams(dimension_semantics=("arbitrary",),
                                             vmem_limit_bytes=VMEM_LIMIT),
        name="proj_small",
    )(x, g, w)


def _forget_lower_bound(lbp):
    e = jnp.exp(lbp - jnp.max(lbp, axis=0, keepdims=True))
    return e[0:1] / jnp.sum(e, axis=0, keepdims=True)


def _cumsum_rows(g, tri_bf):
    g1 = g.astype(BF16)
    r1 = g - g1.astype(F32)
    g2 = r1.astype(BF16)
    g3 = (r1 - g2.astype(F32)).astype(BF16)
    dot = functools.partial(jnp.dot, preferred_element_type=F32)
    return dot(tri_bf, g1) + dot(tri_bf, g2) + dot(tri_bf, g3)


def _hgrn_chunk_local(q, hf, v, lb):
    c = q.shape[0]
    row = lax.broadcasted_iota(jnp.int32, (c, c), 0)
    col = lax.broadcasted_iota(jnp.int32, (c, c), 1)
    causal = row >= col
    f = lb + (1.0 - lb) * _sigmoid(hf)
    k = 1.0 - f
    b = _cumsum_rows(jnp.log(f), causal.astype(BF16))
    qe = (q * jnp.exp(b)).astype(BF16)
    ke = (k * jnp.exp(-b)).astype(BF16)
    a = jnp.where(causal, _dot_nt(qe, ke), 0.0)
    v_bf = v.astype(BF16)
    intra = jnp.dot(a.astype(BF16), v_bf, preferred_element_type=F32)
    b_last = b[c - 1:c]
    kd = (k * jnp.exp(b_last - b)).astype(BF16)
    return qe, intra, _dot_tn(v_bf, kd), jnp.exp(b_last)


HGRN_SLAB = 256
HGRN_PAIR = 128


def _hgrn_prompt_kernel(q_ref, f_ref, v_ref, qm_ref, fm_ref, vm_ref, lbp_ref, o_ref, s_ref,
                        b_ref, qe_ref, ke_ref, kd_ref, vb_ref, vt_ref, a_ref, ut_ref, dec_ref, st_ref):
    ck, slab, pair = HGRN_CHUNK, HGRN_SLAB, HGRN_PAIR
    seq = q_ref.shape[1]
    n_chunks = seq // ck
    dot = functools.partial(jnp.dot, preferred_element_type=F32)
    lb = _forget_lower_bound(lbp_ref[...])

    _, _, st, _ = _hgrn_chunk_local(qm_ref[...], fm_ref[...], vm_ref[...], lb)

    f = lb + (1.0 - lb) * _sigmoid(f_ref[0].astype(F32))
    g = jnp.log(f)
    g1 = g.astype(BF16)
    r1 = g - g1.astype(F32)
    g2 = r1.astype(BF16)
    g3 = (r1 - g2.astype(F32)).astype(BF16)
    gcat = jnp.concatenate([g1, g2, g3], axis=1)
    r = lax.broadcasted_iota(jnp.int32, (slab, slab), 0)
    c = lax.broadcasted_iota(jnp.int32, (slab, slab), 1)
    tri_bd = ((r // ck == c // ck) & (r >= c)).astype(BF16)
    for si in range(seq // slab):
        rows = slice(si * slab, (si + 1) * slab)
        t = dot(tri_bd, gcat[rows])
        b_ref[rows, :] = t[:, :HEAD_W] + t[:, HEAD_W:2 * HEAD_W] + t[:, 2 * HEAD_W:]

    k = 1.0 - f
    b = b_ref[...]
    qe_ref[...] = (q_ref[0].astype(F32) * jnp.exp(b)).astype(BF16)
    ke_ref[...] = (k * jnp.exp(-b)).astype(BF16)
    v = v_ref[0].astype(F32)
    vb_ref[...] = v.astype(BF16)
    vt_ref[...] = v.T.astype(BF16)
    for ci in range(n_chunks):
        rows = slice(ci * ck, (ci + 1) * ck)
        b_last = b_ref[(ci + 1) * ck - 1:(ci + 1) * ck, :]
        dec_ref[ci] = jnp.exp(b_last)
        kd_ref[rows, :] = (k[rows] * jnp.exp(b_last - b_ref[rows, :])).astype(BF16)

    rp = lax.broadcasted_iota(jnp.int32, (pair, pair), 0)
    cp = lax.broadcasted_iota(jnp.int32, (pair, pair), 1)
    keep = (rp // ck == cp // ck) & (rp >= cp)
    for pi in range(seq // pair):
        rows = slice(pi * pair, (pi + 1) * pair)
        a_ref[rows, :] = jnp.where(keep, _dot_nt(qe_ref[rows, :], ke_ref[rows, :]), 0.0).astype(BF16)

    for pi in range(seq // pair):
        rows = slice(pi * pair, (pi + 1) * pair)
        o_ref[0, rows, :] = dot(a_ref[rows, :], vb_ref[rows, :])
    for ci in range(n_chunks):
        rows = slice(ci * ck, (ci + 1) * ck)
        ut_ref[ci] = dot(vt_ref[:, rows], kd_ref[rows, :])

    for ci in range(n_chunks):
        st_ref[ci] = st.astype(BF16)
        st = st * dec_ref[ci] + ut_ref[ci]
    s_ref[0, 0] = st.T

    for ci in range(n_chunks):
        rows = slice(ci * ck, (ci + 1) * ck)
        o_ref[0, rows, :] += _dot_nt(qe_ref[rows, :], st_ref[ci])


def _hgrn_prompt(z_lo, z_small, hgrn_lb, batch, seq, heads):
    zp3 = z_lo.reshape(z_lo.shape[0], batch, seq, GROUP_W)
    big = lambda g: pl.BlockSpec((None, 1, seq, HEAD_W), lambda b, h, g=g: (LO_SLOT[g], b, 0, h))
    meta = lambda g: pl.BlockSpec((None, N_META, HEAD_W), lambda b, h, g=g: (g, 0, h))
    return pl.pallas_call(
        _hgrn_prompt_kernel,
        grid=(batch, heads),
        in_specs=[big(G_HQ), big(G_HF), big(G_HI), meta(G_HQ), meta(G_HF), meta(G_HI),
                  pl.BlockSpec((hgrn_lb.shape[0], HEAD_W), lambda b, h: (0, h))],
        out_specs=[pl.BlockSpec((1, seq, HEAD_W), lambda b, h: (b, 0, h)),
                   pl.BlockSpec((1, 1, HEAD_W, HEAD_W), lambda b, h: (b, h, 0, 0))],
        out_shape=[jax.ShapeDtypeStruct((batch, seq, heads * HEAD_W), F32),
                   jax.ShapeDtypeStruct((batch, heads, HEAD_W, HEAD_W), F32)],
        scratch_shapes=[pltpu.VMEM((seq, HEAD_W), F32),
                        pltpu.VMEM((seq, HEAD_W), BF16), pltpu.VMEM((seq, HEAD_W), BF16),
                        pltpu.VMEM((seq, HEAD_W), BF16), pltpu.VMEM((seq, HEAD_W), BF16),
                        pltpu.VMEM((HEAD_W, seq), BF16), pltpu.VMEM((seq, HGRN_PAIR), BF16),
                        pltpu.VMEM((seq // HGRN_CHUNK, HEAD_W, HEAD_W), F32),
                        pltpu.VMEM((seq // HGRN_CHUNK, 1, HEAD_W), F32),
                        pltpu.VMEM((seq // HGRN_CHUNK, HEAD_W, HEAD_W), BF16)],
        compiler_params=pltpu.CompilerParams(dimension_semantics=("arbitrary", "arbitrary"),
                                             vmem_limit_bytes=VMEM_LIMIT),
        name="hgrn_prompt",
    )(zp3, zp3, zp3, z_small, z_small, z_small, hgrn_lb)


def _diff_lambda(lp):
    s01 = jnp.sum(lp[0:1] * lp[1:2], axis=-1, keepdims=True)
    s23 = jnp.sum(lp[2:3] * lp[3:4], axis=-1, keepdims=True)
    return jnp.exp(s01) - jnp.exp(s23) + LAMBDA_INIT


def _bias_by_distance(dist, bias_of_bucket):
    out = jnp.where(dist >= BUCKET_THR[1], bias_of_bucket(1), bias_of_bucket(0))
    for bkt in range(2, N_BUCKETS):
        out = jnp.where(dist >= BUCKET_THR[bkt], bias_of_bucket(bkt), out)
    return out


ATT_TQ = 256
ATT_AHEAD = 2


def _colmax(s):
    return jnp.max(s.reshape(s.shape[0] // 8, 8, s.shape[1]), axis=0)


def _colsum(s):
    return jnp.sum(s.reshape(s.shape[0] // 8, 8, s.shape[1]), axis=0)


def _assemble_kv_block(main_ref, lead_refs, meta_ref, out_ref, first):
    lead = jnp.where(first, meta_ref[...], jnp.concatenate([r[0] for r in lead_refs], axis=0))
    rows = jnp.concatenate([lead, main_ref[0, :out_ref.shape[1] - N_META, :]], axis=0)
    out_ref[...] = rows.reshape(out_ref.shape)


def _attn_prompt_kernel(rb_ref, lp_ref, q_ref, k_ref, v_ref, km_ref, vm_ref,
                        ka_ref, kl0_ref, kl1_ref, kma_ref, va_ref, vl0_ref, vl1_ref, vma_ref,
                        o_ref, ko_ref, vo_ref,
                        kb_ref, vt_ref, vmt_ref, s_ref, p_ref, bdiag_ref, bnear_ref, bmeta_ref):
    h = pl.program_id(0)
    b = pl.program_id(1)
    tq = ATT_TQ
    n_q = q_ref.shape[1] // tq
    log2e = 1.0 / math.log(2.0)
    bias_of_bucket = lambda bkt: rb_ref[bkt * rb_ref.shape[0] // N_BUCKETS + h] * log2e
    far_bias = bias_of_bucket(N_BUCKETS - 1)
    key = lax.broadcasted_iota(jnp.int32, (tq, tq), 0)
    qry = lax.broadcasted_iota(jnp.int32, (tq, tq), 1)

    @pl.when(b == 0)
    def _():
        bdiag_ref[...] = _bias_by_distance(qry - key, bias_of_bucket)
        bnear_ref[...] = _bias_by_distance(qry - key + tq, bias_of_bucket)
        keym = lax.broadcasted_iota(jnp.int32, (N_META, tq), 0)
        qrym = lax.broadcasted_iota(jnp.int32, (N_META, tq), 1)
        bmeta_ref[...] = _bias_by_distance(qrym + N_META - keym, bias_of_bucket)

    kb_ref[...] = k_ref[0].astype(BF16)
    vt_ref[...] = v_ref[0].T.astype(BF16)
    pad = jnp.zeros((HEAD_W - N_META, HEAD_W), F32)
    vmt_ref[...] = jnp.concatenate([vm_ref[...], pad], axis=0).T.astype(BF16)
    km_bf = km_ref[...].astype(BF16)
    lam = _diff_lambda(lp_ref[...])
    causal = qry >= key
    dot = functools.partial(jnp.dot, preferred_element_type=F32)

    def pass_a(qi):
        par = qi % (ATT_AHEAD + 1)
        q = q_ref[0, qi * tq:(qi + 1) * tq, :].astype(F32) * (AT_DQK ** -0.5 * log2e)
        lane = lax.broadcasted_iota(jnp.int32, q.shape, 1)
        qc = (jnp.where(lane < AT_DQK, q, 0.0).astype(BF16), jnp.where(lane >= AT_DQK, q, 0.0).astype(BF16))
        n_keys = (qi + 1) * tq
        out = []
        for c in range(2):
            s_ref[par, c, :n_keys, :] = _dot_nt(kb_ref[:n_keys, :], qc[c])
            sm = _dot_nt(km_bf, qc[c]) + (bmeta_ref[...] if qi == 0 else far_bias)
            m8 = _colmax(sm)
            for j in range(qi + 1):
                m8 = jnp.maximum(m8, _colmax(biased(s_ref[par, c, j * tq:(j + 1) * tq, :], j, qi))
                                 + (far_bias if j < qi - 1 else 0.0))
            out.append((sm, jnp.max(m8, axis=0, keepdims=True)))
        return out

    def biased(s, j, qi):
        if j == qi:
            return jnp.where(causal, s + bdiag_ref[...], NEG_INF)
        if j == qi - 1:
            return s + bnear_ref[...]
        return s

    def pass_b(qi, stats):
        par = qi % (ATT_AHEAD + 1)
        n_keys = (qi + 1) * tq
        o_c = []
        for c in range(2):
            sm, m = stats[c]
            m_far = m - far_bias
            pm = jnp.exp2(sm - m)
            l8 = _colsum(pm)
            for j in range(qi + 1):
                rows = slice(j * tq, (j + 1) * tq)
                p = jnp.exp2(biased(s_ref[par, c, rows, :], j, qi) - (m_far if j < qi - 1 else m))
                l8 = l8 + _colsum(p)
                p_ref[qi % 2, c, rows, :] = p.astype(BF16)
            acc = dot(vt_ref[:, :n_keys], p_ref[qi % 2, c, :n_keys, :]) + dot(vmt_ref[:, :N_META], pm.astype(BF16))
            o_c.append(acc / jnp.sum(l8, axis=0, keepdims=True))
        o_ref[0, qi * tq:(qi + 1) * tq, :] = (o_c[0] - lam * o_c[1]).T

    stats = [pass_a(qi) for qi in range(min(ATT_AHEAD, n_q))]
    for qi in range(n_q):
        if qi + ATT_AHEAD < n_q:
            stats.append(pass_a(qi + ATT_AHEAD))
        pass_b(qi, stats.pop(0))

    _assemble_kv_block(ka_ref, (kl0_ref, kl1_ref), kma_ref, ko_ref, h == 0)
    _assemble_kv_block(va_ref, (vl0_ref, vl1_ref), vma_ref, vo_ref, h == 0)


def _attn_prompt(z_lo, z_kv, z_small, rel_bias, diff_lambda, batch, seq, heads):
    assert CONST_BIAS_DIST <= ATT_TQ + 1 and CONST_BIAS_DIST <= N_META + 1 + ATT_TQ - N_META
    zlo = z_lo.reshape(z_lo.shape[0], batch, seq, GROUP_W)
    zkv = z_kv.reshape(z_kv.shape[0], batch, seq, GROUP_W)
    tq = ATT_TQ
    full = lambda slot: pl.BlockSpec((None, 1, seq, HEAD_W), lambda h, b, slot=slot: (slot, b, 0, h))
    meta = lambda g: pl.BlockSpec((None, N_META, HEAD_W), lambda h, b, g=g: (g, 0, h))

    tokens = N_META + seq
    blk = -(-tokens // (8 * heads)) * 8
    assert N_META == 16 and (heads - 1) * blk < seq
    asm_main = lambda g: pl.BlockSpec((None, 1, blk, GROUP_W), lambda h, b, g=g: (KV_SLOT[g], b, h, 0))
    asm_lead = lambda g, piece: pl.BlockSpec(
        (None, 1, 8, GROUP_W),
        lambda h, b, g=g, piece=piece: (KV_SLOT[g], b, jnp.maximum(h * (blk // 8) - 2 + piece, 0), 0))
    asm_meta = lambda g: pl.BlockSpec((None, N_META, GROUP_W), lambda h, b, g=g: (g, 0, 0))
    asm_in = lambda g: [asm_main(g), asm_lead(g, 0), asm_lead(g, 1), asm_meta(g)]
    asm_out = pl.BlockSpec((1, blk, heads, HEAD_W), lambda h, b: (b, h, 0, 0))
    kv_shape = jax.ShapeDtypeStruct((batch, tokens, heads, HEAD_W), F32)

    return pl.pallas_call(
        _attn_prompt_kernel,
        grid=(heads, batch),
        in_specs=[pl.BlockSpec(memory_space=pltpu.SMEM),
                  pl.BlockSpec(diff_lambda.shape, lambda h, b: (0, 0)),
                  full(LO_SLOT[G_AQ]), full(KV_SLOT[G_AK]), full(KV_SLOT[G_AV]), meta(G_AK), meta(G_AV)]
                 + asm_in(G_AK) + asm_in(G_AV),
        out_specs=[pl.BlockSpec((1, seq, HEAD_W), lambda h, b: (b, 0, h)), asm_out, asm_out],
        out_shape=[jax.ShapeDtypeStruct((batch, seq, heads * HEAD_W), F32), kv_shape, kv_shape],
        scratch_shapes=[pltpu.VMEM((seq, HEAD_W), BF16), pltpu.VMEM((HEAD_W, seq), BF16),
                        pltpu.VMEM((HEAD_W, HEAD_W), BF16),
                        pltpu.VMEM((ATT_AHEAD + 1, 2, seq, tq), F32), pltpu.VMEM((2, 2, seq, tq), BF16),
                        pltpu.VMEM((tq, tq), F32), pltpu.VMEM((tq, tq), F32),
                        pltpu.VMEM((N_META, tq), F32)],
        compiler_params=pltpu.CompilerParams(dimension_semantics=("arbitrary", "arbitrary"),
                                             vmem_limit_bytes=VMEM_LIMIT),
        name="attn_prompt",
    )(rel_bias.reshape(-1), diff_lambda, zlo, zkv, zkv, z_small, z_small,
      zkv, zkv, zkv, z_small, zkv, zkv, zkv, z_small)


def _head_rmsnorm(o, g):
    parts = []
    for hh in range(o.shape[1] // HEAD_W):
        oh = o[:, hh * HEAD_W:(hh + 1) * HEAD_W]
        ms = jnp.mean(oh * oh, axis=-1, keepdims=True)
        parts.append(oh * lax.rsqrt(ms + RMS_EPS) * g)
    return jnp.concatenate(parts, axis=1)


MERGE_SUB = 256


def _merge_kernel(ohg_ref, hg_ref, oat_ref, ag_ref, x_ref, gh_ref, ga_ref, w_ref, gf_ref, y_ref):
    tm = x_ref.shape[0]
    sub = min(MERGE_SUB, tm)
    subs = [slice(s, s + sub) for s in range(0, tm, sub)]
    mixed = []
    for r in subs:
        mix_h = _head_rmsnorm(ohg_ref[r, :], gh_ref[...]) * _silu(hg_ref[r, :].astype(F32))
        mix_a = (_head_rmsnorm(oat_ref[r, :], ga_ref[...]) * (1.0 - LAMBDA_INIT)
                 * _silu(ag_ref[r, :].astype(F32)))
        mixed.append(jnp.concatenate([mix_h, mix_a], axis=1).astype(BF16))
    proj = [jnp.dot(m, w_ref[...], preferred_element_type=F32) for m in mixed]
    for r, p in zip(subs, proj):
        hres = x_ref[r, :] + p
        ms = jnp.mean(hres * hres, axis=-1, keepdims=True)
        y_ref[r, :] = hres * lax.rsqrt(ms + RMS_EPS) * gf_ref[...]


def _merge(o_hg, o_at, z, slot_hg, slot_ag, x, g_hg, g_at, w_out_bf, g_final, tm, heads):
    rows, d = x.shape
    gw = heads * HEAD_W
    row_blk = lambda w: pl.BlockSpec((tm, w), lambda i: (i, 0))
    gate = lambda slot: pl.BlockSpec((None, tm, gw), lambda i, slot=slot: (slot, i, 0))
    const = lambda shape: pl.BlockSpec(shape, lambda i: (0, 0))
    return pl.pallas_call(
        _merge_kernel,
        grid=(rows // tm,),
        in_specs=[row_blk(gw), gate(slot_hg), row_blk(gw), gate(slot_ag), row_blk(d),
                  const((1, HEAD_W)), const((1, HEAD_W)),
                  pl.BlockSpec(w_out_bf.shape, lambda i: (0, 0), pipeline_mode=pl.Buffered(1)),
                  const((1, d))],
        out_specs=row_blk(d),
        out_shape=jax.ShapeDtypeStruct((rows, d), F32),
        compiler_params=pltpu.CompilerParams(dimension_semantics=("arbitrary",),
                                             vmem_limit_bytes=VMEM_LIMIT),
        name="merge",
    )(o_hg, z, o_at, z, x, g_hg, g_at, w_out_bf, g_final)


def _transpose_rows_to_cols(x):
    pad = jnp.zeros((HEAD_W - x.shape[0], HEAD_W), F32)
    return jnp.concatenate([x, pad], axis=0).T


def _hgrn_step_kernel(q_ref, f_ref, v_ref, lbp_ref, s_ref, w_ref, o_ref, sn_ref, wb_ref):
    wb_ref[...] = w_ref[...].astype(BF16)
    lb = _forget_lower_bound(lbp_ref[...])
    f = lb + (1.0 - lb) * _sigmoid(f_ref[...])
    v = v_ref[...]
    f_t = _transpose_rows_to_cols(f)
    q_t = _transpose_rows_to_cols(q_ref[...])
    for i in range(q_ref.shape[0]):
        fi = jnp.broadcast_to(f_t[:, i:i + 1], (HEAD_W, HEAD_W))
        s_new = fi * s_ref[0, i, 0] + (1.0 - fi) * v[i:i + 1]
        sn_ref[0, i, 0] = s_new
        o_ref[i:i + 1, :] = jnp.sum(q_t[:, i:i + 1] * s_new, axis=0, keepdims=True)


def _hgrn_step(z_samp, hgrn_lb, state, w_out, heads):
    db = z_samp.shape[1]
    d = w_out.shape[0]
    col = lambda g: pl.BlockSpec((None, db, HEAD_W), lambda h, g=g: (g, 0, h))
    st_spec = pl.BlockSpec((1, db, 1, HEAD_W, HEAD_W), lambda h: (0, 0, h, 0, 0))
    w_spec = pl.BlockSpec((d // heads, w_out.shape[1]), lambda h: (h, 0))
    return pl.pallas_call(
        _hgrn_step_kernel,
        grid=(heads,),
        in_specs=[col(G_HQ), col(G_HF), col(G_HI),
                  pl.BlockSpec((hgrn_lb.shape[0], HEAD_W), lambda h: (0, h)), st_spec, w_spec],
        out_specs=[pl.BlockSpec((db, HEAD_W), lambda h: (0, h)), st_spec, w_spec],
        out_shape=[jax.ShapeDtypeStruct((db, heads * HEAD_W), F32),
                   jax.ShapeDtypeStruct(state.shape, F32),
                   jax.ShapeDtypeStruct(w_out.shape, BF16)],
        compiler_params=pltpu.CompilerParams(dimension_semantics=("arbitrary",),
                                             vmem_limit_bytes=VMEM_LIMIT),
        name="hgrn_step",
    )(z_samp, z_samp, z_samp, hgrn_lb, state, w_out)


DEC_PAGES_PER_STEP = 8
DEC_RING = 3


def _page_copies(pt_ref, ck_ref, cv_ref, kbuf_ref, vbuf_ref, sem_ref, step, n_steps):
    npp = DEC_PAGES_PER_STEP
    slot = step % DEC_RING
    seq = step // n_steps
    first_page = (step % n_steps) * npp
    copies = []
    for p in range(npp):
        phys = pt_ref[seq, first_page + p]
        copies.append(pltpu.make_async_copy(ck_ref.at[phys], kbuf_ref.at[slot, p], sem_ref.at[slot]))
        copies.append(pltpu.make_async_copy(cv_ref.at[phys], vbuf_ref.at[slot, p], sem_ref.at[slot]))
    return copies


def _proj_decode_kernel(pt_ref, x_ref, g_ref, w_ref, rbt_ref, lp_ref, q_ref, kn_ref, vn_ref, ck_ref, cv_ref,
                        z_ref, zkv_ref, o_ref, xn_ref, ztile_ref, qs_ref, m_ref, l_ref, acc_ref, blast_ref,
                        kbuf_ref, vbuf_ref, sem_ref, *, heads, page, n_steps):
    npp = DEC_PAGES_PER_STEP
    col_tile = pl.program_id(1)
    row_sub = pl.program_id(2)
    step = (pl.program_id(0) * pl.num_programs(1) + col_tile) * pl.num_programs(2) + row_sub
    total_steps = pl.num_programs(0) * pl.num_programs(1) * pl.num_programs(2)
    copies_for = lambda s: _page_copies(pt_ref, ck_ref, cv_ref, kbuf_ref, vbuf_ref, sem_ref, s, n_steps)

    def start_all(copies):
        for cp in copies:
            cp.start()

    @pl.when(step == 0)
    def _():
        for s in range(DEC_RING - 1):
            start_all(copies_for(s))

    @pl.when(step + (DEC_RING - 1) < total_steps)
    def _():
        start_all(copies_for(step + (DEC_RING - 1)))

    @pl.when(col_tile == 0)
    def _():
        x = x_ref[...]
        ms = jnp.mean(x * x, axis=-1, keepdims=True)
        xn_ref[row_sub] = (x * lax.rsqrt(ms + RMS_EPS) * g_ref[...]).astype(BF16)

    for cp in copies_for(step):
        cp.wait()
    slot = step % DEC_RING
    k_refs = [kbuf_ref.at[slot, p] for p in range(npp)]
    v_refs = [vbuf_ref.at[slot, p] for p in range(npp)]

    j = step % n_steps
    rows = 2 * heads
    lanes = page * heads
    rbt = jnp.concatenate([rbt_ref[...], rbt_ref[...]], axis=0)
    far_bias = rbt[:, N_BUCKETS - 1:N_BUCKETS]
    r_id = lax.broadcasted_iota(jnp.int32, (rows, lanes), 0)
    l_id = lax.broadcasted_iota(jnp.int32, (rows, lanes), 1)
    own_head = l_id % heads == r_id % heads

    def stacked(x):
        lane = lax.broadcasted_iota(jnp.int32, x.shape, 1)
        return jnp.concatenate([jnp.where(lane < AT_DQK, x, 0.0), jnp.where(lane >= AT_DQK, x, 0.0)], axis=0)

    @pl.when(j == 0)
    def _():
        qs_ref[...] = stacked(q_ref[0] * (AT_DQK ** -0.5)).astype(BF16)
        m_ref[...] = jnp.full(m_ref.shape, NEG_INF, F32)
        l_ref[...] = jnp.zeros(l_ref.shape, F32)
        acc_ref[...] = jnp.zeros(acc_ref.shape, F32)
        blast_ref[...] = _bias_by_distance(page - l_id // heads, lambda bkt: rbt[:, bkt:bkt + 1])

    qs = qs_ref[...]
    s_parts = []
    for i in range(npp):
        s_i = _dot_nt(qs, k_refs[i][...].astype(BF16))
        if i == npp - 1:
            s_i = s_i + jnp.where(j == n_steps - 1, blast_ref[...], far_bias)
        else:
            s_i = s_i + far_bias
        s_parts.append(jnp.where(own_head, s_i, NEG_INF))

    m_old = m_ref[...]
    m_blk = s_parts[0]
    for s_i in s_parts[1:]:
        m_blk = jnp.maximum(m_blk, s_i)
    m_new = jnp.maximum(m_old, jnp.max(m_blk, axis=-1, keepdims=True))
    alpha = jnp.exp(m_old - m_new)

    ztile_ref[...] = jnp.dot(xn_ref[row_sub], w_ref[...], preferred_element_type=F32)

    l_blk = jnp.zeros((rows, lanes), F32)
    pv = jnp.zeros(acc_ref.shape, F32)
    for i in range(npp):
        p_i = jnp.exp(s_parts[i] - m_new)
        l_blk = l_blk + p_i
        pv = pv + jnp.dot(p_i.astype(BF16), v_refs[i][...].astype(BF16), preferred_element_type=F32)
    l_ref[...] = alpha * l_ref[...] + jnp.sum(l_blk, axis=-1, keepdims=True)
    acc_ref[...] = alpha * acc_ref[...] + pv
    m_ref[...] = m_new

    @pl.when(j == n_steps - 1)
    def _():
        kn = jnp.concatenate([kn_ref[0], kn_ref[0]], axis=0)
        vn = jnp.concatenate([vn_ref[0], vn_ref[0]], axis=0)
        s_new = jnp.sum(qs.astype(F32) * kn, axis=-1, keepdims=True) + rbt[:, 0:1]
        m_old = m_ref[...]
        m_fin = jnp.maximum(m_old, s_new)
        alpha = jnp.exp(m_old - m_fin)
        p_new = jnp.exp(s_new - m_fin)
        l_fin = alpha * l_ref[...] + p_new
        o = (alpha * acc_ref[...] + p_new * vn) / l_fin
        o_ref[0] = o[:heads] - _diff_lambda(lp_ref[...]) * o[heads:]

    group = col_tile // (GROUP_W // z_ref.shape[1])
    is_kv = (group == G_AK) | (group == G_AV)

    @pl.when(is_kv)
    def _():
        zkv_ref[...] = ztile_ref[...]

    @pl.when(jnp.logical_not(is_kv))
    def _():
        z_ref[...] = ztile_ref[...].astype(BF16)


PD_TM = 512
PD_TN = 512
PD_ROW_TILES_PER_W = 8


def _proj_decode(x, g, w_bf, q_s, k_s, v_s, cache_k, cache_v, page_table, rel_bias, diff_lambda):
    assert (G_AK, G_AV) == (G_AQ + 1, G_AQ + 2)
    rows, d = x.shape
    n = w_bf.shape[1]
    db, n_pages = page_table.shape
    n_phys, page, heads, _ = cache_k.shape
    npp = DEC_PAGES_PER_STEP
    n_steps = n_pages // npp
    nr = PD_ROW_TILES_PER_W
    grid = (rows // (PD_TM * nr), n // PD_TN, nr)
    tiles_per_group = GROUP_W // PD_TN
    assert CONST_BIAS_DIST <= page + 1 and n_pages % npp == 0 and GROUP_W % PD_TN == 0
    assert rows % (PD_TM * nr) == 0 and n % PD_TN == 0 and grid[0] * grid[1] * nr == db * n_steps
    ck = cache_k.reshape(n_phys, page * heads, HEAD_W)
    cv = cache_v.reshape(n_phys, page * heads, HEAD_W)
    seq_of = lambda p, t, r: ((p * grid[1] + t) * nr + r) // n_steps
    vec = pl.BlockSpec((1, heads, HEAD_W), lambda p, t, r, pt: (seq_of(p, t, r), 0, 0))
    hbm = pl.BlockSpec(memory_space=pl.ANY)
    srows = 2 * heads

    x_block = lambda p, t, r, pt: (jnp.where(t == 0, p * nr + r, p * nr + nr - 1), 0)

    last = tiles_per_group - 1

    def lo_block(p, t, r, pt):
        grp, col = t // tiles_per_group, t % tiles_per_group
        kv = (grp == G_AK) | (grp == G_AV)
        slot = jnp.where(grp < G_AK, grp, jnp.where(grp > G_AV, grp - 2, G_AK - 1))
        return slot, jnp.where(kv, p * nr + nr - 1, p * nr + r), jnp.where(kv, last, col)

    def kv_block(p, t, r, pt):
        grp, col = t // tiles_per_group, t % tiles_per_group
        slot = jnp.clip(grp - G_AK, 0, 1)
        row = jnp.where(grp < G_AK, p * nr, jnp.where(grp > G_AV, p * nr + nr - 1, p * nr + r))
        return slot, row, jnp.where(grp < G_AK, 0, jnp.where(grp > G_AV, last, col))

    kernel = functools.partial(_proj_decode_kernel, heads=heads, page=page, n_steps=n_steps)
    return pl.pallas_call(
        kernel,
        grid_spec=pltpu.PrefetchScalarGridSpec(
            num_scalar_prefetch=1,
            grid=grid,
            in_specs=[pl.BlockSpec((PD_TM, d), x_block),
                      pl.BlockSpec((1, d), lambda p, t, r, pt: (0, 0)),
                      pl.BlockSpec((d, PD_TN), lambda p, t, r, pt: (0, t)),
                      pl.BlockSpec((heads, N_BUCKETS), lambda p, t, r, pt: (0, 0)),
                      pl.BlockSpec(diff_lambda.shape, lambda p, t, r, pt: (0, 0)),
                      vec, vec, vec, hbm, hbm],
            out_specs=[pl.BlockSpec((None, PD_TM, PD_TN), lo_block),
                       pl.BlockSpec((None, PD_TM, PD_TN), kv_block),
                       pl.BlockSpec((1, heads, HEAD_W), lambda p, t, r, pt: (seq_of(p, t, r), 0, 0))],
            scratch_shapes=[pltpu.VMEM((nr, PD_TM, d), BF16), pltpu.VMEM((PD_TM, PD_TN), F32),
                            pltpu.VMEM((srows, HEAD_W), BF16), pltpu.VMEM((srows, 1), F32),
                            pltpu.VMEM((srows, 1), F32), pltpu.VMEM((srows, HEAD_W), F32),
                            pltpu.VMEM((srows, page * heads), F32),
                            pltpu.VMEM((DEC_RING, npp, page * heads, HEAD_W), F32),
                            pltpu.VMEM((DEC_RING, npp, page * heads, HEAD_W), F32),
                            pltpu.SemaphoreType.DMA((DEC_RING,))]),
        out_shape=[jax.ShapeDtypeStruct((n // GROUP_W - 2, rows, GROUP_W), BF16),
                   jax.ShapeDtypeStruct((2, rows, GROUP_W), F32),
                   jax.ShapeDtypeStruct((db, heads, HEAD_W), F32)],
        compiler_params=pltpu.CompilerParams(dimension_semantics=("arbitrary", "arbitrary", "arbitrary"),
                                             vmem_limit_bytes=VMEM_LIMIT),
        name="proj_decode",
    )(page_table, x, g, w_bf, rel_bias.T, diff_lambda, q_s, k_s, v_s, ck, cv)


def kernel(x_prompt, x_sample, cache_k, cache_v, state_hgrn, page_table, meta_tokens, rel_bias, hgrn_lb,
           norm_g, w_in, hgrn_norm_g, diff_norm_g, diff_lambda, w_out, final_norm_g):
    batch, seq, d = x_prompt.shape
    db = x_sample.shape[0]
    heads = cache_k.shape[3]
    gw = heads * HEAD_W
    assert norm_g.shape[0] == 1 and x_sample.shape[1] == 1, "one layer, one sample token per sequence"
    assert gw == GROUP_W

    g_in = norm_g[0].reshape(1, d)
    g_final = final_norm_g.reshape(1, d)
    g_hg = hgrn_norm_g[0].reshape(1, HEAD_W)
    g_at = diff_norm_g[0].reshape(1, HEAD_W)
    lam_p = diff_lambda[0]

    xp = x_prompt.reshape(batch * seq, d)
    xs = x_sample.reshape(db, d)
    small = jnp.concatenate([meta_tokens.astype(F32), xs], axis=0)

    z_small, w_in_bf = _proj_small(small, g_in, w_in[0])
    z_samp = z_small[:, N_META:]
    samp_heads = lambda g: z_samp[g].reshape(db, heads, HEAD_W)
    q_s, k_s, v_s = samp_heads(G_AQ), samp_heads(G_AK), samp_heads(G_AV)
    o_hg_s, s_s, w_out_bf = _hgrn_step(z_samp, hgrn_lb, state_hgrn, w_out[0], heads)

    z_lo, z_kv, o_at_s = _proj_decode(xp, g_in, w_in_bf, q_s, k_s, v_s, cache_k[0], cache_v[0], page_table,
                                      rel_bias, lam_p)

    o_hg_p, s_p = _hgrn_prompt(z_lo, z_small, hgrn_lb, batch, seq, heads)
    o_at_p, k_p, v_p = _attn_prompt(z_lo, z_kv, z_small, rel_bias, lam_p, batch, seq, heads)
    y_p = _merge(o_hg_p.reshape(batch * seq, gw), o_at_p.reshape(batch * seq, gw), z_lo, LO_SLOT[G_HG],
                 LO_SLOT[G_AG], xp, g_hg, g_at, w_out_bf, g_final, tm=512, heads=heads)

    y_s = _merge(o_hg_s, o_at_s.reshape(db, gw), z_samp, G_HG, G_AG, xs, g_hg, g_at, w_out_bf, g_final,
                 tm=db, heads=heads)

    return (y_p.reshape(batch, seq, d), y_s.reshape(db, 1, d), k_p[None], v_p[None],
            s_p[None], k_s.reshape(1, db, 1, heads, HEAD_W), v_s.reshape(1, db, 1, heads, HEAD_W), s_s)
```
